```python
import math
import jax
import jax.numpy as jnp
from jax import lax
import numpy as np


D_MODEL = 1024
BATCH = 2
SEQ = 16384
DEPTH = 4

GRID_W = 64
CTX_LEN = 256
N_MIXERS = 3
D_FF = 4 * D_MODEL
NORM_EPS = 1e-6

DN_K_HEADS = 8
DN_V_HEADS = 16
DN_HEAD_K = 128
DN_HEAD_V = 128
DN_GROUP = DN_V_HEADS // DN_K_HEADS
DN_K_DIM = DN_K_HEADS * DN_HEAD_K
DN_V_DIM = DN_V_HEADS * DN_HEAD_V
DN_QKV_DIM = 2 * DN_K_DIM + DN_V_DIM
DN_IN_DIM = DN_QKV_DIM + DN_V_DIM + 4 * DN_V_HEADS
SHORT_CONV = 5
DN_CHUNK = 64

GLA_HEADS = 4
GLA_K_DIM = D_MODEL // 2
GLA_V_DIM = D_MODEL
GLA_HEAD_K = GLA_K_DIM // GLA_HEADS
GLA_HEAD_V = GLA_V_DIM // GLA_HEADS
GLA_GATE_RANK = 16
GLA_GATE_NORMALIZER = 16.0
GLA_IN_DIM = 2 * GLA_K_DIM + 2 * GLA_V_DIM + 2 * GLA_GATE_RANK
GLA_CHUNK = 64

ATTN_Q_HEADS = 8
ATTN_KV_HEADS = 2
ATTN_HEAD_DIM = 128
ATTN_GROUP = ATTN_Q_HEADS // ATTN_KV_HEADS
ATTN_Q_DIM = ATTN_Q_HEADS * ATTN_HEAD_DIM
ATTN_KV_DIM = ATTN_KV_HEADS * ATTN_HEAD_DIM
ATTN_IN_DIM = ATTN_Q_DIM + 2 * ATTN_KV_DIM
Q_BLOCK = 128
ROPE_THETA = 10000.0

kernel_name = 'hybrid_dit_deltanet_gla_gqa'

F32 = jnp.float32


def rms_norm(x, g):
    xf = x.astype(F32)
    y = xf * lax.rsqrt(jnp.mean(xf * xf, axis=-1, keepdims=True) + NORM_EPS)
    return (y * g.astype(F32)).astype(x.dtype)


def l2_normalize(x):
    xf = x.astype(F32)
    return xf * lax.rsqrt(jnp.sum(xf * xf, axis=-1, keepdims=True) + NORM_EPS)


def modulate(h, shift, scale):
    return h * (1.0 + scale) + shift


def squared_relu_ffn(h, w1, w2):
    return jnp.square(jax.nn.relu(h @ w1)) @ w2


def short_conv(u, w):
    n = u.shape[1]
    pad = SHORT_CONV // 2
    up = jnp.pad(u, ((0, 0), (pad, pad), (0, 0)))
    out = up[:, 0:n] * w[:, 0]
    for tap in range(1, SHORT_CONV):
        out = out + up[:, tap:tap + n] * w[:, tap]
    return out


def to_chunks(t, chunk):
    b, h, n = t.shape[:3]
    t = t.reshape((b, h, n // chunk, chunk) + t.shape[3:])
    return jnp.moveaxis(t, 2, 0)


def from_chunks(t):
    t = jnp.moveaxis(t, 0, 2)
    b, h, nc, chunk = t.shape[:4]
    return t.reshape((b, h, nc * chunk) + t.shape[4:])


def time_flip(t):
    return jnp.flip(t, axis=2)


def identity(t):
    return t


def axial_rope_tables(n_rows):
    row = jnp.repeat(jnp.arange(n_rows), GRID_W).astype(F32)
    col = jnp.tile(jnp.arange(GRID_W), n_rows).astype(F32)
    axis_dim = ATTN_HEAD_DIM // 2
    inv_freq = jnp.power(ROPE_THETA, -jnp.arange(0, axis_dim, 2, dtype=F32) / axis_dim)
    ang_row = row[:, None] * inv_freq
    ang_col = col[:, None] * inv_freq
    return jnp.cos(ang_row), jnp.sin(ang_row), jnp.cos(ang_col), jnp.sin(ang_col)


def rotate_pairs(x, cos, sin):
    x1, x2 = jnp.split(x, 2, axis=-1)
    cs, sn = cos[:, None, :], sin[:, None, :]
    return jnp.concatenate([x1 * cs - x2 * sn, x1 * sn + x2 * cs], axis=-1)


def apply_axial_rope(x, tables):
    cos_r, sin_r, cos_c, sin_c = tables
    xf = x.astype(F32)
    half = ATTN_HEAD_DIM // 2
    y = jnp.concatenate([rotate_pairs(xf[..., :half], cos_r, sin_r),
                         rotate_pairs(xf[..., half:], cos_c, sin_c)], axis=-1)
    return y.astype(x.dtype)


def gated_delta_rule(q, k, v, g, beta, state):
    c = DN_CHUNK
    tri = jnp.tril(jnp.ones((c, c), dtype=bool))
    strict = jnp.tril(jnp.ones((c, c), dtype=bool), -1)
    eye = jnp.eye(c, dtype=F32)

    def step(s, xs):
        qc, kc, vc, gc, bc = xs
        qc = jnp.repeat(qc, DN_GROUP, axis=1)
        kc = jnp.repeat(kc, DN_GROUP, axis=1)
        gcum = jnp.cumsum(gc, axis=-1)
        decay = jnp.exp(jnp.where(tri, gcum[..., :, None] - gcum[..., None, :], -jnp.inf))
        kk = jnp.einsum('bhid,bhjd->bhij', kc, kc)
        m = jnp.where(strict, bc[..., :, None] * kk * decay, 0.0)
        rhs = jnp.concatenate([vc * bc[..., None], kc * (bc * jnp.exp(gcum))[..., None]], axis=-1)
        sol = lax.linalg.triangular_solve(eye + m, rhs, left_side=True, lower=True, unit_diagonal=True)
        u, w = sol[..., :DN_HEAD_V], sol[..., DN_HEAD_V:]
        v_new = u - jnp.einsum('bhid,bhdv->bhiv', w, s)
        a_qk = jnp.einsum('bhid,bhjd->bhij', qc, kc) * decay
        o = (jnp.einsum('bhid,bhdv->bhiv', qc * jnp.exp(gcum)[..., None], s)
             + jnp.einsum('bhij,bhjv->bhiv', a_qk, v_new))
        tail = jnp.exp(gcum[..., -1:] - gcum)
        s = (s * jnp.exp(gcum[..., -1])[..., None, None]
             + jnp.einsum('bhjd,bhjv->bhdv', kc * tail[..., None], v_new))
        return s, o

    xs = (to_chunks(q, c), to_chunks(k, c), to_chunks(v, c), to_chunks(g, c), to_chunks(beta, c))
    state, o = lax.scan(step, state, xs)
    return from_chunks(o), state


def gla_rule(q, k, v, gk, state):
    c = GLA_CHUNK
    tri = jnp.tril(jnp.ones((c, c), dtype=bool))[:, :, None]

    def step(s, xs):
        qc, kc, vc, gc = xs
        bcum = jnp.cumsum(gc, axis=-2)
        rel = jnp.exp(jnp.where(tri, bcum[..., :, None, :] - bcum[..., None, :, :], -jnp.inf))
        a_qk = jnp.einsum('bhid,bhjd,bhijd->bhij', qc, kc, rel)
        o = (jnp.einsum('bhij,bhjv->bhiv', a_qk, vc)
             + jnp.einsum('bhid,bhdv->bhiv', qc * jnp.exp(bcum), s))
        last = bcum[..., -1:, :]
        s = (s * jnp.exp(last[..., 0, :])[..., None]
             + jnp.einsum('bhjd,bhjv->bhdv', kc * jnp.exp(last - bcum), vc))
        return s, o

    xs = (to_chunks(q, c), to_chunks(k, c), to_chunks(v, c), to_chunks(gk, c))
    state, o = lax.scan(step, state, xs)
    return from_chunks(o), state


def deltanet_mixer(hx, hc, w_in, conv_w, a_log, dt_bias, norm_g, w_out, with_ctx_out):
    def project(h):
        bsz, n, _ = h.shape
        p = h @ w_in
        qkv = jax.nn.silu(short_conv(p[..., :DN_QKV_DIM], conv_w))
        z = p[..., DN_QKV_DIM:DN_QKV_DIM + DN_V_DIM]
        ab = p[..., DN_QKV_DIM + DN_V_DIM:].astype(F32)
        q = l2_normalize(qkv[..., :DN_K_DIM].reshape(bsz, n, DN_K_HEADS, DN_HEAD_K)) * DN_HEAD_K ** -0.5
        k = l2_normalize(qkv[..., DN_K_DIM:2 * DN_K_DIM].reshape(bsz, n, DN_K_HEADS, DN_HEAD_K))
        v = qkv[..., 2 * DN_K_DIM:].reshape(bsz, n, DN_V_HEADS, DN_HEAD_V).astype(F32)
        a = ab[..., :2 * DN_V_HEADS].reshape(bsz, n, 2, DN_V_HEADS)
        b = ab[..., 2 * DN_V_HEADS:].reshape(bsz, n, 2, DN_V_HEADS)
        g = -jnp.exp(a_log.astype(F32)) * jax.nn.softplus(a + dt_bias.astype(F32))
        beta = jax.nn.sigmoid(b)
        return (jnp.swapaxes(q, 1, 2), jnp.swapaxes(k, 1, 2), jnp.swapaxes(v, 1, 2),
                jnp.moveaxis(g, 1, -1), jnp.moveaxis(beta, 1, -1), z)

    qx, kx, vx, gx, bx, zx = project(hx)
    qc, kc, vc, gc, bc, zc = project(hc)
    bsz = hx.shape[0]
    out_x, out_c = [], []
    for d in range(2):
        flip = time_flip if d == 1 else identity
        state0 = jnp.zeros((bsz, DN_V_HEADS, DN_HEAD_K, DN_HEAD_V), F32)
        oc, state_c = gated_delta_rule(flip(qc), flip(kc), flip(vc), flip(gc[:, d]), flip(bc[:, d]), state0)
        ox, _ = gated_delta_rule(flip(qx), flip(kx), flip(vx), flip(gx[:, d]), flip(bx[:, d]), state_c)
        out_x.append(flip(ox))
        out_c.append(flip(oc))

    def readout(o, z):
        bsz_, n = z.shape[:2]
        o = rms_norm(jnp.swapaxes(o, 1, 2), norm_g)
        o = o * jax.nn.silu(z.reshape(bsz_, n, DN_V_HEADS, DN_HEAD_V).astype(F32))
        return o.reshape(bsz_, n, DN_V_DIM).astype(z.dtype) @ w_out

    yx = readout(out_x[0] + out_x[1], zx)
    yc = readout(out_c[0] + out_c[1], zc) if with_ctx_out else None
    return yx, yc


def gla_mixer(hx, hc, w_in, gate_w2, gate_b2, norm_g, w_out, with_ctx_out):
    def project(h):
        bsz, n, _ = h.shape
        p = h @ w_in
        q = p[..., :GLA_K_DIM].reshape(bsz, n, GLA_HEADS, GLA_HEAD_K).astype(F32) * GLA_HEAD_K ** -0.5
        k = p[..., GLA_K_DIM:2 * GLA_K_DIM].reshape(bsz, n, GLA_HEADS, GLA_HEAD_K).astype(F32)
        v = p[..., 2 * GLA_K_DIM:2 * GLA_K_DIM + GLA_V_DIM].reshape(bsz, n, GLA_HEADS, GLA_HEAD_V).astype(F32)
        gate = p[..., 2 * GLA_K_DIM + GLA_V_DIM:2 * GLA_K_DIM + 2 * GLA_V_DIM]
        low = p[..., 2 * GLA_K_DIM + 2 * GLA_V_DIM:].reshape(bsz, n, 2, GLA_GATE_RANK)
        logits = jnp.einsum('btdr,drk->btdk', low, gate_w2) + gate_b2
        gk = jax.nn.log_sigmoid(logits.astype(F32)) / GLA_GATE_NORMALIZER
        gk = gk.reshape(bsz, n, 2, GLA_HEADS, GLA_HEAD_K).transpose(0, 2, 3, 1, 4)
        return jnp.swapaxes(q, 1, 2), jnp.swapaxes(k, 1, 2), jnp.swapaxes(v, 1, 2), gk, gate

    qx, kx, vx, gx, zx = project(hx)
    qc, kc, vc, gc, zc = project(hc)
    bsz = hx.shape[0]
    out_x, out_c = [], []
    for d in range(2):
        flip = time_flip if d == 1 else identity
        state0 = jnp.zeros((bsz, GLA_HEADS, GLA_HEAD_K, GLA_HEAD_V), F32)
        oc, state_c = gla_rule(flip(qc), flip(kc), flip(vc), flip(gc[:, d]), state0)
        ox, _ = gla_rule(flip(qx), flip(kx), flip(vx), flip(gx[:, d]), state_c)
        out_x.append(flip(ox))
        out_c.append(flip(oc))

    def readout(o, z):
        bsz_, n = z.shape[:2]
        o = rms_norm(jnp.swapaxes(o, 1, 2), norm_g)
        o = o * jax.nn.silu(z.reshape(bsz_, n, GLA_HEADS, GLA_HEAD_V).astype(F32))
        return o.reshape(bsz_, n, GLA_V_DIM).astype(z.dtype) @ w_out

    yx = readout(out_x[0] + out_x[1], zx)
    yc = readout(out_c[0] + out_c[1], zc) if with_ctx_out else None
    return yx, yc


def attention_mixer(hx, hc, w_in, q_norm_g, k_norm_g, w_out, rope, with_ctx_out):
    def project(h):
        bsz, n, _ = h.shape
        p = h @ w_in
        q = p[..., :ATTN_Q_DIM].reshape(bsz, n, ATTN_Q_HEADS, ATTN_HEAD_DIM)
        k = p[..., ATTN_Q_DIM:ATTN_Q_DIM + ATTN_KV_DIM].reshape(bsz, n, ATTN_KV_HEADS, ATTN_HEAD_DIM)
        v = p[..., ATTN_Q_DIM + ATTN_KV_DIM:].reshape(bsz, n, ATTN_KV_HEADS, ATTN_HEAD_DIM)
        return rms_norm(q, q_norm_g), rms_norm(k, k_norm_g), v

    qx, kx, vx = project(hx)
    qc, kc, vc = project(hc)
    qx = apply_axial_rope(qx, rope)
    kx = apply_axial_rope(kx, rope)
    bsz, n = hx.shape[:2]
    scale = ATTN_HEAD_DIM ** -0.5

    def attend(q_blk, keys, values):
        s = jnp.einsum('bkgqd,bksd->bkgqs', q_blk, keys).astype(F32) * scale
        p = jax.nn.softmax(s, axis=-1).astype(values.dtype)
        return jnp.einsum('bkgqs,bksd->bkgqd', p, values)

    k_all = jnp.concatenate([kx, kc], axis=1).transpose(0, 2, 1, 3)
    v_all = jnp.concatenate([vx, vc], axis=1).transpose(0, 2, 1, 3)
    n_blk = n // Q_BLOCK
    q_blocks = qx.reshape(bsz, n_blk, Q_BLOCK, ATTN_KV_HEADS, ATTN_GROUP, ATTN_HEAD_DIM).transpose(1, 0, 3, 4, 2, 5)
    o = lax.map(lambda qb: attend(qb, k_all, v_all), q_blocks)
    yx = o.transpose(1, 0, 4, 2, 3, 5).reshape(bsz, n, ATTN_Q_DIM) @ w_out
    yc = None
    if with_ctx_out:
        lc = hc.shape[1]
        q_ctx = qc.reshape(bsz, lc, ATTN_KV_HEADS, ATTN_GROUP, ATTN_HEAD_DIM).transpose(0, 2, 3, 1, 4)
        oc = attend(q_ctx, kc.transpose(0, 2, 1, 3), vc.transpose(0, 2, 1, 3))
        yc = oc.transpose(0, 3, 1, 2, 4).reshape(bsz, lc, ATTN_Q_DIM) @ w_out
    return yx, yc


def setup_inputs(seed: int = 0) -> dict:
    key = jax.random.key(seed)
    ks = jax.random.split(key, 32)
    n_a = len(range(0, DEPTH, N_MIXERS))
    n_b = len(range(1, DEPTH, N_MIXERS))
    n_c = len(range(2, DEPTH, N_MIXERS))

    def normal(k, shape, scale):
        return jax.random.normal(k, shape, F32) * scale

    def gain(k, shape):
        return 1.0 + normal(k, shape, 0.02)

    dt = jnp.exp(jax.random.uniform(ks[12], (n_a, 2, DN_V_HEADS), F32, math.log(1e-3), math.log(1e-1)))
    return {
        'x': normal(ks[0], (BATCH, SEQ, D_MODEL), 1.0),
        'c': normal(ks[1], (BATCH, D_MODEL), 1.0),
        'ctx': normal(ks[2], (BATCH, CTX_LEN, D_MODEL), 1.0),
        'c_ctx': normal(ks[3], (D_MODEL,), 1.0),
        'ada_w': normal(ks[4], (DEPTH, D_MODEL, 6 * D_MODEL), 0.5 * D_MODEL ** -0.5),
        'ada_b': normal(ks[5], (DEPTH, 6 * D_MODEL), 0.01),
        'norm_mix_g': gain(ks[6], (DEPTH, D_MODEL)),
        'norm_ffn_g': gain(ks[7], (DEPTH, D_MODEL)),
        'ffn_w1': normal(ks[8], (DEPTH, D_MODEL, D_FF), D_MODEL ** -0.5),
        'ffn_w2': normal(ks[9], (DEPTH, D_FF, D_MODEL), D_FF ** -0.5),
        'dn_w_in': normal(ks[10], (n_a, D_MODEL, DN_IN_DIM), D_MODEL ** -0.5),
        'dn_conv_w': normal(ks[11], (n_a, DN_QKV_DIM, SHORT_CONV), SHORT_CONV ** -0.5),
        'dn_a_log': jnp.log(jax.random.uniform(ks[13], (n_a, 2, DN_V_HEADS), F32, 1.0, 16.0)),
        'dn_dt_bias': dt + jnp.log(-jnp.expm1(-dt)),
        'dn_norm_g': gain(ks[14], (n_a, DN_HEAD_V)),
        'dn_w_out': normal(ks[15], (n_a, DN_V_DIM, D_MODEL), DN_V_DIM ** -0.5),
        'gla_w_in': normal(ks[16], (n_b, D_MODEL, GLA_IN_DIM), D_MODEL ** -0.5),
        'gla_gate_w2': normal(ks[17], (n_b, 2, GLA_GATE_RANK, GLA_K_DIM), GLA_GATE_RANK ** -0.5),
        'gla_gate_b2': normal(ks[18], (n_b, 2, GLA_K_DIM), 0.1),
        'gla_norm_g': gain(ks[19], (n_b, GLA_HEAD_V)),
        'gla_w_out': normal(ks[20], (n_b, GLA_V_DIM, D_MODEL), GLA_V_DIM ** -0.5),
        'attn_w_in': normal(ks[21], (n_c, D_MODEL, ATTN_IN_DIM), D_MODEL ** -0.5),
        'attn_q_norm_g': gain(ks[22], (n_c, ATTN_HEAD_DIM)),
        'attn_k_norm_g': gain(ks[23], (n_c, ATTN_HEAD_DIM)),
        'attn_w_out': normal(ks[24], (n_c, ATTN_Q_DIM, D_MODEL), ATTN_Q_DIM ** -0.5),
    }


def reference(x, c, ctx, c_ctx, ada_w, ada_b, norm_mix_g, norm_ffn_g, ffn_w1, ffn_w2,
              dn_w_in, dn_conv_w, dn_a_log, dn_dt_bias, dn_norm_g, dn_w_out,
              gla_w_in, gla_gate_w2, gla_gate_b2, gla_norm_g, gla_w_out,
              attn_w_in, attn_q_norm_g, attn_k_norm_g, attn_w_out):
    n_tok = x.shape[1]
    ROWS = n_tok // GRID_W
    rope = axial_rope_tables(ROWS)
    for i in range(DEPTH):
        with_ctx_out = i < DEPTH - 1
        mix, slot = i % N_MIXERS, i // N_MIXERS
        mod_x = jnp.split((jax.nn.silu(c) @ ada_w[i] + ada_b[i])[:, None, :], 6, axis=-1)
        mod_c = jnp.split(jax.nn.silu(c_ctx) @ ada_w[i] + ada_b[i], 6, axis=-1)
        hx = modulate(rms_norm(x, norm_mix_g[i]), mod_x[0], mod_x[1])
        hc = modulate(rms_norm(ctx, norm_mix_g[i]), mod_c[0], mod_c[1])
        if mix == 0:
            yx, yc = deltanet_mixer(hx, hc, dn_w_in[slot], dn_conv_w[slot], dn_a_log[slot], dn_dt_bias[slot],
                                    dn_norm_g[slot], dn_w_out[slot], with_ctx_out)
        elif mix == 1:
            yx, yc = gla_mixer(hx, hc, gla_w_in[slot], gla_gate_w2[slot], gla_gate_b2[slot],
                               gla_norm_g[slot], gla_w_out[slot], with_ctx_out)
        else:
            yx, yc = attention_mixer(hx, hc, attn_w_in[slot], attn_q_norm_g[slot], attn_k_norm_g[slot],
                                     attn_w_out[slot], rope, with_ctx_out)
        x = x + mod_x[2] * yx
        hx = modulate(rms_norm(x, norm_ffn_g[i]), mod_x[3], mod_x[4])
        x = x + mod_x[5] * squared_relu_ffn(hx, ffn_w1[i], ffn_w2[i])
        if with_ctx_out:
            ctx = ctx + mod_c[2] * yc
            hc = modulate(rms_norm(ctx, norm_ffn_g[i]), mod_c[3], mod_c[4])
            ctx = ctx + mod_c[5] * squared_relu_ffn(hc, ffn_w1[i], ffn_w2[i])
    return x
```

```python
import functools
import math

import numpy as np
import jax
import jax.numpy as jnp
from jax import lax
from jax.experimental import pallas as pl
from jax.experimental.pallas import tpu as pltpu

F32 = jnp.float32
BF16 = jnp.bfloat16

NORM_EPS = 1e-6
GRID_W = 64
ROPE_THETA = 10000.0
SHORT_CONV = 5

DN_K_HEADS = 8
DN_V_HEADS = 16
DN_HEAD = 128
DN_GROUP = DN_V_HEADS // DN_K_HEADS
DN_K_DIM = DN_K_HEADS * DN_HEAD
DN_V_DIM = DN_V_HEADS * DN_HEAD
DN_QKV_DIM = 2 * DN_K_DIM + DN_V_DIM

GLA_HEADS = 4
GLA_HEAD_K = 128
GLA_HEAD_V = 256
GLA_K_DIM = GLA_HEADS * GLA_HEAD_K
GLA_V_DIM = GLA_HEADS * GLA_HEAD_V
GLA_GATE_RANK = 16
GLA_GATE_NORMALIZER = 16.0

ATTN_Q_HEADS = 8
ATTN_KV_HEADS = 2
ATTN_HEAD = 128
ATTN_GROUP = ATTN_Q_HEADS // ATTN_KV_HEADS

CHUNK = 64
SCAN_BLOCK = 256
LANES = 128
MOD_ROWS = 8
VMEM_LIMIT = 56 * 1024 * 1024


def _params(semantics, vmem=VMEM_LIMIT):
    return pltpu.CompilerParams(dimension_semantics=semantics, vmem_limit_bytes=vmem)


def _sigmoid(x):
    return 1.0 / (1.0 + jnp.exp(-x))


def _softplus(x):
    return jnp.maximum(x, 0.0) + jnp.log(1.0 + jnp.exp(-jnp.abs(x)))


def _split3(x):
    hi = x.astype(BF16)
    r1 = x - hi.astype(F32)
    mid = r1.astype(BF16)
    lo = (r1 - mid.astype(F32)).astype(BF16)
    return hi, mid, lo


def _dot(a, b):
    return jnp.dot(a.astype(BF16), b.astype(BF16), preferred_element_type=F32)


def _dot_nt(a, b):
    return lax.dot_general(a.astype(BF16), b.astype(BF16), (((1,), (1,)), ((), ())),
                           preferred_element_type=F32)


def _dot_tn(a, b):
    return lax.dot_general(a.astype(BF16), b.astype(BF16), (((0,), (0,)), ((), ())),
                           preferred_element_type=F32)


def _dot_sel(p_bf16, x):
    hi, mid, lo = _split3(x)
    d = lambda y: jnp.dot(p_bf16, y, preferred_element_type=F32)
    return d(hi) + d(mid) + d(lo)


def _dot_x3(a, b):
    ah = a.astype(BF16)
    al = (a - ah.astype(F32)).astype(BF16)
    bh = b.astype(BF16)
    bl = (b - bh.astype(F32)).astype(BF16)
    d = lambda x, y: jnp.dot(x, y, preferred_element_type=F32)
    return d(ah, bh) + d(ah, bl) + d(al, bh)


def _row_tile(total, target, multiple=8):
    best = None
    for t in range(multiple, min(total, target) + 1, multiple):
        if total % t == 0:
            best = t
    assert best is not None, (total, target, multiple)
    return best


def _mod_norm(x, g, mod_ref, b, row0, ctx_len, nb, shift_idx, scale_idx):
    d = x.shape[-1]
    ms = jnp.mean(x * x, axis=-1, keepdims=True)
    y = x * lax.rsqrt(ms + NORM_EPS) * g
    rows = row0 + lax.broadcasted_iota(jnp.int32, (x.shape[0], 1), 0)
    is_ctx = rows < ctx_len

    def pick(idx):
        vx = mod_ref[pl.ds(b, 1), idx * d:(idx + 1) * d]
        vc = mod_ref[nb:nb + 1, idx * d:(idx + 1) * d]
        return jnp.where(is_ctx, vc, vx)

    return y * (1.0 + pick(scale_idx)) + pick(shift_idx), is_ctx, pick


def _ada_kernel(c_ref, w_ref, b_ref, o_ref):
    c = c_ref[...]
    s = c * _sigmoid(c)
    o_ref[0] = jnp.dot(s, w_ref[0], preferred_element_type=F32,
                       precision=lax.Precision.HIGHEST) + b_ref[0]


def _ada_vectors(cvec, ada_w, ada_b):
    depth, d, n = ada_w.shape
    tn = _row_tile(n, 1536, LANES)
    return pl.pallas_call(
        _ada_kernel,
        grid=(depth, n // tn),
        in_specs=[pl.BlockSpec((MOD_ROWS, d), lambda l, j: (0, 0)),
                  pl.BlockSpec((1, d, tn), lambda l, j: (l, 0, j)),
                  pl.BlockSpec((1, 1, tn), lambda l, j: (l, 0, j))],
        out_specs=pl.BlockSpec((1, MOD_ROWS, tn), lambda l, j: (l, 0, j)),
        out_shape=jax.ShapeDtypeStruct((depth, MOD_ROWS, n), F32),
        compiler_params=_params(("arbitrary", "arbitrary")),
        name="ada_vectors",
    )(cvec, ada_w, ada_b.reshape(depth, 1, n))


def _inproj_kernel(x_ref, mod_ref, g_ref, w_ref, o_ref, h_ref, *, tm, ctx_len, nb):
    b = pl.program_id(0)
    i = pl.program_id(1)

    @pl.when(pl.program_id(2) == 0)
    def _():
        h, _, _ = _mod_norm(x_ref[0], g_ref[...], mod_ref, b, i * tm, ctx_len, nb, 0, 1)
        h_ref[...] = h.astype(BF16)

    o_ref[0] = jnp.dot(h_ref[...], w_ref[...], preferred_element_type=F32)


def _in_projection(xc, mod, g, w_bf16, tn, ctx_len):
    nb, t, d = xc.shape
    n = w_bf16.shape[1]
    tm = _row_tile(t, 1280)
    return pl.pallas_call(
        functools.partial(_inproj_kernel, tm=tm, ctx_len=ctx_len, nb=nb),
        grid=(nb, t // tm, n // tn),
        in_specs=[pl.BlockSpec((1, tm, d), lambda b, i, j: (b, i, 0)),
                  pl.BlockSpec(mod.shape, lambda b, i, j: (0, 0)),
                  pl.BlockSpec((1, d), lambda b, i, j: (0, 0)),
                  pl.BlockSpec((d, tn), lambda b, i, j: (0, j))],
        out_specs=pl.BlockSpec((1, tm, tn), lambda b, i, j: (b, i, j)),
        out_shape=jax.ShapeDtypeStruct((nb, t, n), F32),
        scratch_shapes=[pltpu.VMEM((tm, d), BF16)],
        compiler_params=_params(("arbitrary", "arbitrary", "arbitrary")),
        name="in_projection",
    )(xc, mod, g.reshape(1, d), w_bf16)


def _outproj_kernel(*refs, n_o, gated, heads, head_dim, tm, ctx_len, nb):
    o_refs = refs[:n_o]
    pos = n_o
    if gated:
        z_ref, ng_ref = refs[pos], refs[pos + 1]
        pos += 2
    w_ref, x_ref, mod_ref, out_ref = refs[pos:pos + 4]
    b = pl.program_id(0)
    i = pl.program_id(1)
    if gated:
        pieces = []
        for h in range(heads):
            sl = slice(h * head_dim, (h + 1) * head_dim)
            o = o_refs[0][0, :, sl]
            for r in o_refs[1:]:
                o = o + r[0, :, sl]
            ms = jnp.mean(o * o, axis=-1, keepdims=True)
            o = o * lax.rsqrt(ms + NORM_EPS) * ng_ref[...]
            z = z_ref[0, :, sl]
            pieces.append((o * (z * _sigmoid(z))).astype(BF16))
        lhs = jnp.concatenate(pieces, axis=-1)
    else:
        lhs = o_refs[0][0]
    y = jnp.dot(lhs, w_ref[...], preferred_element_type=F32)
    d = y.shape[-1]
    rows = i * tm + lax.broadcasted_iota(jnp.int32, (tm, 1), 0)
    gate = jnp.where(rows < ctx_len, mod_ref[nb:nb + 1, 2 * d:3 * d], mod_ref[pl.ds(b, 1), 2 * d:3 * d])
    out_ref[0] = x_ref[0] + gate * y


def _out_projection(o_list, z_src, z_col_block, norm_g, heads, head_dim, w_bf16, xc, mod, ctx_len, tm_target):
    nb, t, d = xc.shape
    dv = w_bf16.shape[0]
    tm = _row_tile(t, tm_target)
    gated = z_src is not None
    row_spec = lambda width, col: pl.BlockSpec((1, tm, width), lambda b, i: (b, i, col))
    in_specs = [row_spec(dv, 0) for _ in o_list]
    args = list(o_list)
    if gated:
        in_specs += [row_spec(dv, z_col_block), pl.BlockSpec((1, head_dim), lambda b, i: (0, 0))]
        args += [z_src, norm_g.reshape(1, head_dim)]
    in_specs += [pl.BlockSpec((dv, d), lambda b, i: (0, 0)), row_spec(d, 0),
                 pl.BlockSpec(mod.shape, lambda b, i: (0, 0))]
    args += [w_bf16, xc, mod]
    return pl.pallas_call(
        functools.partial(_outproj_kernel, n_o=len(o_list), gated=gated, heads=heads, head_dim=head_dim,
                          tm=tm, ctx_len=ctx_len, nb=nb),
        grid=(nb, t // tm),
        in_specs=in_specs,
        out_specs=row_spec(d, 0),
        out_shape=jax.ShapeDtypeStruct((nb, t, d), F32),
        compiler_params=_params(("arbitrary", "arbitrary")),
        name="out_projection",
    )(*args)


def _ffn_kernel(x_ref, mod_ref, g_ref, w1_ref, w2_ref, o_ref, acc_ref, *, tm, fk, ctx_len, nb):
    b = pl.program_id(0)
    i = pl.program_id(1)
    x = x_ref[0]
    h, _, pick = _mod_norm(x, g_ref[...], mod_ref, b, i * tm, ctx_len, nb, 3, 4)
    h = h.astype(BF16)
    d_ff = w1_ref.shape[1]
    for k in range(d_ff // fk):
        u = jnp.dot(h, w1_ref[:, k * fk:(k + 1) * fk], preferred_element_type=F32)
        u = jnp.maximum(u, 0.0)
        u = (u * u).astype(BF16)
        contrib = jnp.dot(u, w2_ref[k * fk:(k + 1) * fk, :], preferred_element_type=F32)
        if k == 0:
            acc_ref[...] = contrib
        else:
            acc_ref[...] += contrib
    o_ref[0] = x + pick(5) * acc_ref[...]


def _ffn(xc, mod, g, w1_bf16, w2_bf16, ctx_len):
    nb, t, d = xc.shape
    d_ff = w1_bf16.shape[1]
    tm = _row_tile(t, 640)
    return pl.pallas_call(
        functools.partial(_ffn_kernel, tm=tm, fk=512, ctx_len=ctx_len, nb=nb),
        grid=(nb, t // tm),
        in_specs=[pl.BlockSpec((1, tm, d), lambda b, i: (b, i, 0)),
                  pl.BlockSpec(mod.shape, lambda b, i: (0, 0)),
                  pl.BlockSpec((1, d), lambda b, i: (0, 0)),
                  pl.BlockSpec((d, d_ff), lambda b, i: (0, 0)),
                  pl.BlockSpec((d_ff, d), lambda b, i: (0, 0))],
        out_specs=pl.BlockSpec((1, tm, d), lambda b, i: (b, i, 0)),
        out_shape=jax.ShapeDtypeStruct((nb, t, d), F32),
        scratch_shapes=[pltpu.VMEM((tm, d), F32)],
        compiler_params=_params(("arbitrary", "arbitrary")),
        name="ffn",
    )(xc, mod, g.reshape(1, d), w1_bf16, w2_bf16)


def _dn_conv_kernel(p_ref, pp_ref, pn_ref, w_ref, o_ref, ext_ref, *, tm, t_total, ctx_len):
    i = pl.program_id(1)
    j = pl.program_id(2)
    ext_ref[0:8, :] = pp_ref[0]
    ext_ref[8:8 + tm, :] = p_ref[0]
    ext_ref[8 + tm:16 + tm, :] = pn_ref[0]
    t = i * tm + lax.broadcasted_iota(jnp.int32, (tm, 1), 0)
    seg = jnp.where(t >= ctx_len, 1, 0)
    acc = None
    for tap in range(SHORT_CONV):
        d = tap - SHORT_CONV // 2
        xs = ext_ref[8 + d:8 + d + tm, :]
        td = t + d
        ok = jnp.where(td >= 0, 1, 0) * jnp.where(td < t_total, 1, 0) * jnp.where(
            jnp.where(td >= ctx_len, 1, 0) == seg, 1, 0)
        term = jnp.where(ok > 0, xs, 0.0) * w_ref[tap:tap + 1, :]
        acc = term if acc is None else acc + term
    y = acc * _sigmoid(acc)

    @pl.when(j < 2)
    def _():
        scale = jnp.where(j == 0, DN_HEAD ** -0.5, 1.0).astype(F32)
        pieces = []
        for h in range(y.shape[1] // DN_HEAD):
            yh = y[:, h * DN_HEAD:(h + 1) * DN_HEAD]
            ss = jnp.sum(yh * yh, axis=-1, keepdims=True)
            pieces.append(yh * (lax.rsqrt(ss + NORM_EPS) * scale))
        o_ref[0] = jnp.concatenate(pieces, axis=-1)

    @pl.when(j >= 2)
    def _():
        o_ref[0] = y


def _dn_conv(p, conv_w_t, ctx_len):
    nb, t, _ = p.shape
    tc = 1024
    tm = _row_tile(t, 640)
    hb = tm // 8
    last = t // 8 - 1
    return pl.pallas_call(
        functools.partial(_dn_conv_kernel, tm=tm, t_total=t, ctx_len=ctx_len),
        grid=(nb, t // tm, DN_QKV_DIM // tc),
        in_specs=[pl.BlockSpec((1, tm, tc), lambda b, i, j: (b, i, j)),
                  pl.BlockSpec((1, 8, tc), lambda b, i, j: (b, jnp.maximum(i * hb - 1, 0), j)),
                  pl.BlockSpec((1, 8, tc), lambda b, i, j: (b, jnp.minimum((i + 1) * hb, last), j)),
                  pl.BlockSpec((8, tc), lambda b, i, j: (0, j))],
        out_specs=pl.BlockSpec((1, tm, tc), lambda b, i, j: (b, i, j)),
        out_shape=jax.ShapeDtypeStruct((nb, t, DN_QKV_DIM), F32),
        scratch_shapes=[pltpu.VMEM((tm + 16, tc), F32)],
        compiler_params=_params(("arbitrary", "arbitrary", "arbitrary")),
        name="dn_conv",
    )(p, p, p, conv_w_t)


def _dn_gate_kernel(ab_ref, par_ref, o_ref, *, tm):
    x = ab_ref[0]
    g = -jnp.exp(par_ref[0:1, :]) * _softplus(x + par_ref[1:2, :])
    beta = _sigmoid(x)
    lane = lax.broadcasted_iota(jnp.int32, (1, LANES), 1)
    r = lax.broadcasted_iota(jnp.int32, (CHUNK, CHUNK), 0)
    c = lax.broadcasted_iota(jnp.int32, (CHUNK, CHUNK), 1)
    lower = jnp.where(c <= r, 1.0, 0.0).astype(BF16)
    upper = jnp.where(c >= r, 1.0, 0.0).astype(BF16)
    nh = DN_V_HEADS
    for k in range(tm // CHUNK):
        rows = slice(k * CHUNK, (k + 1) * CHUNK)
        gk = g[rows]
        fwd = _dot_sel(lower, gk)
        bwd = _dot_sel(upper, gk)
        o_ref[0, rows, :] = jnp.where(lane < nh, fwd,
                                      jnp.where(lane < 2 * nh, bwd,
                                                jnp.where(lane < 4 * nh, beta[rows], 0.0)))


def _dn_gates(p, ab_col_block, a_log, dt_bias):
    nb, t, _ = p.shape
    tm = _row_tile(t, 1280, CHUNK)
    par = jnp.zeros((8, LANES), F32)
    par = par.at[0, :2 * DN_V_HEADS].set(a_log.reshape(-1).astype(F32))
    par = par.at[1, :2 * DN_V_HEADS].set(dt_bias.reshape(-1).astype(F32))
    return pl.pallas_call(
        functools.partial(_dn_gate_kernel, tm=tm),
        grid=(nb, t // tm),
        in_specs=[pl.BlockSpec((1, tm, LANES), lambda b, i: (b, i, ab_col_block)),
                  pl.BlockSpec((8, LANES), lambda b, i: (0, 0))],
        out_specs=pl.BlockSpec((1, tm, LANES), lambda b, i: (b, i, 0)),
        out_shape=jax.ShapeDtypeStruct((nb, t, LANES), F32),
        compiler_params=_params(("arbitrary", "arbitrary")),
        name="dn_gates",
    )(p, par)


def _unit_lower_inverse(m, eye):
    a = -m
    x = eye + a
    p = a
    for _ in range(int(math.log2(CHUNK)) - 1):
        p = _dot_x3(p, p)
        x = x + _dot_x3(x, p)
    return x


def _dn_scan_kernel(qf_ref, kf_ref, vf_ref, gf_ref, gtf_ref,
                    qr_ref, kr_ref, vr_ref, gr_ref, gtr_ref,
                    of_ref, ob_ref, s_ref):
    @pl.when(pl.program_id(2) == 0)
    def _():
        s_ref[...] = jnp.zeros_like(s_ref)

    ri = lax.broadcasted_iota(jnp.int32, (CHUNK, CHUNK), 0)
    ci = lax.broadcasted_iota(jnp.int32, (CHUNK, CHUNK), 1)
    eye = jnp.where(ri == ci, 1.0, 0.0)
    n_chunks = SCAN_BLOCK // CHUNK
    dirs = ((qf_ref, kf_ref, vf_ref, gf_ref, gtf_ref, of_ref, False),
            (qr_ref, kr_ref, vr_ref, gr_ref, gtr_ref, ob_ref, True))
    for d, (q_ref, k_ref, v_ref, g_ref, gt_ref, o_ref, reverse) in enumerate(dirs):
        incl = (ci >= ri) if reverse else (ci <= ri)
        strict = (ci > ri) if reverse else (ci < ri)
        last = 0 if reverse else CHUNK - 1
        states = [s_ref[2 * d + j] for j in range(DN_GROUP)]
        order = range(n_chunks - 1, -1, -1) if reverse else range(n_chunks)
        for c in order:
            rows = slice(c * CHUNK, (c + 1) * CHUNK)
            q = q_ref[0, rows, :]
            k = k_ref[0, rows, :]
            kk = _dot_nt(k, k)
            qk = _dot_nt(q, k)
            outs = []
            for j in range(DN_GROUP):
                a = 2 * d + j
                gc = g_ref[0, 0, rows, a:a + 1]
                bc = g_ref[0, 0, rows, 4 + a:5 + a]
                gr = gt_ref[0, 0, c, a:a + 1, :]
                v = v_ref[0, rows, j * DN_HEAD:(j + 1) * DN_HEAD]
                diff = gc - gr
                decay = jnp.where(incl, jnp.exp(jnp.where(incl, diff, 0.0)), 0.0)
                m = jnp.where(strict, bc * kk * decay, 0.0)
                tinv = _unit_lower_inverse(m, eye)
                eg = jnp.exp(gc)
                rhs = jnp.concatenate([v * bc, k * (bc * eg)], axis=-1)
                sol = _dot_x3(tinv, rhs)
                u, w = sol[:, :DN_HEAD], sol[:, DN_HEAD:]
                s = states[j]
                r = _dot(jnp.concatenate([w, q * eg], axis=0), s)
                v_new = u - r[:CHUNK]
                outs.append(r[CHUNK:] + _dot(qk * decay, v_new))
                gl = gc[last:last + 1, :]
                kt = k * jnp.exp(gl - gc)
                states[j] = s * jnp.exp(gl) + _dot_tn(kt, v_new)
            o_ref[0, rows, :] = jnp.concatenate(outs, axis=-1)
        for j in range(DN_GROUP):
            s_ref[2 * d + j] = states[j]


def _dn_scan(qkv, g_cols, g_rows):
    nb, t, _ = qkv.shape
    n_blocks = t // SCAN_BLOCK
    n_chunks = SCAN_BLOCK // CHUNK
    kh = DN_K_HEADS
    fwd = lambda s: s
    bwd = lambda s: jnp.where(s == 0, 0, n_blocks - s)

    def specs(order):
        return [pl.BlockSpec((1, SCAN_BLOCK, DN_HEAD), lambda b, h, s: (b, order(s), h)),
                pl.BlockSpec((1, SCAN_BLOCK, DN_HEAD), lambda b, h, s: (b, order(s), kh + h)),
                pl.BlockSpec((1, SCAN_BLOCK, DN_GROUP * DN_HEAD), lambda b, h, s: (b, order(s), kh + h)),
                pl.BlockSpec((1, 1, SCAN_BLOCK, 8), lambda b, h, s: (b, h, order(s), 0)),
                pl.BlockSpec((1, 1, n_chunks, 8, CHUNK), lambda b, h, s: (b, h, order(s), 0, 0))]

    out_spec = lambda order: pl.BlockSpec((1, SCAN_BLOCK, DN_GROUP * DN_HEAD), lambda b, h, s: (b, order(s), h))
    o_shape = jax.ShapeDtypeStruct((nb, t, DN_V_DIM), F32)
    return pl.pallas_call(
        _dn_scan_kernel,
        grid=(nb, kh, n_blocks),
        in_specs=specs(fwd) + specs(bwd),
        out_specs=[out_spec(fwd), out_spec(bwd)],
        out_shape=[o_shape, o_shape],
        scratch_shapes=[pltpu.VMEM((2 * DN_GROUP, DN_HEAD, DN_HEAD), F32)],
        compiler_params=_params(("arbitrary", "arbitrary", "arbitrary")),
        name="dn_scan",
    )(qkv, qkv, qkv, g_cols, g_rows, qkv, qkv, qkv, g_cols, g_rows)


_GLA_LEVELS = (32, 16, 8, 4, 2, 1)


def _gla_tables(reverse):
    idx = np.arange(CHUNK)
    tau = (CHUNK - 1 - idx) if reverse else idx
    ti, tk = tau[:, None], tau[None, :]
    groups = [tk <= ti, tk > ti]
    masks = []
    for s in _GLA_LEVELS:
        bi, bk = ti // s, tk // s
        groups.append((bk == bi) & (tk <= ti) & (tk > bi * s))
        groups.append(((bk == bi) & (tk > ti)) | (tk == (bi + 1) * s))
        masks.append((bi % 2 == 1) & (bk == bi - 1))
    masks.append(ti == tk)
    sel = np.concatenate(groups, axis=0).astype(np.float32)
    return jnp.asarray(sel, BF16), jnp.asarray(np.stack(masks).astype(np.float32))


def _gla_scan_kernel(qf_ref, kf_ref, vf_ref, lf_ref, wf_ref, bf_ref, self_ref, mf_ref,
                     qr_ref, kr_ref, vr_ref, lr_ref, wr_ref, br_ref, selr_ref, mr_ref,
                     of_ref, ob_ref, s_ref):
    @pl.when(pl.program_id(2) == 0)
    def _():
        s_ref[...] = jnp.zeros_like(s_ref)

    n_chunks = SCAN_BLOCK // CHUNK
    n_lev = len(_GLA_LEVELS)
    dirs = ((qf_ref, kf_ref, vf_ref, lf_ref, wf_ref, bf_ref, self_ref, mf_ref, of_ref, False),
            (qr_ref, kr_ref, vr_ref, lr_ref, wr_ref, br_ref, selr_ref, mr_ref, ob_ref, True))
    for d, (q_ref, k_ref, v_ref, l_ref, w_ref, b_ref, sel_ref, m_ref, o_ref, reverse) in enumerate(dirs):
        last = 0 if reverse else CHUNK - 1
        state = s_ref[d]
        order = range(n_chunks - 1, -1, -1) if reverse else range(n_chunks)
        for c in order:
            rows = slice(c * CHUNK, (c + 1) * CHUNK)
            q = q_ref[0, rows, :] * (GLA_HEAD_K ** -0.5)
            k = k_ref[0, rows, :]
            v = v_ref[0, rows, :]
            logits = _dot(l_ref[0, rows, :], w_ref[...]) + b_ref[...]
            gk = (jnp.minimum(logits, 0.0) - jnp.log(1.0 + jnp.exp(-jnp.abs(logits)))) * (1.0 / GLA_GATE_NORMALIZER)
            sums = _dot_sel(sel_ref[...], gk)
            part = lambda n: sums[n * CHUNK:(n + 1) * CHUNK]
            bcum, tail = part(0), part(1)
            a = m_ref[n_lev] * _dot_nt(q, k)
            for lv in range(n_lev):
                a = a + m_ref[lv] * _dot_nt(q * jnp.exp(part(2 + 2 * lv)), k * jnp.exp(part(3 + 2 * lv)))
            o_ref[0, rows, :] = _dot(a, v) + _dot_nt(q * jnp.exp(bcum), state)
            state = state * jnp.exp(bcum[last:last + 1, :]) + _dot_tn(v, k * jnp.exp(tail))
        s_ref[d] = state


def _gla_scan(p, w2cat, b2cat):
    nb, t, _ = p.shape
    n_blocks = t // SCAN_BLOCK
    h = GLA_HEADS
    low_block = (2 * GLA_K_DIM + 2 * GLA_V_DIM) // LANES
    fwd = lambda s: s
    bwd = lambda s: jnp.where(s == 0, 0, n_blocks - s)
    n_sel = (2 + 2 * len(_GLA_LEVELS)) * CHUNK
    n_mask = len(_GLA_LEVELS) + 1

    def specs(order, d):
        return [pl.BlockSpec((1, SCAN_BLOCK, GLA_HEAD_K), lambda b, hh, s: (b, order(s), hh)),
                pl.BlockSpec((1, SCAN_BLOCK, GLA_HEAD_K), lambda b, hh, s: (b, order(s), h + hh)),
                pl.BlockSpec((1, SCAN_BLOCK, GLA_HEAD_V), lambda b, hh, s: (b, order(s), h + hh)),
                pl.BlockSpec((1, SCAN_BLOCK, LANES), lambda b, hh, s: (b, order(s), low_block)),
                pl.BlockSpec((LANES, GLA_HEAD_K), lambda b, hh, s: (0, d * h + hh)),
                pl.BlockSpec((1, GLA_HEAD_K), lambda b, hh, s: (0, d * h + hh)),
                pl.BlockSpec((n_sel, CHUNK), lambda b, hh, s: (0, 0)),
                pl.BlockSpec((n_mask, CHUNK, CHUNK), lambda b, hh, s: (0, 0, 0))]

    out_spec = lambda order: pl.BlockSpec((1, SCAN_BLOCK, GLA_HEAD_V), lambda b, hh, s: (b, order(s), hh))
    o_shape = jax.ShapeDtypeStruct((nb, t, GLA_V_DIM), F32)
    sel_f, mask_f = _gla_tables(False)
    sel_r, mask_r = _gla_tables(True)
    return pl.pallas_call(
        _gla_scan_kernel,
        grid=(nb, h, n_blocks),
        in_specs=specs(fwd, 0) + specs(bwd, 1),
        out_specs=[out_spec(fwd), out_spec(bwd)],
        out_shape=[o_shape, o_shape],
        scratch_shapes=[pltpu.VMEM((2, GLA_HEAD_V, GLA_HEAD_K), F32)],
        compiler_params=_params(("arbitrary", "arbitrary", "arbitrary")),
        name="gla_scan",
    )(p, p, p, p, w2cat, b2cat, sel_f, mask_f, p, p, p, p, w2cat, b2cat, sel_r, mask_r)


def _attn_prep_kernel(p_ref, qg_ref, kg_ref, cos_ref, sin_ref, q_ref, k_ref, v_ref):
    cos = cos_ref[...]
    sin = sin_ref[...]
    lane = lax.broadcasted_iota(jnp.int32, (1, ATTN_HEAD), 1)
    first = (lane % (ATTN_HEAD // 2)) < (ATTN_HEAD // 4)
    q_scale = ATTN_HEAD ** -0.5 * math.log2(math.e)

    def norm_rope(x, g):
        ms = jnp.mean(x * x, axis=-1, keepdims=True)
        y = x * lax.rsqrt(ms + NORM_EPS) * g
        partner = jnp.where(first, pltpu.roll(y, ATTN_HEAD - ATTN_HEAD // 4, 1), pltpu.roll(y, ATTN_HEAD // 4, 1))
        return y * cos + partner * sin

    qd = ATTN_Q_HEADS * ATTN_HEAD
    kd = ATTN_KV_HEADS * ATTN_HEAD
    q_pieces = [norm_rope(p_ref[0, :, h * ATTN_HEAD:(h + 1) * ATTN_HEAD], qg_ref[...]) * q_scale
                for h in range(ATTN_Q_HEADS)]
    q_ref[0] = jnp.concatenate(q_pieces, axis=-1).astype(BF16)
    k_pieces = [norm_rope(p_ref[0, :, qd + h * ATTN_HEAD:qd + (h + 1) * ATTN_HEAD], kg_ref[...])
                for h in range(ATTN_KV_HEADS)]
    k_ref[0] = jnp.concatenate(k_pieces, axis=-1).astype(BF16)
    v_ref[0] = p_ref[0, :, qd + kd:qd + 2 * kd].astype(BF16)


def _rope_tables(t, ctx_len):
    pos = jnp.arange(t - ctx_len)
    row = (pos // GRID_W).astype(F32)
    col = (pos % GRID_W).astype(F32)
    axis_dim = ATTN_HEAD // 2
    inv_freq = jnp.power(ROPE_THETA, -jnp.arange(0, axis_dim, 2, dtype=F32) / axis_dim)
    ar, ac = row[:, None] * inv_freq, col[:, None] * inv_freq
    cos = jnp.concatenate([jnp.cos(ar), jnp.cos(ar), jnp.cos(ac), jnp.cos(ac)], axis=-1)
    sin = jnp.concatenate([-jnp.sin(ar), jnp.sin(ar), -jnp.sin(ac), jnp.sin(ac)], axis=-1)
    cos = jnp.concatenate([jnp.ones((ctx_len, ATTN_HEAD), F32), cos], axis=0)
    sin = jnp.concatenate([jnp.zeros((ctx_len, ATTN_HEAD), F32), sin], axis=0)
    return cos, sin


def _attn_prep(p, q_g, k_g, cos, sin):
    nb, t, n = p.shape
    tm = _row_tile(t, 640)
    qd = ATTN_Q_HEADS * ATTN_HEAD
    kd = ATTN_KV_HEADS * ATTN_HEAD
    row = lambda w: pl.BlockSpec((1, tm, w), lambda b, i: (b, i, 0))
    tab = pl.BlockSpec((tm, ATTN_HEAD), lambda b, i: (i, 0))
    vec = pl.BlockSpec((1, ATTN_HEAD), lambda b, i: (0, 0))
    return pl.pallas_call(
        _attn_prep_kernel,
        grid=(nb, t // tm),
        in_specs=[row(n), vec, vec, tab, tab],
        out_specs=[row(qd), row(kd), row(kd)],
        out_shape=[jax.ShapeDtypeStruct((nb, t, qd), BF16),
                   jax.ShapeDtypeStruct((nb, t, kd), BF16),
                   jax.ShapeDtypeStruct((nb, t, kd), BF16)],
        compiler_params=_params(("arbitrary", "arbitrary")),
        name="attn_prep",
    )(p, q_g.reshape(1, -1), k_g.reshape(1, -1), cos, sin)


def _flash_kernel(q_ref, k_ref, v_ref, o_ref, qs_ref, m_ref, l_ref, acc_ref, *, tq, tk, ctx_len, t_total):
    i = pl.program_id(2)
    g, hd = ATTN_GROUP, ATTN_HEAD
    for h in range(g):
        qs_ref[h * tq:(h + 1) * tq, :] = q_ref[0, :, h * hd:(h + 1) * hd]
    m_ref[...] = jnp.full_like(m_ref, -jnp.inf)
    l_ref[...] = jnp.zeros_like(l_ref)
    acc_ref[...] = jnp.zeros_like(acc_ref)
    n_kv = jnp.where(i * tq < ctx_len, ctx_len // tk, t_total // tk)

    def body(c, carry):
        start = pl.multiple_of(c * tk, tk)
        kc = k_ref[0, pl.ds(start, tk), :]
        vc = v_ref[0, pl.ds(start, tk), :]
        s = lax.dot_general(qs_ref[...], kc, (((1,), (1,)), ((), ())), preferred_element_type=F32)
        m_old = m_ref[...]
        m_new = jnp.maximum(m_old, jnp.max(s, axis=-1, keepdims=True))
        alpha = jnp.exp2(m_old - m_new)
        p = jnp.exp2(s - m_new)
        l_ref[...] = alpha * l_ref[...] + jnp.sum(p, axis=-1, keepdims=True)
        acc_ref[...] = alpha * acc_ref[...] + jnp.dot(p.astype(BF16), vc, preferred_element_type=F32)
        m_ref[...] = m_new
        return carry

    lax.fori_loop(0, n_kv, body, 0)
    out = acc_ref[...] / l_ref[...]
    o_ref[0] = jnp.concatenate([out[h * tq:(h + 1) * tq] for h in range(g)], axis=-1).astype(BF16)


def _flash_attention(q, k, v, ctx_len):
    nb, t, qd = q.shape
    tq = tk = 256
    assert ctx_len % tq == 0 and ctx_len % tk == 0 and t % tq == 0
    gw = ATTN_GROUP * ATTN_HEAD
    return pl.pallas_call(
        functools.partial(_flash_kernel, tq=tq, tk=tk, ctx_len=ctx_len, t_total=t),
        grid=(nb, ATTN_KV_HEADS, t // tq),
        in_specs=[pl.BlockSpec((1, tq, gw), lambda b, kv, i: (b, i, kv)),
                  pl.BlockSpec((1, t, ATTN_HEAD), lambda b, kv, i: (b, 0, kv)),
                  pl.BlockSpec((1, t, ATTN_HEAD), lambda b, kv, i: (b, 0, kv))],
        out_specs=pl.BlockSpec((1, tq, gw), lambda b, kv, i: (b, i, kv)),
        out_shape=jax.ShapeDtypeStruct((nb, t, qd), BF16),
        scratch_shapes=[pltpu.VMEM((ATTN_GROUP * tq, ATTN_HEAD), BF16),
                        pltpu.VMEM((ATTN_GROUP * tq, 1), F32),
                        pltpu.VMEM((ATTN_GROUP * tq, 1), F32),
                        pltpu.VMEM((ATTN_GROUP * tq, ATTN_HEAD), F32)],
        compiler_params=_params(("arbitrary", "arbitrary", "arbitrary")),
        name="flash_attention",
    )(q, k, v)


def _pad_cols(w, n):
    return jnp.pad(w, ((0, 0), (0, n - w.shape[1])))


def _deltanet_layer(xc, mod, norm_g, w_in, conv_w, a_log, dt_bias, out_norm_g, w_out, ctx_len):
    nb, t, _ = xc.shape
    n_pad = 7 * 896
    assert n_pad >= w_in.shape[1]
    p = _in_projection(xc, mod, norm_g, _pad_cols(w_in, n_pad).astype(BF16), 896, ctx_len)
    conv_w_t = jnp.pad(conv_w.T.astype(F32), ((0, 8 - SHORT_CONV), (0, 0)))
    qkv = _dn_conv(p, conv_w_t, ctx_len)
    gb = _dn_gates(p, (DN_QKV_DIM + DN_V_DIM) // LANES, a_log, dt_bias)
    g6 = gb[..., :4 * DN_V_HEADS].reshape(nb, t, 2, 2, DN_K_HEADS, DN_GROUP)
    g_cols = g6.transpose(0, 4, 1, 2, 3, 5).reshape(nb, DN_K_HEADS, t, 4 * DN_GROUP)
    g_rows = g_cols.reshape(nb, DN_K_HEADS, t // CHUNK, CHUNK, 4 * DN_GROUP).transpose(0, 1, 2, 4, 3)
    o_f, o_b = _dn_scan(qkv, g_cols, g_rows)
    return _out_projection([o_f, o_b], p, DN_QKV_DIM // DN_V_DIM, out_norm_g, DN_V_HEADS, DN_HEAD,
                           w_out.astype(BF16), xc, mod, ctx_len, 320)


def _gla_layer(xc, mod, norm_g, w_in, gate_w2, gate_b2, out_norm_g, w_out, ctx_len):
    n_pad = 5 * 640
    assert n_pad >= w_in.shape[1]
    p = _in_projection(xc, mod, norm_g, _pad_cols(w_in, n_pad).astype(BF16), 640, ctx_len)
    r = GLA_GATE_RANK
    w2cat = jnp.zeros((LANES, 2 * GLA_K_DIM), F32)
    w2cat = w2cat.at[0:r, :GLA_K_DIM].set(gate_w2[0]).at[r:2 * r, GLA_K_DIM:].set(gate_w2[1]).astype(BF16)
    b2cat = gate_b2.reshape(1, 2 * GLA_K_DIM).astype(F32)
    o_f, o_b = _gla_scan(p, w2cat, b2cat)
    z_block = (2 * GLA_K_DIM + GLA_V_DIM) // GLA_V_DIM
    return _out_projection([o_f, o_b], p, z_block, out_norm_g, GLA_HEADS, GLA_HEAD_V,
                           w_out.astype(BF16), xc, mod, ctx_len, 640)


def _attention_layer(xc, mod, norm_g, w_in, q_g, k_g, w_out, rope, ctx_len):
    p = _in_projection(xc, mod, norm_g, w_in.astype(BF16), 512, ctx_len)
    q, k, v = _attn_prep(p, q_g, k_g, *rope)
    o = _flash_attention(q, k, v, ctx_len)
    return _out_projection([o], None, 0, None, 0, 0, w_out.astype(BF16), xc, mod, ctx_len, 1280)


def kernel(x, c, ctx, c_ctx, ada_w, ada_b, norm_mix_g, norm_ffn_g, ffn_w1, ffn_w2, dn_w_in, dn_conv_w, dn_a_log, dn_dt_bias, dn_norm_g, dn_w_out, gla_w_in, gla_gate_w2, gla_gate_b2, gla_norm_g, gla_w_out, attn_w_in, attn_q_norm_g, attn_k_norm_g, attn_w_out):
    nb, seq, d = x.shape
    ctx_len = ctx.shape[1]
    depth = ada_w.shape[0]
    assert ctx_len == SCAN_BLOCK and seq % SCAN_BLOCK == 0 and nb < MOD_ROWS
    t = ctx_len + seq
    xc = jnp.concatenate([ctx, x], axis=1)
    cvec = jnp.zeros((MOD_ROWS, d), F32).at[:nb].set(c).at[nb].set(c_ctx)
    mods = _ada_vectors(cvec, ada_w, ada_b)
    rope = _rope_tables(t, ctx_len)
    for i in range(depth):
        mix, slot = i % 3, i // 3
        mod = mods[i]
        if mix == 0:
            xc = _deltanet_layer(xc, mod, norm_mix_g[i], dn_w_in[slot], dn_conv_w[slot], dn_a_log[slot],
                                 dn_dt_bias[slot], dn_norm_g[slot], dn_w_out[slot], ctx_len)
        elif mix == 1:
            xc = _gla_layer(xc, mod, norm_mix_g[i], gla_w_in[slot], gla_gate_w2[slot], gla_gate_b2[slot],
                            gla_norm_g[slot], gla_w_out[slot], ctx_len)
        else:
            xc = _attention_layer(xc, mod, norm_mix_g[i], attn_w_in[slot], attn_q_norm_g[slot],
                                  attn_k_norm_g[slot], attn_w_out[slot], rope, ctx_len)
        xc = _ffn(xc, mod, norm_ffn_g[i], ffn_w1[i].astype(BF16), ffn_w2[i].astype(BF16), ctx_len)
    return xc[:, ctx_len:, :]
```

```python
import functools
import math

import numpy as np
import jax
import jax.numpy as jnp
from jax import lax
from jax.experimental import pallas as pl
from jax.experimental.pallas import tpu as pltpu

F32 = jnp.float32
BF16 = jnp.bfloat16

NORM_EPS = 1e-6
GRID_W = 64
ROPE_THETA = 10000.0
SHORT_CONV = 5

DN_K_HEADS = 8
DN_V_HEADS = 16
DN_HEAD = 128
DN_GROUP = DN_V_HEADS // DN_K_HEADS
DN_K_DIM = DN_K_HEADS * DN_HEAD
DN_V_DIM = DN_V_HEADS * DN_HEAD
DN_QKV_DIM = 2 * DN_K_DIM + DN_V_DIM

GLA_HEADS = 4
GLA_HEAD_K = 128
GLA_HEAD_V = 256
GLA_K_DIM = GLA_HEADS * GLA_HEAD_K
GLA_V_DIM = GLA_HEADS * GLA_HEAD_V
GLA_GATE_RANK = 16
GLA_GATE_NORMALIZER = 16.0

ATTN_Q_HEADS = 8
ATTN_KV_HEADS = 2
ATTN_HEAD = 128
ATTN_GROUP = ATTN_Q_HEADS // ATTN_KV_HEADS

CHUNK = 64
SCAN_BLOCK = 256
LANES = 128
MOD_ROWS = 8
VMEM_LIMIT = 56 * 1024 * 1024


def _params(semantics, vmem=VMEM_LIMIT):
    return pltpu.CompilerParams(dimension_semantics=semantics, vmem_limit_bytes=vmem)


def _sigmoid(x):
    return 1.0 / (1.0 + jnp.exp(-x))


def _softplus(x):
    return jnp.maximum(x, 0.0) + jnp.log(1.0 + jnp.exp(-jnp.abs(x)))


def _split3(x):
    hi = x.astype(BF16)
    r1 = x - hi.astype(F32)
    mid = r1.astype(BF16)
    lo = (r1 - mid.astype(F32)).astype(BF16)
    return hi, mid, lo


def _dot(a, b):
    return jnp.dot(a.astype(BF16), b.astype(BF16), preferred_element_type=F32)


def _dot_nt(a, b):
    return lax.dot_general(a.astype(BF16), b.astype(BF16), (((1,), (1,)), ((), ())),
                           preferred_element_type=F32)


def _dot_tn(a, b):
    return lax.dot_general(a.astype(BF16), b.astype(BF16), (((0,), (0,)), ((), ())),
                           preferred_element_type=F32)


def _dot_sel(p_bf16, x):
    hi, mid, lo = _split3(x)
    d = lambda y: jnp.dot(p_bf16, y, preferred_element_type=F32)
    return d(hi) + d(mid) + d(lo)


def _dot_x3(a, b):
    ah = a.astype(BF16)
    al = (a - ah.astype(F32)).astype(BF16)
    bh = b.astype(BF16)
    bl = (b - bh.astype(F32)).astype(BF16)
    d = lambda x, y: jnp.dot(x, y, preferred_element_type=F32)
    return d(ah, bh) + d(ah, bl) + d(al, bh)


def _row_tile(total, target, multiple=8):
    best = None
    for t in range(multiple, min(total, target) + 1, multiple):
        if total % t == 0:
            best = t
    assert best is not None, (total, target, multiple)
    return best


def _mod_norm(x, g, mod_ref, b, row0, ctx_len, nb, shift_idx, scale_idx):
    d = x.shape[-1]
    ms = jnp.mean(x * x, axis=-1, keepdims=True)
    y = x * lax.rsqrt(ms + NORM_EPS) * g
    rows = row0 + lax.broadcasted_iota(jnp.int32, (x.shape[0], 1), 0)
    is_ctx = rows < ctx_len

    def pick(idx):
        vx = mod_ref[pl.ds(b, 1), idx * d:(idx + 1) * d]
        vc = mod_ref[nb:nb + 1, idx * d:(idx + 1) * d]
        return jnp.where(is_ctx, vc, vx)

    return y * (1.0 + pick(scale_idx)) + pick(shift_idx), is_ctx, pick


def _ada_kernel(c_ref, w_ref, b_ref, o_ref):
    c = c_ref[...]
    s = c * _sigmoid(c)
    o_ref[0] = jnp.dot(s, w_ref[0], preferred_element_type=F32,
                       precision=lax.Precision.HIGHEST) + b_ref[0]


def _ada_vectors(cvec, ada_w, ada_b):
    depth, d, n = ada_w.shape
    tn = _row_tile(n, 1536, LANES)
    return pl.pallas_call(
        _ada_kernel,
        grid=(depth, n // tn),
        in_specs=[pl.BlockSpec((MOD_ROWS, d), lambda l, j: (0, 0)),
                  pl.BlockSpec((1, d, tn), lambda l, j: (l, 0, j)),
                  pl.BlockSpec((1, 1, tn), lambda l, j: (l, 0, j))],
        out_specs=pl.BlockSpec((1, MOD_ROWS, tn), lambda l, j: (l, 0, j)),
        out_shape=jax.ShapeDtypeStruct((depth, MOD_ROWS, n), F32),
        compiler_params=_params(("arbitrary", "arbitrary")),
        name="ada_vectors",
    )(cvec, ada_w, ada_b.reshape(depth, 1, n))


def _inproj_kernel(x_ref, mod_ref, g_ref, w_ref, o_ref, h_ref, *, tm, ctx_len, nb):
    b = pl.program_id(0)
    i = pl.program_id(1)

    @pl.when(pl.program_id(2) == 0)
    def _():
        h, _, _ = _mod_norm(x_ref[0], g_ref[...], mod_ref, b, i * tm, ctx_len, nb, 0, 1)
        h_ref[...] = h.astype(BF16)

    o_ref[0] = jnp.dot(h_ref[...], w_ref[...], preferred_element_type=F32)


def _in_projection(xc, mod, g, w_bf16, tn, ctx_len):
    nb, t, d = xc.shape
    n = w_bf16.shape[1]
    tm = _row_tile(t, 1280)
    return pl.pallas_call(
        functools.partial(_inproj_kernel, tm=tm, ctx_len=ctx_len, nb=nb),
        grid=(nb, t // tm, n // tn),
        in_specs=[pl.BlockSpec((1, tm, d), lambda b, i, j: (b, i, 0)),
                  pl.BlockSpec(mod.shape, lambda b, i, j: (0, 0)),
                  pl.BlockSpec((1, d), lambda b, i, j: (0, 0)),
                  pl.BlockSpec((d, tn), lambda b, i, j: (0, j))],
        out_specs=pl.BlockSpec((1, tm, tn), lambda b, i, j: (b, i, j)),
        out_shape=jax.ShapeDtypeStruct((nb, t, n), F32),
        scratch_shapes=[pltpu.VMEM((tm, d), BF16)],
        compiler_params=_params(("arbitrary", "arbitrary", "arbitrary")),
        name="in_projection",
    )(xc, mod, g.reshape(1, d), w_bf16)


def _outproj_kernel(*refs, n_o, gated, heads, head_dim, tm, ctx_len, nb):
    o_refs = refs[:n_o]
    pos = n_o
    if gated:
        z_ref, ng_ref = refs[pos], refs[pos + 1]
        pos += 2
    w_ref, x_ref, mod_ref, out_ref = refs[pos:pos + 4]
    b = pl.program_id(0)
    i = pl.program_id(1)
    if gated:
        pieces = []
        for h in range(heads):
            sl = slice(h * head_dim, (h + 1) * head_dim)
            o = o_refs[0][0, :, sl]
            for r in o_refs[1:]:
                o = o + r[0, :, sl]
            ms = jnp.mean(o * o, axis=-1, keepdims=True)
            o = o * lax.rsqrt(ms + NORM_EPS) * ng_ref[...]
            z = z_ref[0, :, sl]
            pieces.append((o * (z * _sigmoid(z))).astype(BF16))
        lhs = jnp.concatenate(pieces, axis=-1)
    else:
        lhs = o_refs[0][0]
    y = jnp.dot(lhs, w_ref[...], preferred_element_type=F32)
    d = y.shape[-1]
    rows = i * tm + lax.broadcasted_iota(jnp.int32, (tm, 1), 0)
    gate = jnp.where(rows < ctx_len, mod_ref[nb:nb + 1, 2 * d:3 * d], mod_ref[pl.ds(b, 1), 2 * d:3 * d])
    out_ref[0] = x_ref[0] + gate * y


def _out_projection(o_list, z_src, z_col_block, norm_g, heads, head_dim, w_bf16, xc, mod, ctx_len, tm_target):
    nb, t, d = xc.shape
    dv = w_bf16.shape[0]
    tm = _row_tile(t, tm_target)
    gated = z_src is not None
    row_spec = lambda width, col: pl.BlockSpec((1, tm, width), lambda b, i: (b, i, col))
    in_specs = [row_spec(dv, 0) for _ in o_list]
    args = list(o_list)
    if gated:
        in_specs += [row_spec(dv, z_col_block), pl.BlockSpec((1, head_dim), lambda b, i: (0, 0))]
        args += [z_src, norm_g.reshape(1, head_dim)]
    in_specs += [pl.BlockSpec((dv, d), lambda b, i: (0, 0)), row_spec(d, 0),
                 pl.BlockSpec(mod.shape, lambda b, i: (0, 0))]
    args += [w_bf16, xc, mod]
    return pl.pallas_call(
        functools.partial(_outproj_kernel, n_o=len(o_list), gated=gated, heads=heads, head_dim=head_dim,
                          tm=tm, ctx_len=ctx_len, nb=nb),
        grid=(nb, t // tm),
        in_specs=in_specs,
        out_specs=row_spec(d, 0),
        out_shape=jax.ShapeDtypeStruct((nb, t, d), F32),
        compiler_params=_params(("arbitrary", "arbitrary")),
        name="out_projection",
    )(*args)


def _ffn_kernel(x_ref, mod_ref, g_ref, w1_ref, w2_ref, o_ref, acc_ref, *, tm, fk, ctx_len, nb):
    b = pl.program_id(0)
    i = pl.program_id(1)
    x = x_ref[0]
    h, _, pick = _mod_norm(x, g_ref[...], mod_ref, b, i * tm, ctx_len, nb, 3, 4)
    h = h.astype(BF16)
    d_ff = w1_ref.shape[1]
    for k in range(d_ff // fk):
        u = jnp.dot(h, w1_ref[:, k * fk:(k + 1) * fk], preferred_element_type=F32)
        u = jnp.maximum(u, 0.0)
        u = (u * u).astype(BF16)
        contrib = jnp.dot(u, w2_ref[k * fk:(k + 1) * fk, :], preferred_element_type=F32)
        if k == 0:
            acc_ref[...] = contrib
        else:
            acc_ref[...] += contrib
    o_ref[0] = x + pick(5) * acc_ref[...]


def _ffn(xc, mod, g, w1_bf16, w2_bf16, ctx_len):
    nb, t, d = xc.shape
    d_ff = w1_bf16.shape[1]
    tm = _row_tile(t, 640)
    return pl.pallas_call(
        functools.partial(_ffn_kernel, tm=tm, fk=512, ctx_len=ctx_len, nb=nb),
        grid=(nb, t // tm),
        in_specs=[pl.BlockSpec((1, tm, d), lambda b, i: (b, i, 0)),
                  pl.BlockSpec(mod.shape, lambda b, i: (0, 0)),
                  pl.BlockSpec((1, d), lambda b, i: (0, 0)),
                  pl.BlockSpec((d, d_ff), lambda b, i: (0, 0)),
                  pl.BlockSpec((d_ff, d), lambda b, i: (0, 0))],
        out_specs=pl.BlockSpec((1, tm, d), lambda b, i: (b, i, 0)),
        out_shape=jax.ShapeDtypeStruct((nb, t, d), F32),
        scratch_shapes=[pltpu.VMEM((tm, d), F32)],
        compiler_params=_params(("arbitrary", "arbitrary")),
        name="ffn",
    )(xc, mod, g.reshape(1, d), w1_bf16, w2_bf16)


def _dn_conv_kernel(p_ref, pp_ref, pn_ref, w_ref, o_ref, ext_ref, *, tm, t_total, ctx_len):
    i = pl.program_id(1)
    j = pl.program_id(2)
    ext_ref[0:8, :] = pp_ref[0]
    ext_ref[8:8 + tm, :] = p_ref[0]
    ext_ref[8 + tm:16 + tm, :] = pn_ref[0]
    t = i * tm + lax.broadcasted_iota(jnp.int32, (tm, 1), 0)
    seg = jnp.where(t >= ctx_len, 1, 0)
    acc = None
    for tap in range(SHORT_CONV):
        d = tap - SHORT_CONV // 2
        xs = ext_ref[8 + d:8 + d + tm, :]
        td = t + d
        ok = jnp.where(td >= 0, 1, 0) * jnp.where(td < t_total, 1, 0) * jnp.where(
            jnp.where(td >= ctx_len, 1, 0) == seg, 1, 0)
        term = jnp.where(ok > 0, xs, 0.0) * w_ref[tap:tap + 1, :]
        acc = term if acc is None else acc + term
    y = acc * _sigmoid(acc)

    @pl.when(j < 2)
    def _():
        scale = jnp.where(j == 0, DN_HEAD ** -0.5, 1.0).astype(F32)
        pieces = []
        for h in range(y.shape[1] // DN_HEAD):
            yh = y[:, h * DN_HEAD:(h + 1) * DN_HEAD]
            ss = jnp.sum(yh * yh, axis=-1, keepdims=True)
            pieces.append(yh * (lax.rsqrt(ss + NORM_EPS) * scale))
        o_ref[0] = jnp.concatenate(pieces, axis=-1)

    @pl.when(j >= 2)
    def _():
        o_ref[0] = y


def _dn_conv(p, conv_w_t, ctx_len):
    nb, t, _ = p.shape
    tc = 1024
    tm = _row_tile(t, 640)
    hb = tm // 8
    last = t // 8 - 1
    return pl.pallas_call(
        functools.partial(_dn_conv_kernel, tm=tm, t_total=t, ctx_len=ctx_len),
        grid=(nb, t // tm, DN_QKV_DIM // tc),
        in_specs=[pl.BlockSpec((1, tm, tc), lambda b, i, j: (b, i, j)),
                  pl.BlockSpec((1, 8, tc), lambda b, i, j: (b, jnp.maximum(i * hb - 1, 0), j)),
                  pl.BlockSpec((1, 8, tc), lambda b, i, j: (b, jnp.minimum((i + 1) * hb, last), j)),
                  pl.BlockSpec((8, tc), lambda b, i, j: (0, j))],
        out_specs=pl.BlockSpec((1, tm, tc), lambda b, i, j: (b, i, j)),
        out_shape=jax.ShapeDtypeStruct((nb, t, DN_QKV_DIM), F32),
        scratch_shapes=[pltpu.VMEM((tm + 16, tc), F32)],
        compiler_params=_params(("arbitrary", "arbitrary", "arbitrary")),
        name="dn_conv",
    )(p, p, p, conv_w_t)


def _dn_gate_kernel(ab_ref, par_ref, o_ref, *, tm):
    x = ab_ref[0]
    g = -jnp.exp(par_ref[0:1, :]) * _softplus(x + par_ref[1:2, :])
    beta = _sigmoid(x)
    lane = lax.broadcasted_iota(jnp.int32, (1, LANES), 1)
    r = lax.broadcasted_iota(jnp.int32, (CHUNK, CHUNK), 0)
    c = lax.broadcasted_iota(jnp.int32, (CHUNK, CHUNK), 1)
    lower = jnp.where(c <= r, 1.0, 0.0).astype(BF16)
    upper = jnp.where(c >= r, 1.0, 0.0).astype(BF16)
    nh = DN_V_HEADS
    for k in range(tm // CHUNK):
        rows = slice(k * CHUNK, (k + 1) * CHUNK)
        gk = g[rows]
        fwd = _dot_sel(lower, gk)
        bwd = _dot_sel(upper, gk)
        o_ref[0, rows, :] = jnp.where(lane < nh, fwd,
                                      jnp.where(lane < 2 * nh, bwd,
                                                jnp.where(lane < 4 * nh, beta[rows], 0.0)))


def _dn_gates(p, ab_col_block, a_log, dt_bias):
    nb, t, _ = p.shape
    tm = _row_tile(t, 1280, CHUNK)
    par = jnp.zeros((8, LANES), F32)
    par = par.at[0, :2 * DN_V_HEADS].set(a_log.reshape(-1).astype(F32))
    par = par.at[1, :2 * DN_V_HEADS].set(dt_bias.reshape(-1).astype(F32))
    return pl.pallas_call(
        functools.partial(_dn_gate_kernel, tm=tm),
        grid=(nb, t // tm),
        in_specs=[pl.BlockSpec((1, tm, LANES), lambda b, i: (b, i, ab_col_block)),
                  pl.BlockSpec((8, LANES), lambda b, i: (0, 0))],
        out_specs=pl.BlockSpec((1, tm, LANES), lambda b, i: (b, i, 0)),
        out_shape=jax.ShapeDtypeStruct((nb, t, LANES), F32),
        compiler_params=_params(("arbitrary", "arbitrary")),
        name="dn_gates",
    )(p, par)


def _bdot(a, b):
    return jnp.einsum("nij,njk->nik", a.astype(BF16), b.astype(BF16), preferred_element_type=F32)


def _bdot_nt(a, b):
    return jnp.einsum("nid,njd->nij", a.astype(BF16), b.astype(BF16), preferred_element_type=F32)


def _bdot_tn(a, b):
    return jnp.einsum("nci,ncj->nij", a.astype(BF16), b.astype(BF16), preferred_element_type=F32)


N_LEVELS = int(math.log2(CHUNK))


def _unit_triangular_inverse(parts, eye):
    coupling = lambda lv: jnp.concatenate([jnp.where(masks[lv], m, 0.0) for m, masks in parts], axis=0)
    x = eye - coupling(0)
    for lv in range(1, N_LEVELS):
        x = x - _bdot(x, _bdot(coupling(lv), x))
    return x


def _coupling_masks(ri, ci, reverse):
    ti, tj = (CHUNK - 1 - ri, CHUNK - 1 - ci) if reverse else (ri, ci)
    masks = []
    for lv in range(N_LEVELS):
        bi, bj = lax.shift_right_logical(ti, lv), lax.shift_right_logical(tj, lv)
        masks.append(jnp.where((bi & 1) == 1, bi - 1, -1) == bj)
    return masks


def _dn_scan_kernel(qf_ref, kf_ref, vf_ref, gf_ref, gtf_ref,
                    qr_ref, kr_ref, vr_ref, gr_ref, gtr_ref,
                    of_ref, ob_ref, s_ref):
    @pl.when(pl.program_id(2) == 0)
    def _():
        s_ref[...] = jnp.zeros_like(s_ref)

    ri = lax.broadcasted_iota(jnp.int32, (CHUNK, CHUNK), 0)
    ci = lax.broadcasted_iota(jnp.int32, (CHUNK, CHUNK), 1)
    eye = jnp.where(ri == ci, 1.0, 0.0)
    nc = SCAN_BLOCK // CHUNK
    chunk_rows = [slice(c * CHUNK, (c + 1) * CHUNK) for c in range(nc)]
    dirs = ((qf_ref, kf_ref, vf_ref, gf_ref, gtf_ref, False),
            (qr_ref, kr_ref, vr_ref, gr_ref, gtr_ref, True))
    m_parts, rhs_l, a_l, qg_l, kt_l, egl_l = [], [], [], [], [], []
    for d, (q_ref, k_ref, v_ref, g_ref, gt_ref, reverse) in enumerate(dirs):
        incl = (ci >= ri) if reverse else (ci <= ri)
        strict = (ci > ri) if reverse else (ci < ri)
        last = 0 if reverse else CHUNK - 1
        m_l = []
        q = jnp.stack([q_ref[0, r, :] for r in chunk_rows])
        k = jnp.stack([k_ref[0, r, :] for r in chunk_rows])
        kk = _bdot_nt(k, k)
        qk = _bdot_nt(q, k)
        for j in range(DN_GROUP):
            col = 2 * d + j
            gc = jnp.stack([g_ref[0, 0, r, col:col + 1] for r in chunk_rows])
            bc = jnp.stack([g_ref[0, 0, r, 4 + col:5 + col] for r in chunk_rows])
            gr = jnp.stack([gt_ref[0, 0, c, col:col + 1, :] for c in range(nc)])
            v = jnp.stack([v_ref[0, r, j * DN_HEAD:(j + 1) * DN_HEAD] for r in chunk_rows])
            decay = jnp.where(incl, jnp.exp(jnp.where(incl, gc - gr, 0.0)), 0.0)
            eg = jnp.exp(gc)
            gl = gc[:, last:last + 1, :]
            m_l.append(jnp.where(strict, bc * kk * decay, 0.0))
            rhs_l.append(jnp.concatenate([k * (bc * eg), v * bc], axis=-1))
            a_l.append(qk * decay)
            qg_l.append(q * eg)
            kt_l.append(k * jnp.exp(gl - gc))
            egl_l.append(jnp.exp(gl))
        m_parts.append((jnp.concatenate(m_l, axis=0), _coupling_masks(ri, ci, reverse)))
    cat = lambda xs: jnp.concatenate(xs, axis=0)
    a, kt = cat(a_l), cat(kt_l)
    wu = _bdot(_unit_triangular_inverse(m_parts, eye), cat(rhs_l))
    kb = _bdot_tn(kt, wu)
    qo = _bdot(a, wu)
    qeff = cat(qg_l) - qo[:, :, :DN_HEAD]
    egl = cat(egl_l)
    state = s_ref[...]
    o_refs = (of_ref, ob_ref)
    for step in range(nc):
        chunk_of = [step if d == 0 else nc - 1 - step for d in range(2) for _ in range(DN_GROUP)]
        idx = [ch * nc + c for ch, c in enumerate(chunk_of)]
        pick = lambda x: jnp.stack([x[n] for n in idx])
        o = _bdot(pick(qeff), state) + pick(qo)[:, :, DN_HEAD:]
        kb_s = pick(kb)
        state = pick(egl) * state + kb_s[:, :, DN_HEAD:] - _bdot(kb_s[:, :, :DN_HEAD], state)
        for ch, c in enumerate(chunk_of):
            j = ch % DN_GROUP
            o_refs[ch // DN_GROUP][0, chunk_rows[c], j * DN_HEAD:(j + 1) * DN_HEAD] = o[ch]
    s_ref[...] = state


def _dn_scan(qkv, g_cols, g_rows):
    nb, t, _ = qkv.shape
    n_blocks = t // SCAN_BLOCK
    n_chunks = SCAN_BLOCK // CHUNK
    kh = DN_K_HEADS
    fwd = lambda s: s
    bwd = lambda s: jnp.where(s == 0, 0, n_blocks - s)

    def specs(order):
        return [pl.BlockSpec((1, SCAN_BLOCK, DN_HEAD), lambda b, h, s: (b, order(s), h)),
                pl.BlockSpec((1, SCAN_BLOCK, DN_HEAD), lambda b, h, s: (b, order(s), kh + h)),
                pl.BlockSpec((1, SCAN_BLOCK, DN_GROUP * DN_HEAD), lambda b, h, s: (b, order(s), kh + h)),
                pl.BlockSpec((1, 1, SCAN_BLOCK, 8), lambda b, h, s: (b, h, order(s), 0)),
                pl.BlockSpec((1, 1, n_chunks, 8, CHUNK), lambda b, h, s: (b, h, order(s), 0, 0))]

    out_spec = lambda order: pl.BlockSpec((1, SCAN_BLOCK, DN_GROUP * DN_HEAD), lambda b, h, s: (b, order(s), h))
    o_shape = jax.ShapeDtypeStruct((nb, t, DN_V_DIM), F32)
    return pl.pallas_call(
        _dn_scan_kernel,
        grid=(nb, kh, n_blocks),
        in_specs=specs(fwd) + specs(bwd),
        out_specs=[out_spec(fwd), out_spec(bwd)],
        out_shape=[o_shape, o_shape],
        scratch_shapes=[pltpu.VMEM((2 * DN_GROUP, DN_HEAD, DN_HEAD), F32)],
        compiler_params=_params(("arbitrary", "arbitrary", "arbitrary")),
        name="dn_scan",
    )(qkv, qkv, qkv, g_cols, g_rows, qkv, qkv, qkv, g_cols, g_rows)


_GLA_LEVELS = (32, 16, 8, 4, 2, 1)


def _gla_tables(reverse):
    idx = np.arange(CHUNK)
    tau = (CHUNK - 1 - idx) if reverse else idx
    ti, tk = tau[:, None], tau[None, :]
    groups = [tk <= ti, tk > ti]
    masks = []
    for s in _GLA_LEVELS:
        bi, bk = ti // s, tk // s
        groups.append((bk == bi) & (tk <= ti) & (tk > bi * s))
        groups.append(((bk == bi) & (tk > ti)) | (tk == (bi + 1) * s))
        masks.append((bi % 2 == 1) & (bk == bi - 1))
    masks.append(ti == tk)
    sel = np.concatenate(groups, axis=0).astype(np.float32)
    return jnp.asarray(sel, BF16), jnp.asarray(np.stack(masks).astype(np.float32))


def _gla_scan_kernel(qf_ref, kf_ref, vf_ref, lf_ref, wf_ref, bf_ref, self_ref, mf_ref,
                     qr_ref, kr_ref, vr_ref, lr_ref, wr_ref, br_ref, selr_ref, mr_ref,
                     of_ref, ob_ref, s_ref):
    @pl.when(pl.program_id(2) == 0)
    def _():
        s_ref[...] = jnp.zeros_like(s_ref)

    n_chunks = SCAN_BLOCK // CHUNK
    n_lev = len(_GLA_LEVELS)
    dirs = ((qf_ref, kf_ref, vf_ref, lf_ref, wf_ref, bf_ref, self_ref, mf_ref, of_ref, False),
            (qr_ref, kr_ref, vr_ref, lr_ref, wr_ref, br_ref, selr_ref, mr_ref, ob_ref, True))
    for d, (q_ref, k_ref, v_ref, l_ref, w_ref, b_ref, sel_ref, m_ref, o_ref, reverse) in enumerate(dirs):
        last = 0 if reverse else CHUNK - 1
        state = s_ref[d]
        order = range(n_chunks - 1, -1, -1) if reverse else range(n_chunks)
        for c in order:
            rows = slice(c * CHUNK, (c + 1) * CHUNK)
            q = q_ref[0, rows, :] * (GLA_HEAD_K ** -0.5)
            k = k_ref[0, rows, :]
            v = v_ref[0, rows, :]
            logits = _dot(l_ref[0, rows, :], w_ref[...]) + b_ref[...]
            gk = (jnp.minimum(logits, 0.0) - jnp.log(1.0 + jnp.exp(-jnp.abs(logits)))) * (1.0 / GLA_GATE_NORMALIZER)
            sums = _dot_sel(sel_ref[...], gk)
            part = lambda n: sums[n * CHUNK:(n + 1) * CHUNK]
            bcum, tail = part(0), part(1)
            a = m_ref[n_lev] * _dot_nt(q, k)
            for lv in range(n_lev):
                a = a + m_ref[lv] * _dot_nt(q * jnp.exp(part(2 + 2 * lv)), k * jnp.exp(part(3 + 2 * lv)))
            o_ref[0, rows, :] = _dot(a, v) + _dot_nt(q * jnp.exp(bcum), state)
            state = state * jnp.exp(bcum[last:last + 1, :]) + _dot_tn(v, k * jnp.exp(tail))
        s_ref[d] = state


def _gla_scan(p, w2cat, b2cat):
    nb, t, _ = p.shape
    n_blocks = t // SCAN_BLOCK
    h = GLA_HEADS
    low_block = (2 * GLA_K_DIM + 2 * GLA_V_DIM) // LANES
    fwd = lambda s: s
    bwd = lambda s: jnp.where(s == 0, 0, n_blocks - s)
    n_sel = (2 + 2 * len(_GLA_LEVELS)) * CHUNK
    n_mask = len(_GLA_LEVELS) + 1

    def specs(order, d):
        return [pl.BlockSpec((1, SCAN_BLOCK, GLA_HEAD_K), lambda b, hh, s: (b, order(s), hh)),
                pl.BlockSpec((1, SCAN_BLOCK, GLA_HEAD_K), lambda b, hh, s: (b, order(s), h + hh)),
                pl.BlockSpec((1, SCAN_BLOCK, GLA_HEAD_V), lambda b, hh, s: (b, order(s), h + hh)),
                pl.BlockSpec((1, SCAN_BLOCK, LANES), lambda b, hh, s: (b, order(s), low_block)),
                pl.BlockSpec((LANES, GLA_HEAD_K), lambda b, hh, s: (0, d * h + hh)),
                pl.BlockSpec((1, GLA_HEAD_K), lambda b, hh, s: (0, d * h + hh)),
                pl.BlockSpec((n_sel, CHUNK), lambda b, hh, s: (0, 0)),
                pl.BlockSpec((n_mask, CHUNK, CHUNK), lambda b, hh, s: (0, 0, 0))]

    out_spec = lambda order: pl.BlockSpec((1, SCAN_BLOCK, GLA_HEAD_V), lambda b, hh, s: (b, order(s), hh))
    o_shape = jax.ShapeDtypeStruct((nb, t, GLA_V_DIM), F32)
    sel_f, mask_f = _gla_tables(False)
    sel_r, mask_r = _gla_tables(True)
    return pl.pallas_call(
        _gla_scan_kernel,
        grid=(nb, h, n_blocks),
        in_specs=specs(fwd, 0) + specs(bwd, 1),
        out_specs=[out_spec(fwd), out_spec(bwd)],
        out_shape=[o_shape, o_shape],
        scratch_shapes=[pltpu.VMEM((2, GLA_HEAD_V, GLA_HEAD_K), F32)],
        compiler_params=_params(("arbitrary", "arbitrary", "arbitrary")),
        name="gla_scan",
    )(p, p, p, p, w2cat, b2cat, sel_f, mask_f, p, p, p, p, w2cat, b2cat, sel_r, mask_r)


def _attn_prep_kernel(p_ref, qg_ref, kg_ref, cos_ref, sin_ref, q_ref, k_ref, v_ref):
    cos = cos_ref[...]
    sin = sin_ref[...]
    lane = lax.broadcasted_iota(jnp.int32, (1, ATTN_HEAD), 1)
    first = (lane % (ATTN_HEAD // 2)) < (ATTN_HEAD // 4)
    q_scale = ATTN_HEAD ** -0.5 * math.log2(math.e)

    def norm_rope(x, g):
        ms = jnp.mean(x * x, axis=-1, keepdims=True)
        y = x * lax.rsqrt(ms + NORM_EPS) * g
        partner = jnp.where(first, pltpu.roll(y, ATTN_HEAD - ATTN_HEAD // 4, 1), pltpu.roll(y, ATTN_HEAD // 4, 1))
        return y * cos + partner * sin

    qd = ATTN_Q_HEADS * ATTN_HEAD
    kd = ATTN_KV_HEADS * ATTN_HEAD
    q_pieces = [norm_rope(p_ref[0, :, h * ATTN_HEAD:(h + 1) * ATTN_HEAD], qg_ref[...]) * q_scale
                for h in range(ATTN_Q_HEADS)]
    q_ref[0] = jnp.concatenate(q_pieces, axis=-1).astype(BF16)
    k_pieces = [norm_rope(p_ref[0, :, qd + h * ATTN_HEAD:qd + (h + 1) * ATTN_HEAD], kg_ref[...])
                for h in range(ATTN_KV_HEADS)]
    k_ref[0] = jnp.concatenate(k_pieces, axis=-1).astype(BF16)
    v_ref[0] = p_ref[0, :, qd + kd:qd + 2 * kd].astype(BF16)


def _rope_tables(t, ctx_len):
    pos = jnp.arange(t - ctx_len)
    row = (pos // GRID_W).astype(F32)
    col = (pos % GRID_W).astype(F32)
    axis_dim = ATTN_HEAD // 2
    inv_freq = jnp.power(ROPE_THETA, -jnp.arange(0, axis_dim, 2, dtype=F32) / axis_dim)
    ar, ac = row[:, None] * inv_freq, col[:, None] * inv_freq
    cos = jnp.concatenate([jnp.cos(ar), jnp.cos(ar), jnp.cos(ac), jnp.cos(ac)], axis=-1)
    sin = jnp.concatenate([-jnp.sin(ar), jnp.sin(ar), -jnp.sin(ac), jnp.sin(ac)], axis=-1)
    cos = jnp.concatenate([jnp.ones((ctx_len, ATTN_HEAD), F32), cos], axis=0)
    sin = jnp.concatenate([jnp.zeros((ctx_len, ATTN_HEAD), F32), sin], axis=0)
    return cos, sin


def _attn_prep(p, q_g, k_g, cos, sin):
    nb, t, n = p.shape
    tm = _row_tile(t, 640)
    qd = ATTN_Q_HEADS * ATTN_HEAD
    kd = ATTN_KV_HEADS * ATTN_HEAD
    row = lambda w: pl.BlockSpec((1, tm, w), lambda b, i: (b, i, 0))
    tab = pl.BlockSpec((tm, ATTN_HEAD), lambda b, i: (i, 0))
    vec = pl.BlockSpec((1, ATTN_HEAD), lambda b, i: (0, 0))
    return pl.pallas_call(
        _attn_prep_kernel,
        grid=(nb, t // tm),
        in_specs=[row(n), vec, vec, tab, tab],
        out_specs=[row(qd), row(kd), row(kd)],
        out_shape=[jax.ShapeDtypeStruct((nb, t, qd), BF16),
                   jax.ShapeDtypeStruct((nb, t, kd), BF16),
                   jax.ShapeDtypeStruct((nb, t, kd), BF16)],
        compiler_params=_params(("arbitrary", "arbitrary")),
        name="attn_prep",
    )(p, q_g.reshape(1, -1), k_g.reshape(1, -1), cos, sin)


def _flash_kernel(q_ref, k_ref, v_ref, o_ref, qs_ref, m_ref, l_ref, acc_ref, sa_ref, sb_ref, *, tq, tk, ctx_len, t_total):
    i = pl.program_id(2)
    g, hd = ATTN_GROUP, ATTN_HEAD
    for h in range(g):
        qs_ref[h * tq:(h + 1) * tq, :] = q_ref[0, :, h * hd:(h + 1) * hd]
    m_ref[...] = jnp.full_like(m_ref, -jnp.inf)
    l_ref[...] = jnp.zeros_like(l_ref)
    acc_ref[...] = jnp.zeros_like(acc_ref)

    def scores(start, width):
        kc = k_ref[0, pl.ds(start, width), :]
        return lax.dot_general(qs_ref[...], kc, (((1,), (1,)), ((), ())), preferred_element_type=F32)

    def update(s, start, width):
        vc = v_ref[0, pl.ds(start, width), :]
        m_old = m_ref[...]
        m_new = jnp.maximum(m_old, jnp.max(s, axis=-1, keepdims=True))
        alpha = jnp.exp2(m_old - m_new)
        p = jnp.exp2(s - jnp.tile(m_new, (1, width // LANES)))
        psum = p[:, 0:LANES]
        for n in range(1, width // LANES):
            psum = psum + p[:, n * LANES:(n + 1) * LANES]
        l_ref[...] = alpha * l_ref[...] + psum
        acc_ref[...] = alpha * acc_ref[...] + jnp.dot(p.astype(BF16), vc, preferred_element_type=F32)
        m_ref[...] = m_new

    @pl.when(i * tq < ctx_len)
    def _():
        update(scores(0, ctx_len), 0, ctx_len)

    n_kv = t_total // tk

    @pl.when(i * tq >= ctx_len)
    def _():
        sa_ref[...] = scores(0, tk)

        def body(c, carry):
            first = pl.multiple_of(2 * c * tk, tk)
            second = pl.multiple_of(first + tk, tk)
            third = pl.multiple_of(jnp.minimum(2 * c + 2, n_kv - 1) * tk, tk)
            sb_ref[...] = scores(second, tk)
            update(sa_ref[...], first, tk)
            sa_ref[...] = scores(third, tk)
            update(sb_ref[...], second, tk)
            return carry
        lax.fori_loop(0, n_kv // 2, body, 0)

    out = acc_ref[...] / jnp.sum(l_ref[...], axis=-1, keepdims=True)
    o_ref[0] = jnp.concatenate([out[h * tq:(h + 1) * tq] for h in range(g)], axis=-1).astype(BF16)


def _flash_attention(q, k, v, ctx_len):
    nb, t, qd = q.shape
    tq = 256
    tk = _row_tile(t, 640, LANES)
    assert ctx_len == tq and t % tq == 0 and ctx_len % LANES == 0 and (t // tk) % 2 == 0
    gw = ATTN_GROUP * ATTN_HEAD
    return pl.pallas_call(
        functools.partial(_flash_kernel, tq=tq, tk=tk, ctx_len=ctx_len, t_total=t),
        grid=(nb, ATTN_KV_HEADS, t // tq),
        in_specs=[pl.BlockSpec((1, tq, gw), lambda b, kv, i: (b, i, kv)),
                  pl.BlockSpec((1, t, ATTN_HEAD), lambda b, kv, i: (b, 0, kv)),
                  pl.BlockSpec((1, t, ATTN_HEAD), lambda b, kv, i: (b, 0, kv))],
        out_specs=pl.BlockSpec((1, tq, gw), lambda b, kv, i: (b, i, kv)),
        out_shape=jax.ShapeDtypeStruct((nb, t, qd), BF16),
        scratch_shapes=[pltpu.VMEM((ATTN_GROUP * tq, ATTN_HEAD), BF16),
                        pltpu.VMEM((ATTN_GROUP * tq, LANES), F32),
                        pltpu.VMEM((ATTN_GROUP * tq, LANES), F32),
                        pltpu.VMEM((ATTN_GROUP * tq, ATTN_HEAD), F32),
                        pltpu.VMEM((ATTN_GROUP * tq, tk), F32),
                        pltpu.VMEM((ATTN_GROUP * tq, tk), F32)],
        compiler_params=_params(("arbitrary", "arbitrary", "arbitrary")),
        name="flash_attention",
    )(q, k, v)


def _pad_cols(w, n):
    return jnp.pad(w, ((0, 0), (0, n - w.shape[1])))


def _deltanet_layer(xc, mod, norm_g, w_in, conv_w, a_log, dt_bias, out_norm_g, w_out, ctx_len):
    nb, t, _ = xc.shape
    n_pad = 7 * 896
    assert n_pad >= w_in.shape[1]
    p = _in_projection(xc, mod, norm_g, _pad_cols(w_in, n_pad).astype(BF16), 896, ctx_len)
    conv_w_t = jnp.pad(conv_w.T.astype(F32), ((0, 8 - SHORT_CONV), (0, 0)))
    qkv = _dn_conv(p, conv_w_t, ctx_len)
    gb = _dn_gates(p, (DN_QKV_DIM + DN_V_DIM) // LANES, a_log, dt_bias)
    g6 = gb[..., :4 * DN_V_HEADS].reshape(nb, t, 2, 2, DN_K_HEADS, DN_GROUP)
    g_cols = g6.transpose(0, 4, 1, 2, 3, 5).reshape(nb, DN_K_HEADS, t, 4 * DN_GROUP)
    g_rows = g_cols.reshape(nb, DN_K_HEADS, t // CHUNK, CHUNK, 4 * DN_GROUP).transpose(0, 1, 2, 4, 3)
    o_f, o_b = _dn_scan(qkv, g_cols, g_rows)
    return _out_projection([o_f, o_b], p, DN_QKV_DIM // DN_V_DIM, out_norm_g, DN_V_HEADS, DN_HEAD,
                           w_out.astype(BF16), xc, mod, ctx_len, 320)


def _gla_layer(xc, mod, norm_g, w_in, gate_w2, gate_b2, out_norm_g, w_out, ctx_len):
    n_pad = 5 * 640
    assert n_pad >= w_in.shape[1]
    p = _in_projection(xc, mod, norm_g, _pad_cols(w_in, n_pad).astype(BF16), 640, ctx_len)
    r = GLA_GATE_RANK
    w2cat = jnp.zeros((LANES, 2 * GLA_K_DIM), F32)
    w2cat = w2cat.at[0:r, :GLA_K_DIM].set(gate_w2[0]).at[r:2 * r, GLA_K_DIM:].set(gate_w2[1]).astype(BF16)
    b2cat = gate_b2.reshape(1, 2 * GLA_K_DIM).astype(F32)
    o_f, o_b = _gla_scan(p, w2cat, b2cat)
    z_block = (2 * GLA_K_DIM + GLA_V_DIM) // GLA_V_DIM
    return _out_projection([o_f, o_b], p, z_block, out_norm_g, GLA_HEADS, GLA_HEAD_V,
                           w_out.astype(BF16), xc, mod, ctx_len, 640)


def _attention_layer(xc, mod, norm_g, w_in, q_g, k_g, w_out, rope, ctx_len):
    p = _in_projection(xc, mod, norm_g, w_in.astype(BF16), 512, ctx_len)
    q, k, v = _attn_prep(p, q_g, k_g, *rope)
    o = _flash_attention(q, k, v, ctx_len)
    return _out_projection([o], None, 0, None, 0, 0, w_out.astype(BF16), xc, mod, ctx_len, 1280)


def kernel(x, c, ctx, c_ctx, ada_w, ada_b, norm_mix_g, norm_ffn_g, ffn_w1, ffn_w2, dn_w_in, dn_conv_w, dn_a_log, dn_dt_bias, dn_norm_g, dn_w_out, gla_w_in, gla_gate_w2, gla_gate_b2, gla_norm_g, gla_w_out, attn_w_in, attn_q_norm_g, attn_k_norm_g, attn_w_out):
    nb, seq, d = x.shape
    ctx_len = ctx.shape[1]
    depth = ada_w.shape[0]
    assert ctx_len == SCAN_BLOCK and seq % SCAN_BLOCK == 0 and nb < MOD_ROWS
    t = ctx_len + seq
    xc = jnp.concatenate([ctx, x], axis=1)
    cvec = jnp.zeros((MOD_ROWS, d), F32).at[:nb].set(c).at[nb].set(c_ctx)
    mods = _ada_vectors(cvec, ada_w, ada_b)
    rope = _rope_tables(t, ctx_len)
    for i in range(depth):
        mix, slot = i % 3, i // 3
        mod = mods[i]
        if mix == 0:
            xc = _deltanet_layer(xc, mod, norm_mix_g[i], dn_w_in[slot], dn_conv_w[slot], dn_a_log[slot],
                                 dn_dt_bias[slot], dn_norm_g[slot], dn_w_out[slot], ctx_len)
        elif mix == 1:
            xc = _gla_layer(xc, mod, norm_mix_g[i], gla_w_in[slot], gla_gate_w2[slot], gla_gate_b2[slot],
                            gla_norm_g[slot], gla_w_out[slot], ctx_len)
        else:
            xc = _attention_layer(xc, mod, norm_mix_g[i], attn_w_in[slot], attn_q_norm_g[slot],
                                  attn_k_norm_g[slot], attn_w_out[slot], rope, ctx_len)
        xc = _ffn(xc, mod, norm_ffn_g[i], ffn_w1[i].astype(BF16), ffn_w2[i].astype(BF16), ctx_len)
    return xc[:, ctx_len:, :]
```

```python
import functools
import math

import numpy as np
import jax
import jax.numpy as jnp
from jax import lax
from jax.experimental import pallas as pl
from jax.experimental.pallas import tpu as pltpu

F32 = jnp.float32
BF16 = jnp.bfloat16

NORM_EPS = 1e-6
GRID_W = 64
ROPE_THETA = 10000.0
SHORT_CONV = 5

DN_K_HEADS = 8
DN_V_HEADS = 16
DN_HEAD = 128
DN_GROUP = DN_V_HEADS // DN_K_HEADS
DN_K_DIM = DN_K_HEADS * DN_HEAD
DN_V_DIM = DN_V_HEADS * DN_HEAD
DN_QKV_DIM = 2 * DN_K_DIM + DN_V_DIM

GLA_HEADS = 4
GLA_HEAD_K = 128
GLA_HEAD_V = 256
GLA_K_DIM = GLA_HEADS * GLA_HEAD_K
GLA_V_DIM = GLA_HEADS * GLA_HEAD_V
GLA_GATE_RANK = 16
GLA_GATE_NORMALIZER = 16.0

ATTN_Q_HEADS = 8
ATTN_KV_HEADS = 2
ATTN_HEAD = 128
ATTN_GROUP = ATTN_Q_HEADS // ATTN_KV_HEADS

CHUNK = 64
SCAN_BLOCK = 256
LANES = 128
MOD_ROWS = 8
VMEM_LIMIT = 56 * 1024 * 1024


def _params(semantics, vmem=VMEM_LIMIT):
    return pltpu.CompilerParams(dimension_semantics=semantics, vmem_limit_bytes=vmem)


def _sigmoid(x):
    return 1.0 / (1.0 + jnp.exp(-x))


def _softplus(x):
    return jnp.maximum(x, 0.0) + jnp.log(1.0 + jnp.exp(-jnp.abs(x)))


def _split3(x):
    hi = x.astype(BF16)
    r1 = x - hi.astype(F32)
    mid = r1.astype(BF16)
    lo = (r1 - mid.astype(F32)).astype(BF16)
    return hi, mid, lo


def _dot(a, b):
    return jnp.dot(a.astype(BF16), b.astype(BF16), preferred_element_type=F32)


def _dot_nt(a, b):
    return lax.dot_general(a.astype(BF16), b.astype(BF16), (((1,), (1,)), ((), ())),
                           preferred_element_type=F32)


def _dot_tn(a, b):
    return lax.dot_general(a.astype(BF16), b.astype(BF16), (((0,), (0,)), ((), ())),
                           preferred_element_type=F32)


def _dot_sel(p_bf16, x):
    hi, mid, lo = _split3(x)
    d = lambda y: jnp.dot(p_bf16, y, preferred_element_type=F32)
    return d(hi) + d(mid) + d(lo)


def _dot_x3(a, b):
    ah = a.astype(BF16)
    al = (a - ah.astype(F32)).astype(BF16)
    bh = b.astype(BF16)
    bl = (b - bh.astype(F32)).astype(BF16)
    d = lambda x, y: jnp.dot(x, y, preferred_element_type=F32)
    return d(ah, bh) + d(ah, bl) + d(al, bh)


def _row_tile(total, target, multiple=8):
    best = None
    for t in range(multiple, min(total, target) + 1, multiple):
        if total % t == 0:
            best = t
    assert best is not None, (total, target, multiple)
    return best


def _mod_norm(x, g, mod_ref, b, row0, ctx_len, nb, shift_idx, scale_idx):
    d = x.shape[-1]
    ms = jnp.mean(x * x, axis=-1, keepdims=True)
    y = x * lax.rsqrt(ms + NORM_EPS) * g
    rows = row0 + lax.broadcasted_iota(jnp.int32, (x.shape[0], 1), 0)
    is_ctx = rows < ctx_len

    def pick(idx):
        vx = mod_ref[pl.ds(b, 1), idx * d:(idx + 1) * d]
        vc = mod_ref[nb:nb + 1, idx * d:(idx + 1) * d]
        return jnp.where(is_ctx, vc, vx)

    return y * (1.0 + pick(scale_idx)) + pick(shift_idx), is_ctx, pick


def _ada_kernel(c_ref, w_ref, b_ref, o_ref):
    c = c_ref[...]
    s = c * _sigmoid(c)
    o_ref[0] = jnp.dot(s, w_ref[0], preferred_element_type=F32,
                       precision=lax.Precision.HIGHEST) + b_ref[0]


def _ada_vectors(cvec, ada_w, ada_b):
    depth, d, n = ada_w.shape
    tn = _row_tile(n, 1536, LANES)
    return pl.pallas_call(
        _ada_kernel,
        grid=(depth, n // tn),
        in_specs=[pl.BlockSpec((MOD_ROWS, d), lambda l, j: (0, 0)),
                  pl.BlockSpec((1, d, tn), lambda l, j: (l, 0, j)),
                  pl.BlockSpec((1, 1, tn), lambda l, j: (l, 0, j))],
        out_specs=pl.BlockSpec((1, MOD_ROWS, tn), lambda l, j: (l, 0, j)),
        out_shape=jax.ShapeDtypeStruct((depth, MOD_ROWS, n), F32),
        compiler_params=_params(("arbitrary", "arbitrary")),
        name="ada_vectors",
    )(cvec, ada_w, ada_b.reshape(depth, 1, n))


def _inproj_kernel(x_ref, mod_ref, g_ref, w_ref, o_ref, h_ref, *, tm, ctx_len, nb):
    b = pl.program_id(0)
    i = pl.program_id(1)

    @pl.when(pl.program_id(2) == 0)
    def _():
        h, _, _ = _mod_norm(x_ref[0], g_ref[...], mod_ref, b, i * tm, ctx_len, nb, 0, 1)
        h_ref[...] = h.astype(BF16)

    o_ref[0] = jnp.dot(h_ref[...], w_ref[...], preferred_element_type=F32)


def _in_projection(xc, mod, g, w_bf16, tn, ctx_len):
    nb, t, d = xc.shape
    n = w_bf16.shape[1]
    tm = _row_tile(t, 1280)
    return pl.pallas_call(
        functools.partial(_inproj_kernel, tm=tm, ctx_len=ctx_len, nb=nb),
        grid=(nb, t // tm, n // tn),
        in_specs=[pl.BlockSpec((1, tm, d), lambda b, i, j: (b, i, 0)),
                  pl.BlockSpec(mod.shape, lambda b, i, j: (0, 0)),
                  pl.BlockSpec((1, d), lambda b, i, j: (0, 0)),
                  pl.BlockSpec((d, tn), lambda b, i, j: (0, j))],
        out_specs=pl.BlockSpec((1, tm, tn), lambda b, i, j: (b, i, j)),
        out_shape=jax.ShapeDtypeStruct((nb, t, n), F32),
        scratch_shapes=[pltpu.VMEM((tm, d), BF16)],
        compiler_params=_params(("arbitrary", "arbitrary", "arbitrary")),
        name="in_projection",
    )(xc, mod, g.reshape(1, d), w_bf16)


def _outproj_kernel(*refs, n_o, gated, heads, head_dim, tm, ctx_len, nb):
    o_refs = refs[:n_o]
    pos = n_o
    if gated:
        z_ref, ng_ref = refs[pos], refs[pos + 1]
        pos += 2
    w_ref, x_ref, mod_ref, out_ref = refs[pos:pos + 4]
    b = pl.program_id(0)
    i = pl.program_id(1)
    if gated:
        pieces = []
        for h in range(heads):
            sl = slice(h * head_dim, (h + 1) * head_dim)
            o = o_refs[0][0, :, sl]
            for r in o_refs[1:]:
                o = o + r[0, :, sl]
            ms = jnp.mean(o * o, axis=-1, keepdims=True)
            o = o * lax.rsqrt(ms + NORM_EPS) * ng_ref[...]
            z = z_ref[0, :, sl]
            pieces.append((o * (z * _sigmoid(z))).astype(BF16))
        lhs = jnp.concatenate(pieces, axis=-1)
    else:
        lhs = o_refs[0][0]
    y = jnp.dot(lhs, w_ref[...], preferred_element_type=F32)
    d = y.shape[-1]
    rows = i * tm + lax.broadcasted_iota(jnp.int32, (tm, 1), 0)
    gate = jnp.where(rows < ctx_len, mod_ref[nb:nb + 1, 2 * d:3 * d], mod_ref[pl.ds(b, 1), 2 * d:3 * d])
    out_ref[0] = x_ref[0] + gate * y


def _out_projection(o_list, z_src, z_col_block, norm_g, heads, head_dim, w_bf16, xc, mod, ctx_len, tm_target):
    nb, t, d = xc.shape
    dv = w_bf16.shape[0]
    tm = _row_tile(t, tm_target)
    gated = z_src is not None
    row_spec = lambda width, col: pl.BlockSpec((1, tm, width), lambda b, i: (b, i, col))
    in_specs = [row_spec(dv, 0) for _ in o_list]
    args = list(o_list)
    if gated:
        in_specs += [row_spec(dv, z_col_block), pl.BlockSpec((1, head_dim), lambda b, i: (0, 0))]
        args += [z_src, norm_g.reshape(1, head_dim)]
    in_specs += [pl.BlockSpec((dv, d), lambda b, i: (0, 0)), row_spec(d, 0),
                 pl.BlockSpec(mod.shape, lambda b, i: (0, 0))]
    args += [w_bf16, xc, mod]
    return pl.pallas_call(
        functools.partial(_outproj_kernel, n_o=len(o_list), gated=gated, heads=heads, head_dim=head_dim,
                          tm=tm, ctx_len=ctx_len, nb=nb),
        grid=(nb, t // tm),
        in_specs=in_specs,
        out_specs=row_spec(d, 0),
        out_shape=jax.ShapeDtypeStruct((nb, t, d), F32),
        compiler_params=_params(("arbitrary", "arbitrary")),
        name="out_projection",
    )(*args)


def _ffn_kernel(x_ref, mod_ref, g_ref, w1_ref, w2_ref, o_ref, acc_ref, *, tm, fk, ctx_len, nb):
    b = pl.program_id(0)
    i = pl.program_id(1)
    x = x_ref[0]
    h, _, pick = _mod_norm(x, g_ref[...], mod_ref, b, i * tm, ctx_len, nb, 3, 4)
    h = h.astype(BF16)
    d_ff = w1_ref.shape[1]
    for k in range(d_ff // fk):
        u = jnp.dot(h, w1_ref[:, k * fk:(k + 1) * fk], preferred_element_type=F32)
        u = jnp.maximum(u, 0.0)
        u = (u * u).astype(BF16)
        contrib = jnp.dot(u, w2_ref[k * fk:(k + 1) * fk, :], preferred_element_type=F32)
        if k == 0:
            acc_ref[...] = contrib
        else:
            acc_ref[...] += contrib
    o_ref[0] = x + pick(5) * acc_ref[...]


def _ffn(xc, mod, g, w1_bf16, w2_bf16, ctx_len):
    nb, t, d = xc.shape
    d_ff = w1_bf16.shape[1]
    tm = _row_tile(t, 640)
    return pl.pallas_call(
        functools.partial(_ffn_kernel, tm=tm, fk=512, ctx_len=ctx_len, nb=nb),
        grid=(nb, t // tm),
        in_specs=[pl.BlockSpec((1, tm, d), lambda b, i: (b, i, 0)),
                  pl.BlockSpec(mod.shape, lambda b, i: (0, 0)),
                  pl.BlockSpec((1, d), lambda b, i: (0, 0)),
                  pl.BlockSpec((d, d_ff), lambda b, i: (0, 0)),
                  pl.BlockSpec((d_ff, d), lambda b, i: (0, 0))],
        out_specs=pl.BlockSpec((1, tm, d), lambda b, i: (b, i, 0)),
        out_shape=jax.ShapeDtypeStruct((nb, t, d), F32),
        scratch_shapes=[pltpu.VMEM((tm, d), F32)],
        compiler_params=_params(("arbitrary", "arbitrary")),
        name="ffn",
    )(xc, mod, g.reshape(1, d), w1_bf16, w2_bf16)


def _dn_conv_kernel(p_ref, pp_ref, pn_ref, w_ref, o_ref, ext_ref, *, tm, t_total, ctx_len):
    i = pl.program_id(1)
    j = pl.program_id(2)
    ext_ref[0:8, :] = pp_ref[0]
    ext_ref[8:8 + tm, :] = p_ref[0]
    ext_ref[8 + tm:16 + tm, :] = pn_ref[0]
    n_ext = tm + 16
    scale = jnp.where(j == 0, DN_HEAD ** -0.5, 1.0).astype(F32)

    def conv_silu_norm(masked):
        t = i * tm + lax.broadcasted_iota(jnp.int32, (tm, 1), 0)
        seg = jnp.where(t >= ctx_len, 1, 0)
        for h in range(p_ref.shape[2] // DN_HEAD):
            cols = slice(h * DN_HEAD, (h + 1) * DN_HEAD)
            x_ext = ext_ref[:, cols]
            acc = None
            for tap in range(SHORT_CONV):
                d = tap - SHORT_CONV // 2
                xs = (x_ext if d == 0 else pltpu.roll(x_ext, (-d) % n_ext, 0))[8:8 + tm]
                if masked and d != 0:
                    td = t + d
                    ok = jnp.where(td >= 0, 1, 0) * jnp.where(td < t_total, 1, 0) * jnp.where(
                        jnp.where(td >= ctx_len, 1, 0) == seg, 1, 0)
                    xs = jnp.where(ok > 0, xs, 0.0)
                term = xs * w_ref[tap:tap + 1, cols]
                acc = term if acc is None else acc + term
            y = acc * (0.5 * (1.0 + jnp.tanh(0.5 * acc)))
            ss = jnp.sum(y * y, axis=-1, keepdims=True)
            o_ref[0, :, cols] = y * jnp.where(j < 2, lax.rsqrt(ss + NORM_EPS) * scale, 1.0)

    row0 = i * tm
    pad = SHORT_CONV // 2
    edge = ((row0 == 0) | (row0 + tm == t_total)
            | ((row0 - pad < ctx_len) & (row0 + tm + pad > ctx_len)))

    @pl.when(edge)
    def _():
        conv_silu_norm(True)

    @pl.when(jnp.logical_not(edge))
    def _():
        conv_silu_norm(False)


def _dn_conv(p, conv_w_t, ctx_len):
    nb, t, _ = p.shape
    tc = 1024
    tm = _row_tile(t, 640)
    hb = tm // 8
    last = t // 8 - 1
    return pl.pallas_call(
        functools.partial(_dn_conv_kernel, tm=tm, t_total=t, ctx_len=ctx_len),
        grid=(nb, t // tm, DN_QKV_DIM // tc),
        in_specs=[pl.BlockSpec((1, tm, tc), lambda b, i, j: (b, i, j)),
                  pl.BlockSpec((1, 8, tc), lambda b, i, j: (b, jnp.maximum(i * hb - 1, 0), j)),
                  pl.BlockSpec((1, 8, tc), lambda b, i, j: (b, jnp.minimum((i + 1) * hb, last), j)),
                  pl.BlockSpec((8, tc), lambda b, i, j: (0, j))],
        out_specs=pl.BlockSpec((1, tm, tc), lambda b, i, j: (b, i, j)),
        out_shape=jax.ShapeDtypeStruct((nb, t, DN_QKV_DIM), F32),
        scratch_shapes=[pltpu.VMEM((tm + 16, tc), F32)],
        compiler_params=_params(("arbitrary", "arbitrary", "arbitrary")),
        name="dn_conv",
    )(p, p, p, conv_w_t)


def _dn_gate_kernel(ab_ref, par_ref, o_ref, *, tm):
    x = ab_ref[0]
    g = -jnp.exp(par_ref[0:1, :]) * _softplus(x + par_ref[1:2, :])
    beta = _sigmoid(x)
    lane = lax.broadcasted_iota(jnp.int32, (1, LANES), 1)
    r = lax.broadcasted_iota(jnp.int32, (CHUNK, CHUNK), 0)
    c = lax.broadcasted_iota(jnp.int32, (CHUNK, CHUNK), 1)
    lower = jnp.where(c <= r, 1.0, 0.0).astype(BF16)
    upper = jnp.where(c >= r, 1.0, 0.0).astype(BF16)
    nh = DN_V_HEADS
    for k in range(tm // CHUNK):
        rows = slice(k * CHUNK, (k + 1) * CHUNK)
        gk = g[rows]
        fwd = _dot_sel(lower, gk)
        bwd = _dot_sel(upper, gk)
        o_ref[0, rows, :] = jnp.where(lane < nh, fwd,
                                      jnp.where(lane < 2 * nh, bwd,
                                                jnp.where(lane < 4 * nh, beta[rows], 0.0)))


def _dn_gates(p, ab_col_block, a_log, dt_bias):
    nb, t, _ = p.shape
    tm = _row_tile(t, 1280, CHUNK)
    par = jnp.zeros((8, LANES), F32)
    par = par.at[0, :2 * DN_V_HEADS].set(a_log.reshape(-1).astype(F32))
    par = par.at[1, :2 * DN_V_HEADS].set(dt_bias.reshape(-1).astype(F32))
    return pl.pallas_call(
        functools.partial(_dn_gate_kernel, tm=tm),
        grid=(nb, t // tm),
        in_specs=[pl.BlockSpec((1, tm, LANES), lambda b, i: (b, i, ab_col_block)),
                  pl.BlockSpec((8, LANES), lambda b, i: (0, 0))],
        out_specs=pl.BlockSpec((1, tm, LANES), lambda b, i: (b, i, 0)),
        out_shape=jax.ShapeDtypeStruct((nb, t, LANES), F32),
        compiler_params=_params(("arbitrary", "arbitrary")),
        name="dn_gates",
    )(p, par)


def _bdot(a, b):
    return jnp.einsum("nij,njk->nik", a.astype(BF16), b.astype(BF16), preferred_element_type=F32)


def _bdot_nt(a, b):
    return jnp.einsum("nid,njd->nij", a.astype(BF16), b.astype(BF16), preferred_element_type=F32)


def _bdot_tn(a, b):
    return jnp.einsum("nci,ncj->nij", a.astype(BF16), b.astype(BF16), preferred_element_type=F32)


N_LEVELS = int(math.log2(CHUNK))


def _unit_triangular_inverse(parts, eye):
    coupling = lambda lv: jnp.concatenate([jnp.where(masks[lv], m, 0.0) for m, masks in parts], axis=0)
    x = eye - coupling(0)
    for lv in range(1, N_LEVELS):
        x = x - _bdot(x, _bdot(coupling(lv), x))
    return x


def _coupling_masks(ri, ci, reverse):
    ti, tj = (CHUNK - 1 - ri, CHUNK - 1 - ci) if reverse else (ri, ci)
    masks = []
    for lv in range(N_LEVELS):
        bi, bj = lax.shift_right_logical(ti, lv), lax.shift_right_logical(tj, lv)
        masks.append(jnp.where((bi & 1) == 1, bi - 1, -1) == bj)
    return masks


def _dn_scan_kernel(qf_ref, kf_ref, vf_ref, gf_ref, gtf_ref,
                    qr_ref, kr_ref, vr_ref, gr_ref, gtr_ref,
                    of_ref, ob_ref, s_ref):
    @pl.when(pl.program_id(2) == 0)
    def _():
        s_ref[...] = jnp.zeros_like(s_ref)

    ri = lax.broadcasted_iota(jnp.int32, (CHUNK, CHUNK), 0)
    ci = lax.broadcasted_iota(jnp.int32, (CHUNK, CHUNK), 1)
    eye = jnp.where(ri == ci, 1.0, 0.0)
    nc = SCAN_BLOCK // CHUNK
    chunk_rows = [slice(c * CHUNK, (c + 1) * CHUNK) for c in range(nc)]
    dirs = ((qf_ref, kf_ref, vf_ref, gf_ref, gtf_ref, False),
            (qr_ref, kr_ref, vr_ref, gr_ref, gtr_ref, True))
    m_parts, rhs_l, a_l, qg_l, kt_l, egl_l = [], [], [], [], [], []
    for d, (q_ref, k_ref, v_ref, g_ref, gt_ref, reverse) in enumerate(dirs):
        incl = (ci >= ri) if reverse else (ci <= ri)
        strict = (ci > ri) if reverse else (ci < ri)
        last = 0 if reverse else CHUNK - 1
        m_l = []
        q = jnp.stack([q_ref[0, r, :] for r in chunk_rows])
        k = jnp.stack([k_ref[0, r, :] for r in chunk_rows])
        kk = _bdot_nt(k, k)
        qk = _bdot_nt(q, k)
        for j in range(DN_GROUP):
            col = 2 * d + j
            gc = jnp.stack([g_ref[0, 0, r, col:col + 1] for r in chunk_rows])
            bc = jnp.stack([g_ref[0, 0, r, 4 + col:5 + col] for r in chunk_rows])
            gr = jnp.stack([gt_ref[0, 0, c, col:col + 1, :] for c in range(nc)])
            v = jnp.stack([v_ref[0, r, j * DN_HEAD:(j + 1) * DN_HEAD] for r in chunk_rows])
            decay = jnp.where(incl, jnp.exp(jnp.where(incl, gc - gr, 0.0)), 0.0)
            eg = jnp.exp(gc)
            gl = gc[:, last:last + 1, :]
            m_l.append(jnp.where(strict, bc * kk * decay, 0.0))
            rhs_l.append(jnp.concatenate([k * (bc * eg), v * bc], axis=-1))
            a_l.append(qk * decay)
            qg_l.append(q * eg)
            kt_l.append(k * jnp.exp(gl - gc))
            egl_l.append(jnp.exp(gl))
        m_parts.append((jnp.concatenate(m_l, axis=0), _coupling_masks(ri, ci, reverse)))
    cat = lambda xs: jnp.concatenate(xs, axis=0)
    a, kt = cat(a_l), cat(kt_l)
    wu = _bdot(_unit_triangular_inverse(m_parts, eye), cat(rhs_l))
    kb = _bdot_tn(kt, wu)
    qo = _bdot(a, wu)
    qeff = cat(qg_l) - qo[:, :, :DN_HEAD]
    egl = cat(egl_l)
    state = s_ref[...]
    o_refs = (of_ref, ob_ref)
    for step in range(nc):
        chunk_of = [step if d == 0 else nc - 1 - step for d in range(2) for _ in range(DN_GROUP)]
        idx = [ch * nc + c for ch, c in enumerate(chunk_of)]
        pick = lambda x: jnp.stack([x[n] for n in idx])
        o = _bdot(pick(qeff), state) + pick(qo)[:, :, DN_HEAD:]
        kb_s = pick(kb)
        state = pick(egl) * state + kb_s[:, :, DN_HEAD:] - _bdot(kb_s[:, :, :DN_HEAD], state)
        for ch, c in enumerate(chunk_of):
            j = ch % DN_GROUP
            o_refs[ch // DN_GROUP][0, chunk_rows[c], j * DN_HEAD:(j + 1) * DN_HEAD] = o[ch]
    s_ref[...] = state


def _dn_scan(qkv, g_cols, g_rows):
    nb, t, _ = qkv.shape
    n_blocks = t // SCAN_BLOCK
    n_chunks = SCAN_BLOCK // CHUNK
    kh = DN_K_HEADS
    fwd = lambda s: s
    bwd = lambda s: jnp.where(s == 0, 0, n_blocks - s)

    def specs(order):
        return [pl.BlockSpec((1, SCAN_BLOCK, DN_HEAD), lambda b, h, s: (b, order(s), h)),
                pl.BlockSpec((1, SCAN_BLOCK, DN_HEAD), lambda b, h, s: (b, order(s), kh + h)),
                pl.BlockSpec((1, SCAN_BLOCK, DN_GROUP * DN_HEAD), lambda b, h, s: (b, order(s), kh + h)),
                pl.BlockSpec((1, 1, SCAN_BLOCK, 8), lambda b, h, s: (b, h, order(s), 0)),
                pl.BlockSpec((1, 1, n_chunks, 8, CHUNK), lambda b, h, s: (b, h, order(s), 0, 0))]

    out_spec = lambda order: pl.BlockSpec((1, SCAN_BLOCK, DN_GROUP * DN_HEAD), lambda b, h, s: (b, order(s), h))
    o_shape = jax.ShapeDtypeStruct((nb, t, DN_V_DIM), F32)
    return pl.pallas_call(
        _dn_scan_kernel,
        grid=(nb, kh, n_blocks),
        in_specs=specs(fwd) + specs(bwd),
        out_specs=[out_spec(fwd), out_spec(bwd)],
        out_shape=[o_shape, o_shape],
        scratch_shapes=[pltpu.VMEM((2 * DN_GROUP, DN_HEAD, DN_HEAD), F32)],
        compiler_params=_params(("arbitrary", "arbitrary", "arbitrary")),
        name="dn_scan",
    )(qkv, qkv, qkv, g_cols, g_rows, qkv, qkv, qkv, g_cols, g_rows)


_GLA_LEVELS = (32, 16, 8, 4, 2, 1)


def _gla_tables(reverse):
    idx = np.arange(CHUNK)
    tau = (CHUNK - 1 - idx) if reverse else idx
    ti, tk = tau[:, None], tau[None, :]
    groups = [tk <= ti, tk > ti]
    masks = []
    for s in _GLA_LEVELS:
        bi, bk = ti // s, tk // s
        groups.append((bk == bi) & (tk <= ti) & (tk > bi * s))
        groups.append(((bk == bi) & (tk > ti)) | (tk == (bi + 1) * s))
        masks.append((bi % 2 == 1) & (bk == bi - 1))
    masks.append(ti == tk)
    sel = np.concatenate(groups, axis=0).astype(np.float32)
    return jnp.asarray(sel, BF16), jnp.asarray(np.stack(masks).astype(np.float32))


def _gla_scan_kernel(qf_ref, kf_ref, vf_ref, lf_ref, wf_ref, bf_ref, self_ref, mf_ref,
                     qr_ref, kr_ref, vr_ref, lr_ref, wr_ref, br_ref, selr_ref, mr_ref,
                     of_ref, ob_ref, s_ref):
    @pl.when(pl.program_id(2) == 0)
    def _():
        s_ref[...] = jnp.zeros_like(s_ref)

    nc = SCAN_BLOCK // CHUNK
    n_lev = len(_GLA_LEVELS)
    chunk_rows = [slice(c * CHUNK, (c + 1) * CHUNK) for c in range(nc)]
    dirs = ((qf_ref, kf_ref, vf_ref, lf_ref, wf_ref, bf_ref, self_ref, mf_ref, False),
            (qr_ref, kr_ref, vr_ref, lr_ref, wr_ref, br_ref, selr_ref, mr_ref, True))
    ql, kl, qg_l, kt_l, e_l, v_l = [], [], [], [], [], []
    for q_ref, k_ref, v_ref, l_ref, w_ref, b_ref, sel_ref, m_ref, reverse in dirs:
        last = 0 if reverse else CHUNK - 1
        logits = _dot(l_ref[0], w_ref[...]) + b_ref[...]
        gk_all = (jnp.minimum(logits, 0.0) - jnp.log(1.0 + jnp.exp(-jnp.abs(logits)))) * (1.0 / GLA_GATE_NORMALIZER)
        for rows in chunk_rows:
            q = q_ref[0, rows, :] * (GLA_HEAD_K ** -0.5)
            k = k_ref[0, rows, :]
            gk = gk_all[rows]
            hi = gk.astype(BF16)
            mid = (gk - hi.astype(F32)).astype(BF16)
            both = jnp.dot(sel_ref[...], jnp.concatenate([hi, mid], axis=-1), preferred_element_type=F32)
            sums = both[:, :GLA_HEAD_K] + both[:, GLA_HEAD_K:]
            part = lambda n: sums[n * CHUNK:(n + 1) * CHUNK]
            bcum, tail = part(0), part(1)
            ql += [q * jnp.exp(part(2 + 2 * lv)) for lv in range(n_lev)] + [q]
            kl += [k * jnp.exp(part(3 + 2 * lv)) for lv in range(n_lev)] + [k]
            qg_l.append(q * jnp.exp(bcum))
            kt_l.append(k * jnp.exp(tail))
            e_l.append(jnp.exp(bcum[last:last + 1, :]))
            v_l.append(v_ref[0, rows, :])
    scores = _bdot_nt(jnp.stack(ql), jnp.stack(kl))
    a_l = []
    for n in range(2 * nc):
        m_ref = dirs[n // nc][7]
        a = m_ref[0] * scores[n * (n_lev + 1)]
        for lv in range(1, n_lev + 1):
            a = a + m_ref[lv] * scores[n * (n_lev + 1) + lv]
        a_l.append(a)
    v = jnp.stack(v_l)
    x = _bdot_tn(v, jnp.stack(kt_l))
    s_l = [None] * (2 * nc)
    for d in range(2):
        state = s_ref[d]
        for c in (range(nc - 1, -1, -1) if dirs[d][8] else range(nc)):
            n = d * nc + c
            s_l[n] = state
            state = state * e_l[n] + x[n]
        s_ref[d] = state
    o = _bdot(jnp.stack(a_l), v) + _bdot_nt(jnp.stack(qg_l), jnp.stack(s_l))
    for n in range(2 * nc):
        (of_ref, ob_ref)[n // nc][0, chunk_rows[n % nc], :] = o[n]


def _gla_scan(p, w2cat, b2cat):
    nb, t, _ = p.shape
    n_blocks = t // SCAN_BLOCK
    h = GLA_HEADS
    low_block = (2 * GLA_K_DIM + 2 * GLA_V_DIM) // LANES
    fwd = lambda s: s
    bwd = lambda s: jnp.where(s == 0, 0, n_blocks - s)
    n_sel = (2 + 2 * len(_GLA_LEVELS)) * CHUNK
    n_mask = len(_GLA_LEVELS) + 1

    def specs(order, d):
        return [pl.BlockSpec((1, SCAN_BLOCK, GLA_HEAD_K), lambda b, hh, s: (b, order(s), hh)),
                pl.BlockSpec((1, SCAN_BLOCK, GLA_HEAD_K), lambda b, hh, s: (b, order(s), h + hh)),
                pl.BlockSpec((1, SCAN_BLOCK, GLA_HEAD_V), lambda b, hh, s: (b, order(s), h + hh)),
                pl.BlockSpec((1, SCAN_BLOCK, LANES), lambda b, hh, s: (b, order(s), low_block)),
                pl.BlockSpec((LANES, GLA_HEAD_K), lambda b, hh, s: (0, d * h + hh)),
                pl.BlockSpec((1, GLA_HEAD_K), lambda b, hh, s: (0, d * h + hh)),
                pl.BlockSpec((n_sel, CHUNK), lambda b, hh, s: (0, 0)),
                pl.BlockSpec((n_mask, CHUNK, CHUNK), lambda b, hh, s: (0, 0, 0))]

    out_spec = lambda order: pl.BlockSpec((1, SCAN_BLOCK, GLA_HEAD_V), lambda b, hh, s: (b, order(s), hh))
    o_shape = jax.ShapeDtypeStruct((nb, t, GLA_V_DIM), F32)
    sel_f, mask_f = _gla_tables(False)
    sel_r, mask_r = _gla_tables(True)
    return pl.pallas_call(
        _gla_scan_kernel,
        grid=(nb, h, n_blocks),
        in_specs=specs(fwd, 0) + specs(bwd, 1),
        out_specs=[out_spec(fwd), out_spec(bwd)],
        out_shape=[o_shape, o_shape],
        scratch_shapes=[pltpu.VMEM((2, GLA_HEAD_V, GLA_HEAD_K), F32)],
        compiler_params=_params(("arbitrary", "arbitrary", "arbitrary")),
        name="gla_scan",
    )(p, p, p, p, w2cat, b2cat, sel_f, mask_f, p, p, p, p, w2cat, b2cat, sel_r, mask_r)


def _attn_prep_kernel(p_ref, qg_ref, kg_ref, cos_ref, sin_ref, q_ref, k_ref, v_ref):
    cos = cos_ref[...]
    sin = sin_ref[...]
    lane = lax.broadcasted_iota(jnp.int32, (1, ATTN_HEAD), 1)
    first = (lane % (ATTN_HEAD // 2)) < (ATTN_HEAD // 4)
    q_scale = ATTN_HEAD ** -0.5 * math.log2(math.e)

    def norm_rope(x, g):
        ms = jnp.mean(x * x, axis=-1, keepdims=True)
        y = x * lax.rsqrt(ms + NORM_EPS) * g
        partner = jnp.where(first, pltpu.roll(y, ATTN_HEAD - ATTN_HEAD // 4, 1), pltpu.roll(y, ATTN_HEAD // 4, 1))
        return y * cos + partner * sin

    qd = ATTN_Q_HEADS * ATTN_HEAD
    kd = ATTN_KV_HEADS * ATTN_HEAD
    q_pieces = [norm_rope(p_ref[0, :, h * ATTN_HEAD:(h + 1) * ATTN_HEAD], qg_ref[...]) * q_scale
                for h in range(ATTN_Q_HEADS)]
    q_ref[0] = jnp.concatenate(q_pieces, axis=-1).astype(BF16)
    k_pieces = [norm_rope(p_ref[0, :, qd + h * ATTN_HEAD:qd + (h + 1) * ATTN_HEAD], kg_ref[...])
                for h in range(ATTN_KV_HEADS)]
    k_ref[0] = jnp.concatenate(k_pieces, axis=-1).astype(BF16)
    v_ref[0] = p_ref[0, :, qd + kd:qd + 2 * kd].astype(BF16)


def _rope_tables(t, ctx_len):
    pos = jnp.arange(t - ctx_len)
    row = (pos // GRID_W).astype(F32)
    col = (pos % GRID_W).astype(F32)
    axis_dim = ATTN_HEAD // 2
    inv_freq = jnp.power(ROPE_THETA, -jnp.arange(0, axis_dim, 2, dtype=F32) / axis_dim)
    ar, ac = row[:, None] * inv_freq, col[:, None] * inv_freq
    cos = jnp.concatenate([jnp.cos(ar), jnp.cos(ar), jnp.cos(ac), jnp.cos(ac)], axis=-1)
    sin = jnp.concatenate([-jnp.sin(ar), jnp.sin(ar), -jnp.sin(ac), jnp.sin(ac)], axis=-1)
    cos = jnp.concatenate([jnp.ones((ctx_len, ATTN_HEAD), F32), cos], axis=0)
    sin = jnp.concatenate([jnp.zeros((ctx_len, ATTN_HEAD), F32), sin], axis=0)
    return cos, sin


def _attn_prep(p, q_g, k_g, cos, sin):
    nb, t, n = p.shape
    tm = _row_tile(t, 640)
    qd = ATTN_Q_HEADS * ATTN_HEAD
    kd = ATTN_KV_HEADS * ATTN_HEAD
    row = lambda w: pl.BlockSpec((1, tm, w), lambda b, i: (b, i, 0))
    tab = pl.BlockSpec((tm, ATTN_HEAD), lambda b, i: (i, 0))
    vec = pl.BlockSpec((1, ATTN_HEAD), lambda b, i: (0, 0))
    return pl.pallas_call(
        _attn_prep_kernel,
        grid=(nb, t // tm),
        in_specs=[row(n), vec, vec, tab, tab],
        out_specs=[row(qd), row(kd), row(kd)],
        out_shape=[jax.ShapeDtypeStruct((nb, t, qd), BF16),
                   jax.ShapeDtypeStruct((nb, t, kd), BF16),
                   jax.ShapeDtypeStruct((nb, t, kd), BF16)],
        compiler_params=_params(("arbitrary", "arbitrary")),
        name="attn_prep",
    )(p, q_g.reshape(1, -1), k_g.reshape(1, -1), cos, sin)


def _flash_kernel(q_ref, k_ref, v_ref, o_ref, qs_ref, m_ref, l_ref, acc_ref, sa_ref, sb_ref, *, tq, tk, ctx_len, t_total):
    i = pl.program_id(2)
    g, hd = ATTN_GROUP, ATTN_HEAD
    for h in range(g):
        qs_ref[h * tq:(h + 1) * tq, :] = q_ref[0, :, h * hd:(h + 1) * hd]
    m_ref[...] = jnp.full_like(m_ref, -jnp.inf)
    l_ref[...] = jnp.zeros_like(l_ref)
    acc_ref[...] = jnp.zeros_like(acc_ref)

    def scores(start, width):
        kc = k_ref[0, pl.ds(start, width), :]
        return lax.dot_general(qs_ref[...], kc, (((1,), (1,)), ((), ())), preferred_element_type=F32)

    def update(s, start, width):
        vc = v_ref[0, pl.ds(start, width), :]
        m_old = m_ref[...]
        m_new = jnp.maximum(m_old, jnp.max(s, axis=-1, keepdims=True))
        alpha = jnp.exp2(m_old - m_new)
        p = jnp.exp2(s - jnp.tile(m_new, (1, width // LANES)))
        psum = p[:, 0:LANES]
        for n in range(1, width // LANES):
            psum = psum + p[:, n * LANES:(n + 1) * LANES]
        l_ref[...] = alpha * l_ref[...] + psum
        acc_ref[...] = alpha * acc_ref[...] + jnp.dot(p.astype(BF16), vc, preferred_element_type=F32)
        m_ref[...] = m_new

    @pl.when(i * tq < ctx_len)
    def _():
        update(scores(0, ctx_len), 0, ctx_len)

    n_kv = t_total // tk

    @pl.when(i * tq >= ctx_len)
    def _():
        sa_ref[...] = scores(0, tk)

        def body(c, carry):
            first = pl.multiple_of(2 * c * tk, tk)
            second = pl.multiple_of(first + tk, tk)
            third = pl.multiple_of(jnp.minimum(2 * c + 2, n_kv - 1) * tk, tk)
            sb_ref[...] = scores(second, tk)
            update(sa_ref[...], first, tk)
            sa_ref[...] = scores(third, tk)
            update(sb_ref[...], second, tk)
            return carry
        lax.fori_loop(0, n_kv // 2, body, 0)

    out = acc_ref[...] / jnp.sum(l_ref[...], axis=-1, keepdims=True)
    o_ref[0] = jnp.concatenate([out[h * tq:(h + 1) * tq] for h in range(g)], axis=-1).astype(BF16)


def _flash_attention(q, k, v, ctx_len):
    nb, t, qd = q.shape
    tq = 256
    tk = _row_tile(t, 640, LANES)
    assert ctx_len == tq and t % tq == 0 and ctx_len % LANES == 0 and (t // tk) % 2 == 0
    gw = ATTN_GROUP * ATTN_HEAD
    return pl.pallas_call(
        functools.partial(_flash_kernel, tq=tq, tk=tk, ctx_len=ctx_len, t_total=t),
        grid=(nb, ATTN_KV_HEADS, t // tq),
        in_specs=[pl.BlockSpec((1, tq, gw), lambda b, kv, i: (b, i, kv)),
                  pl.BlockSpec((1, t, ATTN_HEAD), lambda b, kv, i: (b, 0, kv)),
                  pl.BlockSpec((1, t, ATTN_HEAD), lambda b, kv, i: (b, 0, kv))],
        out_specs=pl.BlockSpec((1, tq, gw), lambda b, kv, i: (b, i, kv)),
        out_shape=jax.ShapeDtypeStruct((nb, t, qd), BF16),
        scratch_shapes=[pltpu.VMEM((ATTN_GROUP * tq, ATTN_HEAD), BF16),
                        pltpu.VMEM((ATTN_GROUP * tq, LANES), F32),
                        pltpu.VMEM((ATTN_GROUP * tq, LANES), F32),
                        pltpu.VMEM((ATTN_GROUP * tq, ATTN_HEAD), F32),
                        pltpu.VMEM((ATTN_GROUP * tq, tk), F32),
                        pltpu.VMEM((ATTN_GROUP * tq, tk), F32)],
        compiler_params=_params(("arbitrary", "arbitrary", "arbitrary")),
        name="flash_attention",
    )(q, k, v)


def _pad_cols(w, n):
    return jnp.pad(w, ((0, 0), (0, n - w.shape[1])))


def _deltanet_layer(xc, mod, norm_g, w_in, conv_w, a_log, dt_bias, out_norm_g, w_out, ctx_len):
    nb, t, _ = xc.shape
    n_pad = 7 * 896
    assert n_pad >= w_in.shape[1]
    p = _in_projection(xc, mod, norm_g, _pad_cols(w_in, n_pad).astype(BF16), 896, ctx_len)
    conv_w_t = jnp.pad(conv_w.T.astype(F32), ((0, 8 - SHORT_CONV), (0, 0)))
    qkv = _dn_conv(p, conv_w_t, ctx_len)
    gb = _dn_gates(p, (DN_QKV_DIM + DN_V_DIM) // LANES, a_log, dt_bias)
    g6 = gb[..., :4 * DN_V_HEADS].reshape(nb, t, 2, 2, DN_K_HEADS, DN_GROUP)
    g_cols = g6.transpose(0, 4, 1, 2, 3, 5).reshape(nb, DN_K_HEADS, t, 4 * DN_GROUP)
    g_rows = g_cols.reshape(nb, DN_K_HEADS, t // CHUNK, CHUNK, 4 * DN_GROUP).transpose(0, 1, 2, 4, 3)
    o_f, o_b = _dn_scan(qkv, g_cols, g_rows)
    return _out_projection([o_f, o_b], p, DN_QKV_DIM // DN_V_DIM, out_norm_g, DN_V_HEADS, DN_HEAD,
                           w_out.astype(BF16), xc, mod, ctx_len, 320)


def _gla_layer(xc, mod, norm_g, w_in, gate_w2, gate_b2, out_norm_g, w_out, ctx_len):
    n_pad = 5 * 640
    assert n_pad >= w_in.shape[1]
    p = _in_projection(xc, mod, norm_g, _pad_cols(w_in, n_pad).astype(BF16), 640, ctx_len)
    r = GLA_GATE_RANK
    w2cat = jnp.zeros((LANES, 2 * GLA_K_DIM), F32)
    w2cat = w2cat.at[0:r, :GLA_K_DIM].set(gate_w2[0]).at[r:2 * r, GLA_K_DIM:].set(gate_w2[1]).astype(BF16)
    b2cat = gate_b2.reshape(1, 2 * GLA_K_DIM).astype(F32)
    o_f, o_b = _gla_scan(p, w2cat, b2cat)
    z_block = (2 * GLA_K_DIM + GLA_V_DIM) // GLA_V_DIM
    return _out_projection([o_f, o_b], p, z_block, out_norm_g, GLA_HEADS, GLA_HEAD_V,
                           w_out.astype(BF16), xc, mod, ctx_len, 640)


def _attention_layer(xc, mod, norm_g, w_in, q_g, k_g, w_out, rope, ctx_len):
    p = _in_projection(xc, mod, norm_g, w_in.astype(BF16), 512, ctx_len)
    q, k, v = _attn_prep(p, q_g, k_g, *rope)
    o = _flash_attention(q, k, v, ctx_len)
    return _out_projection([o], None, 0, None, 0, 0, w_out.astype(BF16), xc, mod, ctx_len, 1280)


def kernel(x, c, ctx, c_ctx, ada_w, ada_b, norm_mix_g, norm_ffn_g, ffn_w1, ffn_w2, dn_w_in, dn_conv_w, dn_a_log, dn_dt_bias, dn_norm_g, dn_w_out, gla_w_in, gla_gate_w2, gla_gate_b2, gla_norm_g, gla_w_out, attn_w_in, attn_q_norm_g, attn_k_norm_g, attn_w_out):
    nb, seq, d = x.shape
    ctx_len = ctx.shape[1]
    depth = ada_w.shape[0]
    assert ctx_len == SCAN_BLOCK and seq % SCAN_BLOCK == 0 and nb < MOD_ROWS
    t = ctx_len + seq
    xc = jnp.concatenate([ctx, x], axis=1)
    cvec = jnp.zeros((MOD_ROWS, d), F32).at[:nb].set(c).at[nb].set(c_ctx)
    mods = _ada_vectors(cvec, ada_w, ada_b)
    rope = _rope_tables(t, ctx_len)
    for i in range(depth):
        mix, slot = i % 3, i // 3
        mod = mods[i]
        if mix == 0:
            xc = _deltanet_layer(xc, mod, norm_mix_g[i], dn_w_in[slot], dn_conv_w[slot], dn_a_log[slot],
                                 dn_dt_bias[slot], dn_norm_g[slot], dn_w_out[slot], ctx_len)
        elif mix == 1:
            xc = _gla_layer(xc, mod, norm_mix_g[i], gla_w_in[slot], gla_gate_w2[slot], gla_gate_b2[slot],
                            gla_norm_g[slot], gla_w_out[slot], ctx_len)
        else:
            xc = _attention_layer(xc, mod, norm_mix_g[i], attn_w_in[slot], attn_q_norm_g[slot],
                                  attn_k_norm_g[slot], attn_w_out[slot], rope, ctx_len)
        xc = _ffn(xc, mod, norm_ffn_g[i], ffn_w1[i].astype(BF16), ffn_w2[i].astype(BF16), ctx_len)
    return xc[:, ctx_len:, :]
```

```python
import functools
import math

import numpy as np
import jax
import jax.numpy as jnp
from jax import lax
from jax.experimental import pallas as pl
from jax.experimental.pallas import tpu as pltpu

F32 = jnp.float32
BF16 = jnp.bfloat16

NORM_EPS = 1e-6
GRID_W = 64
ROPE_THETA = 10000.0
SHORT_CONV = 5

DN_K_HEADS = 8
DN_V_HEADS = 16
DN_HEAD = 128
DN_GROUP = DN_V_HEADS // DN_K_HEADS
DN_K_DIM = DN_K_HEADS * DN_HEAD
DN_V_DIM = DN_V_HEADS * DN_HEAD
DN_QKV_DIM = 2 * DN_K_DIM + DN_V_DIM
DN_KH_STEP = 2

GLA_HEADS = 4
GLA_HEAD_K = 128
GLA_HEAD_V = 256
GLA_K_DIM = GLA_HEADS * GLA_HEAD_K
GLA_V_DIM = GLA_HEADS * GLA_HEAD_V
GLA_GATE_RANK = 16
GLA_GATE_NORMALIZER = 16.0

ATTN_Q_HEADS = 8
ATTN_KV_HEADS = 2
ATTN_HEAD = 128
ATTN_GROUP = ATTN_Q_HEADS // ATTN_KV_HEADS

CHUNK = 64
SCAN_BLOCK = 256
LANES = 128
MOD_ROWS = 8
VMEM_LIMIT = 56 * 1024 * 1024


def _params(semantics, vmem=VMEM_LIMIT):
    return pltpu.CompilerParams(dimension_semantics=semantics, vmem_limit_bytes=vmem)


def _sigmoid(x):
    return 1.0 / (1.0 + jnp.exp(-x))


def _softplus(x):
    return jnp.maximum(x, 0.0) + jnp.log(1.0 + jnp.exp(-jnp.abs(x)))


def _split3(x):
    hi = x.astype(BF16)
    r1 = x - hi.astype(F32)
    mid = r1.astype(BF16)
    lo = (r1 - mid.astype(F32)).astype(BF16)
    return hi, mid, lo


def _dot(a, b):
    return jnp.dot(a.astype(BF16), b.astype(BF16), preferred_element_type=F32)


def _dot_nt(a, b):
    return lax.dot_general(a.astype(BF16), b.astype(BF16), (((1,), (1,)), ((), ())),
                           preferred_element_type=F32)


def _dot_tn(a, b):
    return lax.dot_general(a.astype(BF16), b.astype(BF16), (((0,), (0,)), ((), ())),
                           preferred_element_type=F32)


def _dot_sel(p_bf16, x):
    hi, mid, lo = _split3(x)
    d = lambda y: jnp.dot(p_bf16, y, preferred_element_type=F32)
    return d(hi) + d(mid) + d(lo)


def _row_tile(total, target, multiple=8):
    best = None
    for t in range(multiple, min(total, target) + 1, multiple):
        if total % t == 0:
            best = t
    assert best is not None, (total, target, multiple)
    return best


def _mod_norm(x, g, mod_ref, b, row0, ctx_len, nb, shift_idx, scale_idx):
    d = x.shape[-1]
    ms = jnp.mean(x * x, axis=-1, keepdims=True)
    y = x * lax.rsqrt(ms + NORM_EPS) * g
    rows = row0 + lax.broadcasted_iota(jnp.int32, (x.shape[0], 1), 0)
    is_ctx = rows < ctx_len

    def pick(idx):
        vx = mod_ref[pl.ds(b, 1), idx * d:(idx + 1) * d]
        vc = mod_ref[nb:nb + 1, idx * d:(idx + 1) * d]
        return jnp.where(is_ctx, vc, vx)

    return y * (1.0 + pick(scale_idx)) + pick(shift_idx), is_ctx, pick


def _ada_kernel(c_ref, w_ref, b_ref, o_ref):
    c = c_ref[...]
    s = c * _sigmoid(c)
    o_ref[0] = jnp.dot(s, w_ref[0], preferred_element_type=F32,
                       precision=lax.Precision.HIGHEST) + b_ref[0]


def _ada_vectors(cvec, ada_w, ada_b):
    depth, d, n = ada_w.shape
    tn = _row_tile(n, 1536, LANES)
    return pl.pallas_call(
        _ada_kernel,
        grid=(depth, n // tn),
        in_specs=[pl.BlockSpec((MOD_ROWS, d), lambda l, j: (0, 0)),
                  pl.BlockSpec((1, d, tn), lambda l, j: (l, 0, j)),
                  pl.BlockSpec((1, 1, tn), lambda l, j: (l, 0, j))],
        out_specs=pl.BlockSpec((1, MOD_ROWS, tn), lambda l, j: (l, 0, j)),
        out_shape=jax.ShapeDtypeStruct((depth, MOD_ROWS, n), F32),
        compiler_params=_params(("arbitrary", "arbitrary")),
        name="ada_vectors",
    )(cvec, ada_w, ada_b.reshape(depth, 1, n))


def _inproj_kernel(x_ref, mod_ref, g_ref, w_ref, o_ref, h_ref, *, tm, ctx_len, nb):
    b = pl.program_id(0)
    i = pl.program_id(1)

    @pl.when(pl.program_id(2) == 0)
    def _():
        h, _, _ = _mod_norm(x_ref[0], g_ref[...], mod_ref, b, i * tm, ctx_len, nb, 0, 1)
        h_ref[...] = h.astype(BF16)

    o_ref[0] = jnp.dot(h_ref[...], w_ref[...], preferred_element_type=F32)


def _in_projection(xc, mod, g, w_bf16, tn, ctx_len):
    nb, t, d = xc.shape
    n = w_bf16.shape[1]
    tm = _row_tile(t, 1280)
    return pl.pallas_call(
        functools.partial(_inproj_kernel, tm=tm, ctx_len=ctx_len, nb=nb),
        grid=(nb, t // tm, n // tn),
        in_specs=[pl.BlockSpec((1, tm, d), lambda b, i, j: (b, i, 0)),
                  pl.BlockSpec(mod.shape, lambda b, i, j: (0, 0)),
                  pl.BlockSpec((1, d), lambda b, i, j: (0, 0)),
                  pl.BlockSpec((d, tn), lambda b, i, j: (0, j))],
        out_specs=pl.BlockSpec((1, tm, tn), lambda b, i, j: (b, i, j)),
        out_shape=jax.ShapeDtypeStruct((nb, t, n), F32),
        scratch_shapes=[pltpu.VMEM((tm, d), BF16)],
        compiler_params=_params(("arbitrary", "arbitrary", "arbitrary")),
        name="in_projection",
    )(xc, mod, g.reshape(1, d), w_bf16)


def _outproj_kernel(*refs, n_o, gated, heads, head_dim, tm, ctx_len, nb):
    o_refs = refs[:n_o]
    pos = n_o
    if gated:
        z_ref, ng_ref = refs[pos], refs[pos + 1]
        pos += 2
    w_ref, x_ref, mod_ref, out_ref = refs[pos:pos + 4]
    b = pl.program_id(0)
    i = pl.program_id(1)
    if gated:
        pieces = []
        for h in range(heads):
            sl = slice(h * head_dim, (h + 1) * head_dim)
            o = o_refs[0][0, :, sl]
            for r in o_refs[1:]:
                o = o + r[0, :, sl]
            ms = jnp.mean(o * o, axis=-1, keepdims=True)
            o = o * lax.rsqrt(ms + NORM_EPS) * ng_ref[...]
            z = z_ref[0, :, sl]
            pieces.append((o * (z * _sigmoid(z))).astype(BF16))
        lhs = jnp.concatenate(pieces, axis=-1)
    else:
        lhs = o_refs[0][0]
    y = jnp.dot(lhs, w_ref[...], preferred_element_type=F32)
    d = y.shape[-1]
    rows = i * tm + lax.broadcasted_iota(jnp.int32, (tm, 1), 0)
    gate = jnp.where(rows < ctx_len, mod_ref[nb:nb + 1, 2 * d:3 * d], mod_ref[pl.ds(b, 1), 2 * d:3 * d])
    out_ref[0] = x_ref[0] + gate * y


def _out_projection(o_list, z_src, z_col_block, norm_g, heads, head_dim, w_bf16, xc, mod, ctx_len, tm_target):
    nb, t, d = xc.shape
    dv = w_bf16.shape[0]
    tm = _row_tile(t, tm_target)
    gated = z_src is not None
    row_spec = lambda width, col: pl.BlockSpec((1, tm, width), lambda b, i: (b, i, col))
    in_specs = [row_spec(dv, 0) for _ in o_list]
    args = list(o_list)
    if gated:
        in_specs += [row_spec(dv, z_col_block), pl.BlockSpec((1, head_dim), lambda b, i: (0, 0))]
        args += [z_src, norm_g.reshape(1, head_dim)]
    in_specs += [pl.BlockSpec((dv, d), lambda b, i: (0, 0)), row_spec(d, 0),
                 pl.BlockSpec(mod.shape, lambda b, i: (0, 0))]
    args += [w_bf16, xc, mod]
    return pl.pallas_call(
        functools.partial(_outproj_kernel, n_o=len(o_list), gated=gated, heads=heads, head_dim=head_dim,
                          tm=tm, ctx_len=ctx_len, nb=nb),
        grid=(nb, t // tm),
        in_specs=in_specs,
        out_specs=row_spec(d, 0),
        out_shape=jax.ShapeDtypeStruct((nb, t, d), F32),
        compiler_params=_params(("arbitrary", "arbitrary")),
        name="out_projection",
    )(*args)


def _ffn_kernel(x_ref, mod_ref, g_ref, w1_ref, w2_ref, o_ref, acc_ref, *, tm, fk, ctx_len, nb):
    b = pl.program_id(0)
    i = pl.program_id(1)
    x = x_ref[0]
    h, _, pick = _mod_norm(x, g_ref[...], mod_ref, b, i * tm, ctx_len, nb, 3, 4)
    h = h.astype(BF16)
    d_ff = w1_ref.shape[1]
    for k in range(d_ff // fk):
        u = jnp.dot(h, w1_ref[:, k * fk:(k + 1) * fk], preferred_element_type=F32)
        u = jnp.maximum(u, 0.0)
        u = (u * u).astype(BF16)
        contrib = jnp.dot(u, w2_ref[k * fk:(k + 1) * fk, :], preferred_element_type=F32)
        if k == 0:
            acc_ref[...] = contrib
        else:
            acc_ref[...] += contrib
    o_ref[0] = x + pick(5) * acc_ref[...]


def _ffn(xc, mod, g, w1_bf16, w2_bf16, ctx_len):
    nb, t, d = xc.shape
    d_ff = w1_bf16.shape[1]
    tm = _row_tile(t, 640)
    return pl.pallas_call(
        functools.partial(_ffn_kernel, tm=tm, fk=512, ctx_len=ctx_len, nb=nb),
        grid=(nb, t // tm),
        in_specs=[pl.BlockSpec((1, tm, d), lambda b, i: (b, i, 0)),
                  pl.BlockSpec(mod.shape, lambda b, i: (0, 0)),
                  pl.BlockSpec((1, d), lambda b, i: (0, 0)),
                  pl.BlockSpec((d, d_ff), lambda b, i: (0, 0)),
                  pl.BlockSpec((d_ff, d), lambda b, i: (0, 0))],
        out_specs=pl.BlockSpec((1, tm, d), lambda b, i: (b, i, 0)),
        out_shape=jax.ShapeDtypeStruct((nb, t, d), F32),
        scratch_shapes=[pltpu.VMEM((tm, d), F32)],
        compiler_params=_params(("arbitrary", "arbitrary")),
        name="ffn",
    )(xc, mod, g.reshape(1, d), w1_bf16, w2_bf16)


def _dn_conv_kernel(p_ref, pp_ref, pn_ref, w_ref, o_ref, ext_ref, *, tm, t_total, ctx_len):
    i = pl.program_id(1)
    j = pl.program_id(2)
    ext_ref[0:8, :] = pp_ref[0]
    ext_ref[8:8 + tm, :] = p_ref[0]
    ext_ref[8 + tm:16 + tm, :] = pn_ref[0]
    n_ext = tm + 16
    scale = jnp.where(j == 0, DN_HEAD ** -0.5, 1.0).astype(F32)

    def conv_silu_norm(masked):
        t = i * tm + lax.broadcasted_iota(jnp.int32, (tm, 1), 0)
        seg = jnp.where(t >= ctx_len, 1, 0)
        for h in range(p_ref.shape[2] // DN_HEAD):
            cols = slice(h * DN_HEAD, (h + 1) * DN_HEAD)
            x_ext = ext_ref[:, cols]
            acc = None
            for tap in range(SHORT_CONV):
                d = tap - SHORT_CONV // 2
                xs = (x_ext if d == 0 else pltpu.roll(x_ext, (-d) % n_ext, 0))[8:8 + tm]
                if masked and d != 0:
                    td = t + d
                    ok = jnp.where(td >= 0, 1, 0) * jnp.where(td < t_total, 1, 0) * jnp.where(
                        jnp.where(td >= ctx_len, 1, 0) == seg, 1, 0)
                    xs = jnp.where(ok > 0, xs, 0.0)
                term = xs * w_ref[tap:tap + 1, cols]
                acc = term if acc is None else acc + term
            y = acc * (0.5 * (1.0 + jnp.tanh(0.5 * acc)))
            ss = jnp.sum(y * y, axis=-1, keepdims=True)
            o_ref[0, :, cols] = y * jnp.where(j < 2, lax.rsqrt(ss + NORM_EPS) * scale, 1.0)

    row0 = i * tm
    pad = SHORT_CONV // 2
    edge = ((row0 == 0) | (row0 + tm == t_total)
            | ((row0 - pad < ctx_len) & (row0 + tm + pad > ctx_len)))

    @pl.when(edge)
    def _():
        conv_silu_norm(True)

    @pl.when(jnp.logical_not(edge))
    def _():
        conv_silu_norm(False)


def _dn_conv(p, conv_w_t, ctx_len):
    nb, t, _ = p.shape
    tc = 1024
    tm = _row_tile(t, 640)
    hb = tm // 8
    last = t // 8 - 1
    return pl.pallas_call(
        functools.partial(_dn_conv_kernel, tm=tm, t_total=t, ctx_len=ctx_len),
        grid=(nb, t // tm, DN_QKV_DIM // tc),
        in_specs=[pl.BlockSpec((1, tm, tc), lambda b, i, j: (b, i, j)),
                  pl.BlockSpec((1, 8, tc), lambda b, i, j: (b, jnp.maximum(i * hb - 1, 0), j)),
                  pl.BlockSpec((1, 8, tc), lambda b, i, j: (b, jnp.minimum((i + 1) * hb, last), j)),
                  pl.BlockSpec((8, tc), lambda b, i, j: (0, j))],
        out_specs=pl.BlockSpec((1, tm, tc), lambda b, i, j: (b, i, j)),
        out_shape=jax.ShapeDtypeStruct((nb, t, DN_QKV_DIM), F32),
        scratch_shapes=[pltpu.VMEM((tm + 16, tc), F32)],
        compiler_params=_params(("arbitrary", "arbitrary", "arbitrary")),
        name="dn_conv",
    )(p, p, p, conv_w_t)


def _dn_gate_kernel(ab_ref, par_ref, o_ref, *, tm):
    x = ab_ref[0]
    g = -jnp.exp(par_ref[0:1, :]) * _softplus(x + par_ref[1:2, :])
    beta = _sigmoid(x)
    lane = lax.broadcasted_iota(jnp.int32, (1, LANES), 1)
    r = lax.broadcasted_iota(jnp.int32, (CHUNK, CHUNK), 0)
    c = lax.broadcasted_iota(jnp.int32, (CHUNK, CHUNK), 1)
    lower = jnp.where(c <= r, 1.0, 0.0).astype(BF16)
    upper = jnp.where(c >= r, 1.0, 0.0).astype(BF16)
    nh = DN_V_HEADS
    for k in range(tm // CHUNK):
        rows = slice(k * CHUNK, (k + 1) * CHUNK)
        gk = g[rows]
        fwd = _dot_sel(lower, gk)
        bwd = _dot_sel(upper, gk)
        o_ref[0, rows, :] = jnp.where(lane < nh, fwd,
                                      jnp.where(lane < 2 * nh, bwd,
                                                jnp.where(lane < 4 * nh, beta[rows], 0.0)))


def _dn_gates(p, ab_col_block, a_log, dt_bias):
    nb, t, _ = p.shape
    tm = _row_tile(t, 1280, CHUNK)
    par = jnp.zeros((8, LANES), F32)
    par = par.at[0, :2 * DN_V_HEADS].set(a_log.reshape(-1).astype(F32))
    par = par.at[1, :2 * DN_V_HEADS].set(dt_bias.reshape(-1).astype(F32))
    return pl.pallas_call(
        functools.partial(_dn_gate_kernel, tm=tm),
        grid=(nb, t // tm),
        in_specs=[pl.BlockSpec((1, tm, LANES), lambda b, i: (b, i, ab_col_block)),
                  pl.BlockSpec((8, LANES), lambda b, i: (0, 0))],
        out_specs=pl.BlockSpec((1, tm, LANES), lambda b, i: (b, i, 0)),
        out_shape=jax.ShapeDtypeStruct((nb, t, LANES), F32),
        compiler_params=_params(("arbitrary", "arbitrary")),
        name="dn_gates",
    )(p, par)


def _bdot(a, b):
    return jnp.einsum("nij,njk->nik", a.astype(BF16), b.astype(BF16), preferred_element_type=F32)


def _bdot_nt(a, b):
    return jnp.einsum("nid,njd->nij", a.astype(BF16), b.astype(BF16), preferred_element_type=F32)


def _bdot_tn(a, b):
    return jnp.einsum("nci,ncj->nij", a.astype(BF16), b.astype(BF16), preferred_element_type=F32)


N_LEVELS = int(math.log2(CHUNK))


def _unit_triangular_inverse(parts, eye):
    coupling = lambda lv: jnp.concatenate([jnp.where(masks[lv], m, 0.0) for m, masks in parts], axis=0)
    x = eye - coupling(0)
    for lv in range(1, N_LEVELS):
        x = x - _bdot(x, _bdot(coupling(lv), x))
    return x


def _coupling_masks(ri, ci, reverse):
    ti, tj = (CHUNK - 1 - ri, CHUNK - 1 - ci) if reverse else (ri, ci)
    masks = []
    for lv in range(N_LEVELS):
        bi, bj = lax.shift_right_logical(ti, lv), lax.shift_right_logical(tj, lv)
        masks.append(jnp.where((bi & 1) == 1, bi - 1, -1) == bj)
    return masks


def _dn_scan_kernel(qf_ref, kf_ref, vf_ref, gf_ref, gtf_ref,
                    qr_ref, kr_ref, vr_ref, gr_ref, gtr_ref,
                    of_ref, ob_ref, s_ref):
    @pl.when(pl.program_id(2) == 0)
    def _():
        s_ref[...] = jnp.zeros_like(s_ref)

    ri = lax.broadcasted_iota(jnp.int32, (CHUNK, CHUNK), 0)
    ci = lax.broadcasted_iota(jnp.int32, (CHUNK, CHUNK), 1)
    eye = jnp.where(ri == ci, 1.0, 0.0)
    nc = SCAN_BLOCK // CHUNK
    chunk_rows = [slice(c * CHUNK, (c + 1) * CHUNK) for c in range(nc)]
    dirs = ((qf_ref, kf_ref, vf_ref, gf_ref, gtf_ref, False),
            (qr_ref, kr_ref, vr_ref, gr_ref, gtr_ref, True))
    m_parts, rhs_l, a_l, qg_l, kt_l, egl_l = [], [], [], [], [], []
    head_cols = lambda n: slice(n * DN_HEAD, (n + 1) * DN_HEAD)
    for d, (q_ref, k_ref, v_ref, g_ref, gt_ref, reverse) in enumerate(dirs):
        incl = (ci >= ri) if reverse else (ci <= ri)
        strict = (ci > ri) if reverse else (ci < ri)
        last = 0 if reverse else CHUNK - 1
        m_l = []
        for hh in range(DN_KH_STEP):
            q = jnp.stack([q_ref[0, r, head_cols(hh)] for r in chunk_rows])
            k = jnp.stack([k_ref[0, r, head_cols(hh)] for r in chunk_rows])
            kk = _bdot_nt(k, k)
            qk = _bdot_nt(q, k)
            for j in range(DN_GROUP):
                col = 2 * d + j
                gc = jnp.stack([g_ref[0, hh, r, col:col + 1] for r in chunk_rows])
                bc = jnp.stack([g_ref[0, hh, r, 4 + col:5 + col] for r in chunk_rows])
                gr = jnp.stack([gt_ref[0, hh, c, col:col + 1, :] for c in range(nc)])
                v = jnp.stack([v_ref[0, r, head_cols(hh * DN_GROUP + j)] for r in chunk_rows])
                decay = jnp.where(incl, jnp.exp(jnp.where(incl, gc - gr, 0.0)), 0.0)
                eg = jnp.exp(gc)
                gl = gc[:, last:last + 1, :]
                m_l.append(jnp.where(strict, bc * kk * decay, 0.0))
                rhs_l.append(jnp.concatenate([k * (bc * eg), v * bc], axis=-1))
                a_l.append(qk * decay)
                qg_l.append(q * eg)
                kt_l.append(k * jnp.exp(gl - gc))
                egl_l.append(jnp.exp(gl))
        m_parts.append((jnp.concatenate(m_l, axis=0), _coupling_masks(ri, ci, reverse)))
    cat = lambda xs: jnp.concatenate(xs, axis=0)
    a, kt = cat(a_l), cat(kt_l)
    wu = _bdot(_unit_triangular_inverse(m_parts, eye), cat(rhs_l))
    kb = _bdot_tn(kt, wu)
    qo = _bdot(a, wu)
    qeff = cat(qg_l) - qo[:, :, :DN_HEAD]
    egl = cat(egl_l)
    state = s_ref[...]
    o_refs = (of_ref, ob_ref)
    vh_step = DN_KH_STEP * DN_GROUP
    for step in range(nc):
        chunk_of = [step if d == 0 else nc - 1 - step for d in range(2) for _ in range(vh_step)]
        idx = [ch * nc + c for ch, c in enumerate(chunk_of)]
        pick = lambda x: jnp.stack([x[n] for n in idx])
        o = _bdot(pick(qeff), state) + pick(qo)[:, :, DN_HEAD:]
        kb_s = pick(kb)
        state = pick(egl) * state + kb_s[:, :, DN_HEAD:] - _bdot(kb_s[:, :, :DN_HEAD], state)
        for ch, c in enumerate(chunk_of):
            o_refs[ch // vh_step][0, chunk_rows[c], head_cols(ch % vh_step)] = o[ch]
    s_ref[...] = state


def _dn_scan(qkv, g_cols, g_rows):
    nb, t, _ = qkv.shape
    n_blocks = t // SCAN_BLOCK
    n_chunks = SCAN_BLOCK // CHUNK
    ks = DN_KH_STEP
    groups = DN_K_HEADS // ks
    fwd = lambda s: s
    bwd = lambda s: jnp.where(s == 0, 0, n_blocks - s)

    def specs(order):
        return [pl.BlockSpec((1, SCAN_BLOCK, ks * DN_HEAD), lambda b, h, s: (b, order(s), h)),
                pl.BlockSpec((1, SCAN_BLOCK, ks * DN_HEAD), lambda b, h, s: (b, order(s), groups + h)),
                pl.BlockSpec((1, SCAN_BLOCK, ks * DN_GROUP * DN_HEAD), lambda b, h, s: (b, order(s), groups + h)),
                pl.BlockSpec((1, ks, SCAN_BLOCK, 8), lambda b, h, s: (b, h, order(s), 0)),
                pl.BlockSpec((1, ks, n_chunks, 8, CHUNK), lambda b, h, s: (b, h, order(s), 0, 0))]

    out_spec = lambda order: pl.BlockSpec((1, SCAN_BLOCK, ks * DN_GROUP * DN_HEAD), lambda b, h, s: (b, order(s), h))
    o_shape = jax.ShapeDtypeStruct((nb, t, DN_V_DIM), F32)
    return pl.pallas_call(
        _dn_scan_kernel,
        grid=(nb, groups, n_blocks),
        in_specs=specs(fwd) + specs(bwd),
        out_specs=[out_spec(fwd), out_spec(bwd)],
        out_shape=[o_shape, o_shape],
        scratch_shapes=[pltpu.VMEM((2 * ks * DN_GROUP, DN_HEAD, DN_HEAD), F32)],
        compiler_params=_params(("arbitrary", "arbitrary", "arbitrary")),
        name="dn_scan",
    )(qkv, qkv, qkv, g_cols, g_rows, qkv, qkv, qkv, g_cols, g_rows)


_GLA_LEVELS = (32, 16, 8, 4, 2, 1)


def _gla_tables(reverse):
    idx = np.arange(CHUNK)
    tau = (CHUNK - 1 - idx) if reverse else idx
    ti, tk = tau[:, None], tau[None, :]
    groups = [tk <= ti, tk > ti]
    masks = []
    for s in _GLA_LEVELS:
        bi, bk = ti // s, tk // s
        groups.append((bk == bi) & (tk <= ti) & (tk > bi * s))
        groups.append(((bk == bi) & (tk > ti)) | (tk == (bi + 1) * s))
        masks.append((bi % 2 == 1) & (bk == bi - 1))
    masks.append(ti == tk)
    sel = np.concatenate(groups, axis=0).astype(np.float32)
    return jnp.asarray(sel, BF16), jnp.asarray(np.stack(masks).astype(np.float32))


def _gla_scan_kernel(qf_ref, kf_ref, vf_ref, lf_ref, wf_ref, bf_ref, self_ref, mf_ref,
                     qr_ref, kr_ref, vr_ref, lr_ref, wr_ref, br_ref, selr_ref, mr_ref,
                     of_ref, ob_ref, s_ref):
    @pl.when(pl.program_id(2) == 0)
    def _():
        s_ref[...] = jnp.zeros_like(s_ref)

    nc = SCAN_BLOCK // CHUNK
    n_lev = len(_GLA_LEVELS)
    chunk_rows = [slice(c * CHUNK, (c + 1) * CHUNK) for c in range(nc)]
    dirs = ((qf_ref, kf_ref, vf_ref, lf_ref, wf_ref, bf_ref, self_ref, mf_ref, False),
            (qr_ref, kr_ref, vr_ref, lr_ref, wr_ref, br_ref, selr_ref, mr_ref, True))
    ql, kl, qg_l, kt_l, e_l, v_l = [], [], [], [], [], []
    for q_ref, k_ref, v_ref, l_ref, w_ref, b_ref, sel_ref, m_ref, reverse in dirs:
        last = 0 if reverse else CHUNK - 1
        logits = _dot(l_ref[0], w_ref[...]) + b_ref[...]
        gk_all = (jnp.minimum(logits, 0.0) - jnp.log(1.0 + jnp.exp(-jnp.abs(logits)))) * (1.0 / GLA_GATE_NORMALIZER)
        for rows in chunk_rows:
            q = q_ref[0, rows, :] * (GLA_HEAD_K ** -0.5)
            k = k_ref[0, rows, :]
            gk = gk_all[rows]
            hi = gk.astype(BF16)
            mid = (gk - hi.astype(F32)).astype(BF16)
            both = jnp.dot(sel_ref[...], jnp.concatenate([hi, mid], axis=-1), preferred_element_type=F32)
            sums = both[:, :GLA_HEAD_K] + both[:, GLA_HEAD_K:]
            part = lambda n: sums[n * CHUNK:(n + 1) * CHUNK]
            bcum, tail = part(0), part(1)
            ql += [q * jnp.exp(part(2 + 2 * lv)) for lv in range(n_lev)] + [q]
            kl += [k * jnp.exp(part(3 + 2 * lv)) for lv in range(n_lev)] + [k]
            qg_l.append(q * jnp.exp(bcum))
            kt_l.append(k * jnp.exp(tail))
            e_l.append(jnp.exp(bcum[last:last + 1, :]))
            v_l.append(v_ref[0, rows, :])
    scores = _bdot_nt(jnp.stack(ql), jnp.stack(kl))
    a_l = []
    for n in range(2 * nc):
        m_ref = dirs[n // nc][7]
        a = m_ref[0] * scores[n * (n_lev + 1)]
        for lv in range(1, n_lev + 1):
            a = a + m_ref[lv] * scores[n * (n_lev + 1) + lv]
        a_l.append(a)
    v = jnp.stack(v_l)
    x = _bdot_tn(v, jnp.stack(kt_l))
    s_l = [None] * (2 * nc)
    for d in range(2):
        state = s_ref[d]
        for c in (range(nc - 1, -1, -1) if dirs[d][8] else range(nc)):
            n = d * nc + c
            s_l[n] = state
            state = state * e_l[n] + x[n]
        s_ref[d] = state
    o = _bdot(jnp.stack(a_l), v) + _bdot_nt(jnp.stack(qg_l), jnp.stack(s_l))
    for n in range(2 * nc):
        (of_ref, ob_ref)[n // nc][0, chunk_rows[n % nc], :] = o[n]


def _gla_scan(p, w2cat, b2cat):
    nb, t, _ = p.shape
    n_blocks = t // SCAN_BLOCK
    h = GLA_HEADS
    low_block = (2 * GLA_K_DIM + 2 * GLA_V_DIM) // LANES
    fwd = lambda s: s
    bwd = lambda s: jnp.where(s == 0, 0, n_blocks - s)
    n_sel = (2 + 2 * len(_GLA_LEVELS)) * CHUNK
    n_mask = len(_GLA_LEVELS) + 1

    def specs(order, d):
        return [pl.BlockSpec((1, SCAN_BLOCK, GLA_HEAD_K), lambda b, hh, s: (b, order(s), hh)),
                pl.BlockSpec((1, SCAN_BLOCK, GLA_HEAD_K), lambda b, hh, s: (b, order(s), h + hh)),
                pl.BlockSpec((1, SCAN_BLOCK, GLA_HEAD_V), lambda b, hh, s: (b, order(s), h + hh)),
                pl.BlockSpec((1, SCAN_BLOCK, LANES), lambda b, hh, s: (b, order(s), low_block)),
                pl.BlockSpec((LANES, GLA_HEAD_K), lambda b, hh, s: (0, d * h + hh)),
                pl.BlockSpec((1, GLA_HEAD_K), lambda b, hh, s: (0, d * h + hh)),
                pl.BlockSpec((n_sel, CHUNK), lambda b, hh, s: (0, 0)),
                pl.BlockSpec((n_mask, CHUNK, CHUNK), lambda b, hh, s: (0, 0, 0))]

    out_spec = lambda order: pl.BlockSpec((1, SCAN_BLOCK, GLA_HEAD_V), lambda b, hh, s: (b, order(s), hh))
    o_shape = jax.ShapeDtypeStruct((nb, t, GLA_V_DIM), F32)
    sel_f, mask_f = _gla_tables(False)
    sel_r, mask_r = _gla_tables(True)
    return pl.pallas_call(
        _gla_scan_kernel,
        grid=(nb, h, n_blocks),
        in_specs=specs(fwd, 0) + specs(bwd, 1),
        out_specs=[out_spec(fwd), out_spec(bwd)],
        out_shape=[o_shape, o_shape],
        scratch_shapes=[pltpu.VMEM((2, GLA_HEAD_V, GLA_HEAD_K), F32)],
        compiler_params=_params(("arbitrary", "arbitrary", "arbitrary")),
        name="gla_scan",
    )(p, p, p, p, w2cat, b2cat, sel_f, mask_f, p, p, p, p, w2cat, b2cat, sel_r, mask_r)


def _attn_prep_kernel(p_ref, qg_ref, kg_ref, cos_ref, sin_ref, q_ref, k_ref, v_ref):
    cos = cos_ref[...]
    sin = sin_ref[...]
    lane = lax.broadcasted_iota(jnp.int32, (1, ATTN_HEAD), 1)
    first = (lane % (ATTN_HEAD // 2)) < (ATTN_HEAD // 4)
    q_scale = ATTN_HEAD ** -0.5 * math.log2(math.e)

    def norm_rope(x, g):
        ms = jnp.mean(x * x, axis=-1, keepdims=True)
        y = x * lax.rsqrt(ms + NORM_EPS) * g
        partner = jnp.where(first, pltpu.roll(y, ATTN_HEAD - ATTN_HEAD // 4, 1), pltpu.roll(y, ATTN_HEAD // 4, 1))
        return y * cos + partner * sin

    qd = ATTN_Q_HEADS * ATTN_HEAD
    kd = ATTN_KV_HEADS * ATTN_HEAD
    q_pieces = [norm_rope(p_ref[0, :, h * ATTN_HEAD:(h + 1) * ATTN_HEAD], qg_ref[...]) * q_scale
                for h in range(ATTN_Q_HEADS)]
    q_ref[0] = jnp.concatenate(q_pieces, axis=-1).astype(BF16)
    k_pieces = [norm_rope(p_ref[0, :, qd + h * ATTN_HEAD:qd + (h + 1) * ATTN_HEAD], kg_ref[...])
                for h in range(ATTN_KV_HEADS)]
    k_ref[0] = jnp.concatenate(k_pieces, axis=-1).astype(BF16)
    v_ref[0] = p_ref[0, :, qd + kd:qd + 2 * kd].astype(BF16)


def _rope_tables(t, ctx_len):
    pos = jnp.arange(t - ctx_len)
    row = (pos // GRID_W).astype(F32)
    col = (pos % GRID_W).astype(F32)
    axis_dim = ATTN_HEAD // 2
    inv_freq = jnp.power(ROPE_THETA, -jnp.arange(0, axis_dim, 2, dtype=F32) / axis_dim)
    ar, ac = row[:, None] * inv_freq, col[:, None] * inv_freq
    cos = jnp.concatenate([jnp.cos(ar), jnp.cos(ar), jnp.cos(ac), jnp.cos(ac)], axis=-1)
    sin = jnp.concatenate([-jnp.sin(ar), jnp.sin(ar), -jnp.sin(ac), jnp.sin(ac)], axis=-1)
    cos = jnp.concatenate([jnp.ones((ctx_len, ATTN_HEAD), F32), cos], axis=0)
    sin = jnp.concatenate([jnp.zeros((ctx_len, ATTN_HEAD), F32), sin], axis=0)
    return cos, sin


def _attn_prep(p, q_g, k_g, cos, sin):
    nb, t, n = p.shape
    tm = _row_tile(t, 640)
    qd = ATTN_Q_HEADS * ATTN_HEAD
    kd = ATTN_KV_HEADS * ATTN_HEAD
    row = lambda w: pl.BlockSpec((1, tm, w), lambda b, i: (b, i, 0))
    tab = pl.BlockSpec((tm, ATTN_HEAD), lambda b, i: (i, 0))
    vec = pl.BlockSpec((1, ATTN_HEAD), lambda b, i: (0, 0))
    return pl.pallas_call(
        _attn_prep_kernel,
        grid=(nb, t // tm),
        in_specs=[row(n), vec, vec, tab, tab],
        out_specs=[row(qd), row(kd), row(kd)],
        out_shape=[jax.ShapeDtypeStruct((nb, t, qd), BF16),
                   jax.ShapeDtypeStruct((nb, t, kd), BF16),
                   jax.ShapeDtypeStruct((nb, t, kd), BF16)],
        compiler_params=_params(("arbitrary", "arbitrary")),
        name="attn_prep",
    )(p, q_g.reshape(1, -1), k_g.reshape(1, -1), cos, sin)


def _flash_kernel(q_ref, k_ref, v_ref, o_ref, qs_ref, m_ref, l_ref, acc_ref, sa_ref, sb_ref, *, tq, tk, ctx_len, t_total):
    i = pl.program_id(2)
    g, hd = ATTN_GROUP, ATTN_HEAD
    for h in range(g):
        qs_ref[h * tq:(h + 1) * tq, :] = q_ref[0, :, h * hd:(h + 1) * hd]
    m_ref[...] = jnp.full_like(m_ref, -jnp.inf)
    l_ref[...] = jnp.zeros_like(l_ref)
    acc_ref[...] = jnp.zeros_like(acc_ref)

    def scores(start, width):
        kc = k_ref[0, pl.ds(start, width), :]
        return lax.dot_general(qs_ref[...], kc, (((1,), (1,)), ((), ())), preferred_element_type=F32)

    def update(s, start, width):
        vc = v_ref[0, pl.ds(start, width), :]
        m_old = m_ref[...]
        m_new = jnp.maximum(m_old, jnp.max(s, axis=-1, keepdims=True))
        alpha = jnp.exp2(m_old - m_new)
        p = jnp.exp2(s - jnp.tile(m_new, (1, width // LANES)))
        psum = p[:, 0:LANES]
        for n in range(1, width // LANES):
            psum = psum + p[:, n * LANES:(n + 1) * LANES]
        l_ref[...] = alpha * l_ref[...] + psum
        acc_ref[...] = alpha * acc_ref[...] + jnp.dot(p.astype(BF16), vc, preferred_element_type=F32)
        m_ref[...] = m_new

    @pl.when(i * tq < ctx_len)
    def _():
        update(scores(0, ctx_len), 0, ctx_len)

    n_kv = t_total // tk

    @pl.when(i * tq >= ctx_len)
    def _():
        sa_ref[...] = scores(0, tk)

        def pair(c):
            first = pl.multiple_of(2 * c * tk, tk)
            second = pl.multiple_of(first + tk, tk)
            third = pl.multiple_of(jnp.minimum(2 * c + 2, n_kv - 1) * tk, tk)
            sb_ref[...] = scores(second, tk)
            update(sa_ref[...], first, tk)
            sa_ref[...] = scores(third, tk)
            update(sb_ref[...], second, tk)

        n_pairs = n_kv // 2
        unroll = 4 if n_pairs % 4 == 1 else 2

        def body(c, carry):
            for u in range(unroll):
                pair(unroll * c + u)
            return carry
        lax.fori_loop(0, n_pairs // unroll, body, 0)
        for c in range(n_pairs - n_pairs % unroll, n_pairs):
            pair(c)

    out = acc_ref[...] / jnp.sum(l_ref[...], axis=-1, keepdims=True)
    o_ref[0] = jnp.concatenate([out[h * tq:(h + 1) * tq] for h in range(g)], axis=-1).astype(BF16)


def _flash_attention(q, k, v, ctx_len):
    nb, t, qd = q.shape
    tq = 256
    tk = _row_tile(t, 640, LANES)
    assert ctx_len == tq and t % tq == 0 and ctx_len % LANES == 0 and (t // tk) % 2 == 0
    gw = ATTN_GROUP * ATTN_HEAD
    return pl.pallas_call(
        functools.partial(_flash_kernel, tq=tq, tk=tk, ctx_len=ctx_len, t_total=t),
        grid=(nb, ATTN_KV_HEADS, t // tq),
        in_specs=[pl.BlockSpec((1, tq, gw), lambda b, kv, i: (b, i, kv)),
                  pl.BlockSpec((1, t, ATTN_HEAD), lambda b, kv, i: (b, 0, kv)),
                  pl.BlockSpec((1, t, ATTN_HEAD), lambda b, kv, i: (b, 0, kv))],
        out_specs=pl.BlockSpec((1, tq, gw), lambda b, kv, i: (b, i, kv)),
        out_shape=jax.ShapeDtypeStruct((nb, t, qd), BF16),
        scratch_shapes=[pltpu.VMEM((ATTN_GROUP * tq, ATTN_HEAD), BF16),
                        pltpu.VMEM((ATTN_GROUP * tq, LANES), F32),
                        pltpu.VMEM((ATTN_GROUP * tq, LANES), F32),
                        pltpu.VMEM((ATTN_GROUP * tq, ATTN_HEAD), F32),
                        pltpu.VMEM((ATTN_GROUP * tq, tk), F32),
                        pltpu.VMEM((ATTN_GROUP * tq, tk), F32)],
        compiler_params=_params(("arbitrary", "arbitrary", "arbitrary")),
        name="flash_attention",
    )(q, k, v)


def _pad_cols(w, n):
    return jnp.pad(w, ((0, 0), (0, n - w.shape[1])))


def _deltanet_layer(xc, mod, norm_g, w_in, conv_w, a_log, dt_bias, out_norm_g, w_out, ctx_len):
    nb, t, _ = xc.shape
    n_pad = 7 * 896
    assert n_pad >= w_in.shape[1]
    p = _in_projection(xc, mod, norm_g, _pad_cols(w_in, n_pad).astype(BF16), 896, ctx_len)
    conv_w_t = jnp.pad(conv_w.T.astype(F32), ((0, 8 - SHORT_CONV), (0, 0)))
    qkv = _dn_conv(p, conv_w_t, ctx_len)
    gb = _dn_gates(p, (DN_QKV_DIM + DN_V_DIM) // LANES, a_log, dt_bias)
    g6 = gb[..., :4 * DN_V_HEADS].reshape(nb, t, 2, 2, DN_K_HEADS, DN_GROUP)
    g_cols = g6.transpose(0, 4, 1, 2, 3, 5).reshape(nb, DN_K_HEADS, t, 4 * DN_GROUP)
    g_rows = g_cols.reshape(nb, DN_K_HEADS, t // CHUNK, CHUNK, 4 * DN_GROUP).transpose(0, 1, 2, 4, 3)
    o_f, o_b = _dn_scan(qkv, g_cols, g_rows)
    return _out_projection([o_f, o_b], p, DN_QKV_DIM // DN_V_DIM, out_norm_g, DN_V_HEADS, DN_HEAD,
                           w_out.astype(BF16), xc, mod, ctx_len, 320)


def _gla_layer(xc, mod, norm_g, w_in, gate_w2, gate_b2, out_norm_g, w_out, ctx_len):
    n_pad = 5 * 640
    assert n_pad >= w_in.shape[1]
    p = _in_projection(xc, mod, norm_g, _pad_cols(w_in, n_pad).astype(BF16), 640, ctx_len)
    r = GLA_GATE_RANK
    w2cat = jnp.zeros((LANES, 2 * GLA_K_DIM), F32)
    w2cat = w2cat.at[0:r, :GLA_K_DIM].set(gate_w2[0]).at[r:2 * r, GLA_K_DIM:].set(gate_w2[1]).astype(BF16)
    b2cat = gate_b2.reshape(1, 2 * GLA_K_DIM).astype(F32)
    o_f, o_b = _gla_scan(p, w2cat, b2cat)
    z_block = (2 * GLA_K_DIM + GLA_V_DIM) // GLA_V_DIM
    return _out_projection([o_f, o_b], p, z_block, out_norm_g, GLA_HEADS, GLA_HEAD_V,
                           w_out.astype(BF16), xc, mod, ctx_len, 640)


def _attention_layer(xc, mod, norm_g, w_in, q_g, k_g, w_out, rope, ctx_len):
    p = _in_projection(xc, mod, norm_g, w_in.astype(BF16), 512, ctx_len)
    q, k, v = _attn_prep(p, q_g, k_g, *rope)
    o = _flash_attention(q, k, v, ctx_len)
    return _out_projection([o], None, 0, None, 0, 0, w_out.astype(BF16), xc, mod, ctx_len, 1280)


def kernel(x, c, ctx, c_ctx, ada_w, ada_b, norm_mix_g, norm_ffn_g, ffn_w1, ffn_w2, dn_w_in, dn_conv_w, dn_a_log, dn_dt_bias, dn_norm_g, dn_w_out, gla_w_in, gla_gate_w2, gla_gate_b2, gla_norm_g, gla_w_out, attn_w_in, attn_q_norm_g, attn_k_norm_g, attn_w_out):
    nb, seq, d = x.shape
    ctx_len = ctx.shape[1]
    depth = ada_w.shape[0]
    assert ctx_len == SCAN_BLOCK and seq % SCAN_BLOCK == 0 and nb < MOD_ROWS
    t = ctx_len + seq
    xc = jnp.concatenate([ctx, x], axis=1)
    cvec = jnp.zeros((MOD_ROWS, d), F32).at[:nb].set(c).at[nb].set(c_ctx)
    mods = _ada_vectors(cvec, ada_w, ada_b)
    rope = _rope_tables(t, ctx_len)
    for i in range(depth):
        mix, slot = i % 3, i // 3
        mod = mods[i]
        if mix == 0:
            xc = _deltanet_layer(xc, mod, norm_mix_g[i], dn_w_in[slot], dn_conv_w[slot], dn_a_log[slot],
                                 dn_dt_bias[slot], dn_norm_g[slot], dn_w_out[slot], ctx_len)
        elif mix == 1:
            xc = _gla_layer(xc, mod, norm_mix_g[i], gla_w_in[slot], gla_gate_w2[slot], gla_gate_b2[slot],
                            gla_norm_g[slot], gla_w_out[slot], ctx_len)
        else:
            xc = _attention_layer(xc, mod, norm_mix_g[i], attn_w_in[slot], attn_q_norm_g[slot],
                                  attn_k_norm_g[slot], attn_w_out[slot], rope, ctx_len)
        xc = _ffn(xc, mod, norm_ffn_g[i], ffn_w1[i].astype(BF16), ffn_w2[i].astype(BF16), ctx_len)
    return xc[:, ctx_len:, :]
```

```python
import functools
import math

import numpy as np
import jax
import jax.numpy as jnp
from jax import lax
from jax.experimental import pallas as pl
from jax.experimental.pallas import tpu as pltpu

F32 = jnp.float32
BF16 = jnp.bfloat16

NORM_EPS = 1e-6
GRID_W = 64
ROPE_THETA = 10000.0
SHORT_CONV = 5

DN_K_HEADS = 8
DN_V_HEADS = 16
DN_HEAD = 128
DN_GROUP = DN_V_HEADS // DN_K_HEADS
DN_K_DIM = DN_K_HEADS * DN_HEAD
DN_V_DIM = DN_V_HEADS * DN_HEAD
DN_QKV_DIM = 2 * DN_K_DIM + DN_V_DIM
DN_KH_STEP = 2

GLA_HEADS = 4
GLA_HEAD_K = 128
GLA_HEAD_V = 256
GLA_K_DIM = GLA_HEADS * GLA_HEAD_K
GLA_V_DIM = GLA_HEADS * GLA_HEAD_V
GLA_GATE_RANK = 16
GLA_GATE_NORMALIZER = 16.0
GLA_H_STEP = 2

ATTN_Q_HEADS = 8
ATTN_KV_HEADS = 2
ATTN_HEAD = 128
ATTN_GROUP = ATTN_Q_HEADS // ATTN_KV_HEADS

CHUNK = 64
SCAN_BLOCK = 256
LANES = 128
MXU_CHUNK = 1024
MOD_ROWS = 8
VMEM_LIMIT = 56 * 1024 * 1024


def _params(semantics, vmem=VMEM_LIMIT):
    return pltpu.CompilerParams(dimension_semantics=semantics, vmem_limit_bytes=vmem)


def _sigmoid(x):
    return 1.0 / (1.0 + jnp.exp(-x))


def _softplus(x):
    return jnp.maximum(x, 0.0) + jnp.log(1.0 + jnp.exp(-jnp.abs(x)))


def _split3(x):
    hi = x.astype(BF16)
    r1 = x - hi.astype(F32)
    mid = r1.astype(BF16)
    lo = (r1 - mid.astype(F32)).astype(BF16)
    return hi, mid, lo


def _dot(a, b):
    return jnp.dot(a.astype(BF16), b.astype(BF16), preferred_element_type=F32)


def _dot_nt(a, b):
    return lax.dot_general(a.astype(BF16), b.astype(BF16), (((1,), (1,)), ((), ())),
                           preferred_element_type=F32)


def _dot_tn(a, b):
    return lax.dot_general(a.astype(BF16), b.astype(BF16), (((0,), (0,)), ((), ())),
                           preferred_element_type=F32)


def _dot_sel(p_bf16, x):
    hi, mid, lo = _split3(x)
    d = lambda y: jnp.dot(p_bf16, y, preferred_element_type=F32)
    return d(hi) + d(mid) + d(lo)


def _row_tile(total, target, multiple=8):
    best = None
    for t in range(multiple, min(total, target) + 1, multiple):
        if total % t == 0:
            best = t
    assert best is not None, (total, target, multiple)
    return best


def _mod_norm(x, g, mod_ref, b, row0, ctx_len, nb, shift_idx, scale_idx):
    d = x.shape[-1]
    ms = jnp.mean(x * x, axis=-1, keepdims=True)
    y = x * lax.rsqrt(ms + NORM_EPS) * g
    rows = row0 + lax.broadcasted_iota(jnp.int32, (x.shape[0], 1), 0)
    is_ctx = rows < ctx_len

    def pick(idx):
        vx = mod_ref[pl.ds(b, 1), idx * d:(idx + 1) * d]
        vc = mod_ref[nb:nb + 1, idx * d:(idx + 1) * d]
        return jnp.where(is_ctx, vc, vx)

    return y * (1.0 + pick(scale_idx)) + pick(shift_idx), is_ctx, pick


def _ada_kernel(c_ref, w_ref, b_ref, o_ref):
    c = c_ref[...]
    s = c * _sigmoid(c)
    o_ref[0] = jnp.dot(s, w_ref[0], preferred_element_type=F32,
                       precision=lax.Precision.HIGHEST) + b_ref[0]


def _ada_vectors(cvec, ada_w, ada_b):
    depth, d, n = ada_w.shape
    tn = _row_tile(n, 1536, LANES)
    return pl.pallas_call(
        _ada_kernel,
        grid=(depth, n // tn),
        in_specs=[pl.BlockSpec((MOD_ROWS, d), lambda l, j: (0, 0)),
                  pl.BlockSpec((1, d, tn), lambda l, j: (l, 0, j)),
                  pl.BlockSpec((1, 1, tn), lambda l, j: (l, 0, j))],
        out_specs=pl.BlockSpec((1, MOD_ROWS, tn), lambda l, j: (l, 0, j)),
        out_shape=jax.ShapeDtypeStruct((depth, MOD_ROWS, n), F32),
        compiler_params=_params(("arbitrary", "arbitrary")),
        name="ada_vectors",
    )(cvec, ada_w, ada_b.reshape(depth, 1, n))


def _inproj_kernel(x_ref, mod_ref, g_ref, w_ref, o_ref, *, tm, tn, ctx_len, nb):
    b = pl.program_id(0)
    i = pl.program_id(1)
    h, _, _ = _mod_norm(x_ref[0], g_ref[...], mod_ref, b, i * tm, ctx_len, nb, 0, 1)
    h = h.astype(BF16)
    n = w_ref.shape[1]
    for start in range(0, n, tn):
        cols = slice(start, min(start + tn, n))
        o_ref[0, :, cols] = jnp.dot(h, w_ref[:, cols], preferred_element_type=F32)


def _in_projection(xc, mod, g, w_bf16, tn, ctx_len):
    nb, t, d = xc.shape
    n = w_bf16.shape[1]
    tm = _row_tile(t, 640)
    return pl.pallas_call(
        functools.partial(_inproj_kernel, tm=tm, tn=tn, ctx_len=ctx_len, nb=nb),
        grid=(nb, t // tm),
        in_specs=[pl.BlockSpec((1, tm, d), lambda b, i: (b, i, 0)),
                  pl.BlockSpec(mod.shape, lambda b, i: (0, 0)),
                  pl.BlockSpec((1, d), lambda b, i: (0, 0)),
                  pl.BlockSpec((d, n), lambda b, i: (0, 0), pipeline_mode=pl.Buffered(1))],
        out_specs=pl.BlockSpec((1, tm, n), lambda b, i: (b, i, 0)),
        out_shape=jax.ShapeDtypeStruct((nb, t, n), F32),
        compiler_params=_params(("arbitrary", "arbitrary")),
        name="in_projection",
    )(xc, mod, g.reshape(1, d), w_bf16)


def _outproj_kernel(*refs, n_o, gated, heads, head_dim, tm, ctx_len, nb):
    o_refs = refs[:n_o]
    pos = n_o
    if gated:
        z_ref, ng_ref = refs[pos], refs[pos + 1]
        pos += 2
    w_ref, x_ref, mod_ref, out_ref = refs[pos:pos + 4]
    b = pl.program_id(0)
    i = pl.program_id(1)
    if gated:
        pieces = []
        for h in range(heads):
            sl = slice(h * head_dim, (h + 1) * head_dim)
            o = o_refs[0][0, :, sl]
            for r in o_refs[1:]:
                o = o + r[0, :, sl]
            ms = jnp.mean(o * o, axis=-1, keepdims=True)
            o = o * lax.rsqrt(ms + NORM_EPS) * ng_ref[...]
            z = z_ref[0, :, sl]
            pieces.append((o * (z * _sigmoid(z))).astype(BF16))
        lhs = jnp.concatenate(pieces, axis=-1)
    else:
        lhs = o_refs[0][0]
    y = jnp.dot(lhs, w_ref[...], preferred_element_type=F32)
    d = y.shape[-1]
    rows = i * tm + lax.broadcasted_iota(jnp.int32, (tm, 1), 0)
    gate = jnp.where(rows < ctx_len, mod_ref[nb:nb + 1, 2 * d:3 * d], mod_ref[pl.ds(b, 1), 2 * d:3 * d])
    out_ref[0] = x_ref[0] + gate * y


def _out_projection(o_list, z_src, z_col_block, norm_g, heads, head_dim, w_bf16, xc, mod, ctx_len, tm_target):
    nb, t, d = xc.shape
    dv = w_bf16.shape[0]
    tm = _row_tile(t, tm_target)
    gated = z_src is not None
    row_spec = lambda width, col: pl.BlockSpec((1, tm, width), lambda b, i: (b, i, col))
    in_specs = [row_spec(dv, 0) for _ in o_list]
    args = list(o_list)
    if gated:
        in_specs += [row_spec(dv, z_col_block), pl.BlockSpec((1, head_dim), lambda b, i: (0, 0))]
        args += [z_src, norm_g.reshape(1, head_dim)]
    in_specs += [pl.BlockSpec((dv, d), lambda b, i: (0, 0)), row_spec(d, 0),
                 pl.BlockSpec(mod.shape, lambda b, i: (0, 0))]
    args += [w_bf16, xc, mod]
    return pl.pallas_call(
        functools.partial(_outproj_kernel, n_o=len(o_list), gated=gated, heads=heads, head_dim=head_dim,
                          tm=tm, ctx_len=ctx_len, nb=nb),
        grid=(nb, t // tm),
        in_specs=in_specs,
        out_specs=row_spec(d, 0),
        out_shape=jax.ShapeDtypeStruct((nb, t, d), F32),
        compiler_params=_params(("arbitrary", "arbitrary")),
        name="out_projection",
    )(*args)


def _ffn_kernel(x_ref, mod_ref, g_ref, w1_ref, w2_ref, o_ref, acc_ref, *, tm, fk, ctx_len, nb):
    b = pl.program_id(0)
    i = pl.program_id(1)
    x = x_ref[0]
    h, _, pick = _mod_norm(x, g_ref[...], mod_ref, b, i * tm, ctx_len, nb, 3, 4)
    h = h.astype(BF16)
    d_ff = w1_ref.shape[1]
    for k in range(d_ff // fk):
        u = jnp.dot(h, w1_ref[:, k * fk:(k + 1) * fk], preferred_element_type=F32)
        u = jnp.maximum(u, 0.0)
        u = (u * u).astype(BF16)
        contrib = jnp.dot(u, w2_ref[k * fk:(k + 1) * fk, :], preferred_element_type=F32)
        if k == 0:
            acc_ref[...] = contrib
        else:
            acc_ref[...] += contrib
    o_ref[0] = x + pick(5) * acc_ref[...]


def _ffn(xc, mod, g, w1_bf16, w2_bf16, ctx_len):
    nb, t, d = xc.shape
    d_ff = w1_bf16.shape[1]
    tm = _row_tile(t, 640)
    return pl.pallas_call(
        functools.partial(_ffn_kernel, tm=tm, fk=512, ctx_len=ctx_len, nb=nb),
        grid=(nb, t // tm),
        in_specs=[pl.BlockSpec((1, tm, d), lambda b, i: (b, i, 0)),
                  pl.BlockSpec(mod.shape, lambda b, i: (0, 0)),
                  pl.BlockSpec((1, d), lambda b, i: (0, 0)),
                  pl.BlockSpec((d, d_ff), lambda b, i: (0, 0)),
                  pl.BlockSpec((d_ff, d), lambda b, i: (0, 0))],
        out_specs=pl.BlockSpec((1, tm, d), lambda b, i: (b, i, 0)),
        out_shape=jax.ShapeDtypeStruct((nb, t, d), F32),
        scratch_shapes=[pltpu.VMEM((tm, d), F32)],
        compiler_params=_params(("arbitrary", "arbitrary")),
        name="ffn",
    )(xc, mod, g.reshape(1, d), w1_bf16, w2_bf16)


def _dn_conv_kernel(p_ref, pp_ref, pn_ref, w_ref, o_ref, ext_ref, *, tm, t_total, ctx_len):
    i = pl.program_id(1)
    j = pl.program_id(2)
    ext_ref[0:8, :] = pp_ref[0]
    ext_ref[8:8 + tm, :] = p_ref[0]
    ext_ref[8 + tm:16 + tm, :] = pn_ref[0]
    n_ext = tm + 16
    scale = jnp.where(j == 0, DN_HEAD ** -0.5, 1.0).astype(F32)

    def conv_silu_norm(masked):
        t = i * tm + lax.broadcasted_iota(jnp.int32, (tm, 1), 0)
        seg = jnp.where(t >= ctx_len, 1, 0)
        for h in range(p_ref.shape[2] // DN_HEAD):
            cols = slice(h * DN_HEAD, (h + 1) * DN_HEAD)
            x_ext = ext_ref[:, cols]
            acc = None
            for tap in range(SHORT_CONV):
                d = tap - SHORT_CONV // 2
                xs = (x_ext if d == 0 else pltpu.roll(x_ext, (-d) % n_ext, 0))[8:8 + tm]
                if masked and d != 0:
                    td = t + d
                    ok = jnp.where(td >= 0, 1, 0) * jnp.where(td < t_total, 1, 0) * jnp.where(
                        jnp.where(td >= ctx_len, 1, 0) == seg, 1, 0)
                    xs = jnp.where(ok > 0, xs, 0.0)
                term = xs * w_ref[tap:tap + 1, cols]
                acc = term if acc is None else acc + term
            y = acc * (0.5 * (1.0 + jnp.tanh(0.5 * acc)))
            ss = jnp.sum(y * y, axis=-1, keepdims=True)
            o_ref[0, :, cols] = y * jnp.where(j < 2, lax.rsqrt(ss + NORM_EPS) * scale, 1.0)

    row0 = i * tm
    pad = SHORT_CONV // 2
    edge = ((row0 == 0) | (row0 + tm == t_total)
            | ((row0 - pad < ctx_len) & (row0 + tm + pad > ctx_len)))

    @pl.when(edge)
    def _():
        conv_silu_norm(True)

    @pl.when(jnp.logical_not(edge))
    def _():
        conv_silu_norm(False)


def _dn_conv(p, conv_w_t, ctx_len):
    nb, t, _ = p.shape
    tc = 1024
    tm = _row_tile(t, 640)
    hb = tm // 8
    last = t // 8 - 1
    return pl.pallas_call(
        functools.partial(_dn_conv_kernel, tm=tm, t_total=t, ctx_len=ctx_len),
        grid=(nb, t // tm, DN_QKV_DIM // tc),
        in_specs=[pl.BlockSpec((1, tm, tc), lambda b, i, j: (b, i, j)),
                  pl.BlockSpec((1, 8, tc), lambda b, i, j: (b, jnp.maximum(i * hb - 1, 0), j)),
                  pl.BlockSpec((1, 8, tc), lambda b, i, j: (b, jnp.minimum((i + 1) * hb, last), j)),
                  pl.BlockSpec((8, tc), lambda b, i, j: (0, j))],
        out_specs=pl.BlockSpec((1, tm, tc), lambda b, i, j: (b, i, j)),
        out_shape=jax.ShapeDtypeStruct((nb, t, DN_QKV_DIM), F32),
        scratch_shapes=[pltpu.VMEM((tm + 16, tc), F32)],
        compiler_params=_params(("arbitrary", "arbitrary", "arbitrary")),
        name="dn_conv",
    )(p, p, p, conv_w_t)


def _dn_gate_kernel(ab_ref, par_ref, o_ref, *, tm):
    x = ab_ref[0]
    g = -jnp.exp(par_ref[0:1, :]) * _softplus(x + par_ref[1:2, :])
    beta = _sigmoid(x)
    lane = lax.broadcasted_iota(jnp.int32, (1, LANES), 1)
    used = lane < 4 * DN_V_HEADS
    is_beta = (lane & (2 * DN_GROUP)) != 0
    is_reverse = (lane & DN_GROUP) != 0
    r = lax.broadcasted_iota(jnp.int32, (CHUNK, CHUNK), 0)
    c = lax.broadcasted_iota(jnp.int32, (CHUNK, CHUNK), 1)
    lower = jnp.where(c <= r, 1.0, 0.0).astype(BF16)
    upper = jnp.where(c >= r, 1.0, 0.0).astype(BF16)
    for k in range(tm // CHUNK):
        rows = slice(k * CHUNK, (k + 1) * CHUNK)
        gk = g[rows]
        fwd = _dot_sel(lower, gk)
        bwd = _dot_sel(upper, gk)
        o_ref[0, rows, :] = jnp.where(used, jnp.where(is_beta, beta[rows], jnp.where(is_reverse, bwd, fwd)), 0.0)


def _dn_gate_lane_perm():
    perm = np.zeros(4 * DN_V_HEADS, np.int32)
    for kh in range(DN_K_HEADS):
        for kind in range(2):
            for d in range(2):
                for j in range(DN_GROUP):
                    lane = ((kh * 2 + kind) * 2 + d) * DN_GROUP + j
                    perm[lane] = (kind * 2 + d) * DN_V_HEADS + kh * DN_GROUP + j
    return perm


def _dn_gates(p, ab_col_block, a_log, dt_bias):
    nb, t, _ = p.shape
    tm = _row_tile(t, 1280, CHUNK)
    perm = _dn_gate_lane_perm()
    on_lanes = lambda v: jnp.concatenate([v.reshape(-1).astype(F32), jnp.zeros(2 * DN_V_HEADS, F32)])[perm]
    par = jnp.zeros((8, LANES), F32)
    par = par.at[0, :4 * DN_V_HEADS].set(on_lanes(a_log))
    par = par.at[1, :4 * DN_V_HEADS].set(on_lanes(dt_bias))
    return pl.pallas_call(
        functools.partial(_dn_gate_kernel, tm=tm),
        grid=(nb, t // tm),
        in_specs=[pl.BlockSpec((1, tm, LANES), lambda b, i: (b, i, ab_col_block)),
                  pl.BlockSpec((8, LANES), lambda b, i: (0, 0))],
        out_specs=pl.BlockSpec((1, tm, LANES), lambda b, i: (b, i, 0)),
        out_shape=jax.ShapeDtypeStruct((nb, t, LANES), F32),
        compiler_params=_params(("arbitrary", "arbitrary")),
        name="dn_gates",
    )(p, par)


def _bdot(a, b):
    return jnp.einsum("nij,njk->nik", a.astype(BF16), b.astype(BF16), preferred_element_type=F32)


def _bdot_nt(a, b):
    return jnp.einsum("nid,njd->nij", a.astype(BF16), b.astype(BF16), preferred_element_type=F32)


def _bdot_tn(a, b):
    return jnp.einsum("nci,ncj->nij", a.astype(BF16), b.astype(BF16), preferred_element_type=F32)


N_LEVELS = int(math.log2(CHUNK))


def _unit_triangular_inverse(parts, eye):
    coupling = lambda lv: jnp.concatenate([jnp.where(masks[lv], m, 0.0) for m, masks in parts], axis=0)
    x = eye - coupling(0)
    for lv in range(1, N_LEVELS):
        x = x - _bdot(x, _bdot(coupling(lv), x))
    return x


def _coupling_masks(ri, ci, reverse):
    ti, tj = (CHUNK - 1 - ri, CHUNK - 1 - ci) if reverse else (ri, ci)
    masks = []
    for lv in range(N_LEVELS):
        bi, bj = lax.shift_right_logical(ti, lv), lax.shift_right_logical(tj, lv)
        masks.append(jnp.where((bi & 1) == 1, bi - 1, -1) == bj)
    return masks


def _dn_scan_kernel(qf_ref, kf_ref, vf_ref, gf_ref, gtf_ref,
                    qr_ref, kr_ref, vr_ref, gr_ref, gtr_ref,
                    of_ref, ob_ref, s_ref):
    @pl.when(pl.program_id(2) == 0)
    def _():
        s_ref[...] = jnp.zeros_like(s_ref)

    ri = lax.broadcasted_iota(jnp.int32, (CHUNK, CHUNK), 0)
    ci = lax.broadcasted_iota(jnp.int32, (CHUNK, CHUNK), 1)
    eye = jnp.where(ri == ci, 1.0, 0.0)
    nc = SCAN_BLOCK // CHUNK
    chunk_rows = [slice(c * CHUNK, (c + 1) * CHUNK) for c in range(nc)]
    dirs = ((qf_ref, kf_ref, vf_ref, gf_ref, gtf_ref, False),
            (qr_ref, kr_ref, vr_ref, gr_ref, gtr_ref, True))
    m_parts, rhs_l, a_l, qg_l, kt_l, egl_l = [], [], [], [], [], []
    head_cols = lambda n: slice(n * DN_HEAD, (n + 1) * DN_HEAD)
    for d, (q_ref, k_ref, v_ref, g_ref, gt_ref, reverse) in enumerate(dirs):
        incl = (ci >= ri) if reverse else (ci <= ri)
        strict = (ci > ri) if reverse else (ci < ri)
        last = 0 if reverse else CHUNK - 1
        m_l = []
        per_head = 4 * DN_GROUP
        first_lane = pl.program_id(1) * (DN_KH_STEP * per_head)
        g_all = pltpu.roll(g_ref[0], jnp.where(first_lane == 0, 0, LANES - first_lane), 1)
        for hh in range(DN_KH_STEP):
            q = jnp.stack([q_ref[0, r, head_cols(hh)] for r in chunk_rows])
            k = jnp.stack([k_ref[0, r, head_cols(hh)] for r in chunk_rows])
            kk = _bdot_nt(k, k)
            qk = _bdot_nt(q, k)
            for j in range(DN_GROUP):
                col = 2 * d + j
                lane = hh * per_head + col
                gc = jnp.stack([g_all[r, lane:lane + 1] for r in chunk_rows])
                bc = jnp.stack([g_all[r, lane + 2 * DN_GROUP:lane + 2 * DN_GROUP + 1] for r in chunk_rows])
                gr = jnp.stack([gt_ref[0, hh, c, col:col + 1, :] for c in range(nc)])
                v = jnp.stack([v_ref[0, r, head_cols(hh * DN_GROUP + j)] for r in chunk_rows])
                decay = jnp.where(incl, jnp.exp(jnp.where(incl, gc - gr, 0.0)), 0.0)
                eg = jnp.exp(gc)
                gl = gc[:, last:last + 1, :]
                m_l.append(jnp.where(strict, bc * kk * decay, 0.0))
                rhs_l.append(jnp.concatenate([k * (bc * eg), v * bc], axis=-1))
                a_l.append(qk * decay)
                qg_l.append(q * eg)
                kt_l.append(k * jnp.exp(gl - gc))
                egl_l.append(jnp.exp(gl))
        m_parts.append((jnp.concatenate(m_l, axis=0), _coupling_masks(ri, ci, reverse)))
    cat = lambda xs: jnp.concatenate(xs, axis=0)
    a, kt = cat(a_l), cat(kt_l)
    wu = _bdot(_unit_triangular_inverse(m_parts, eye), cat(rhs_l))
    kb = _bdot_tn(kt, wu)
    qo = _bdot(a, wu)
    qeff = cat(qg_l) - qo[:, :, :DN_HEAD]
    egl = cat(egl_l)
    state = s_ref[...]
    o_refs = (of_ref, ob_ref)
    vh_step = DN_KH_STEP * DN_GROUP
    for step in range(nc):
        chunk_of = [step if d == 0 else nc - 1 - step for d in range(2) for _ in range(vh_step)]
        idx = [ch * nc + c for ch, c in enumerate(chunk_of)]
        pick = lambda x: jnp.stack([x[n] for n in idx])
        o = _bdot(pick(qeff), state) + pick(qo)[:, :, DN_HEAD:]
        kb_s = pick(kb)
        state = pick(egl) * state + kb_s[:, :, DN_HEAD:] - _bdot(kb_s[:, :, :DN_HEAD], state)
        for ch, c in enumerate(chunk_of):
            o_refs[ch // vh_step][0, chunk_rows[c], head_cols(ch % vh_step)] = o[ch]
    s_ref[...] = state


def _dn_scan(qkv, gates, g_rows):
    nb, t, _ = qkv.shape
    n_blocks = t // SCAN_BLOCK
    n_chunks = SCAN_BLOCK // CHUNK
    ks = DN_KH_STEP
    groups = DN_K_HEADS // ks
    fwd = lambda s: s
    bwd = lambda s: jnp.where(s == 0, 0, n_blocks - s)

    def specs(order):
        return [pl.BlockSpec((1, SCAN_BLOCK, ks * DN_HEAD), lambda b, h, s: (b, order(s), h)),
                pl.BlockSpec((1, SCAN_BLOCK, ks * DN_HEAD), lambda b, h, s: (b, order(s), groups + h)),
                pl.BlockSpec((1, SCAN_BLOCK, ks * DN_GROUP * DN_HEAD), lambda b, h, s: (b, order(s), groups + h)),
                pl.BlockSpec((1, SCAN_BLOCK, LANES), lambda b, h, s: (b, order(s), 0)),
                pl.BlockSpec((1, ks, n_chunks, 8, CHUNK), lambda b, h, s: (b, h, order(s), 0, 0))]

    out_spec = lambda order: pl.BlockSpec((1, SCAN_BLOCK, ks * DN_GROUP * DN_HEAD), lambda b, h, s: (b, order(s), h))
    o_shape = jax.ShapeDtypeStruct((nb, t, DN_V_DIM), F32)
    return pl.pallas_call(
        _dn_scan_kernel,
        grid=(nb, groups, n_blocks),
        in_specs=specs(fwd) + specs(bwd),
        out_specs=[out_spec(fwd), out_spec(bwd)],
        out_shape=[o_shape, o_shape],
        scratch_shapes=[pltpu.VMEM((2 * ks * DN_GROUP, DN_HEAD, DN_HEAD), F32)],
        compiler_params=_params(("arbitrary", "arbitrary", "arbitrary")),
        name="dn_scan",
    )(qkv, qkv, qkv, gates, g_rows, qkv, qkv, qkv, gates, g_rows)


_GLA_LEVELS = (32, 16, 8, 4, 2, 1)


def _gla_tables(reverse):
    idx = np.arange(CHUNK)
    tau = (CHUNK - 1 - idx) if reverse else idx
    ti, tk = tau[:, None], tau[None, :]
    groups = [tk <= ti, tk > ti]
    masks = []
    for s in _GLA_LEVELS:
        bi, bk = ti // s, tk // s
        groups.append((bk == bi) & (tk <= ti) & (tk > bi * s))
        groups.append(((bk == bi) & (tk > ti)) | (tk == (bi + 1) * s))
        masks.append((bi % 2 == 1) & (bk == bi - 1))
    masks.append(ti == tk)
    sel = np.concatenate(groups, axis=0).astype(np.float32)
    return jnp.asarray(sel, BF16), jnp.asarray(np.stack(masks).astype(np.float32))


def _gla_scan_kernel(qf_ref, kf_ref, vf_ref, lf_ref, wf_ref, bf_ref, self_ref, mf_ref,
                     qr_ref, kr_ref, vr_ref, lr_ref, wr_ref, br_ref, selr_ref, mr_ref,
                     of_ref, ob_ref, s_ref):
    @pl.when(pl.program_id(2) == 0)
    def _():
        s_ref[...] = jnp.zeros_like(s_ref)

    nc = SCAN_BLOCK // CHUNK
    n_lev = len(_GLA_LEVELS)
    chunk_rows = [slice(c * CHUNK, (c + 1) * CHUNK) for c in range(nc)]
    dirs = ((qf_ref, kf_ref, vf_ref, lf_ref, wf_ref, bf_ref, self_ref, mf_ref, False),
            (qr_ref, kr_ref, vr_ref, lr_ref, wr_ref, br_ref, selr_ref, mr_ref, True))
    hs = GLA_H_STEP
    kcols = lambda hh: slice(hh * GLA_HEAD_K, (hh + 1) * GLA_HEAD_K)
    vcols = lambda hh: slice(hh * GLA_HEAD_V, (hh + 1) * GLA_HEAD_V)
    ql, kl, qg_l, kt_l, e_l, v_l = [], [], [], [], [], []
    for q_ref, k_ref, v_ref, l_ref, w_ref, b_ref, sel_ref, m_ref, reverse in dirs:
        last = 0 if reverse else CHUNK - 1
        logits = _dot(l_ref[0], w_ref[...]) + b_ref[...]
        gk_all = (jnp.minimum(logits, 0.0) - jnp.log(1.0 + jnp.exp(-jnp.abs(logits)))) * (1.0 / GLA_GATE_NORMALIZER)
        for hh in range(hs):
            for rows in chunk_rows:
                q = q_ref[0, rows, kcols(hh)] * (GLA_HEAD_K ** -0.5)
                k = k_ref[0, rows, kcols(hh)]
                gk = gk_all[rows, kcols(hh)]
                hi = gk.astype(BF16)
                mid = (gk - hi.astype(F32)).astype(BF16)
                both = jnp.dot(sel_ref[...], jnp.concatenate([hi, mid], axis=-1), preferred_element_type=F32)
                sums = both[:, :GLA_HEAD_K] + both[:, GLA_HEAD_K:]
                part = lambda n: sums[n * CHUNK:(n + 1) * CHUNK]
                bcum, tail = part(0), part(1)
                ql += [q * jnp.exp(part(2 + 2 * lv)) for lv in range(n_lev)] + [q]
                kl += [k * jnp.exp(part(3 + 2 * lv)) for lv in range(n_lev)] + [k]
                qg_l.append(q * jnp.exp(bcum))
                kt_l.append(k * jnp.exp(tail))
                e_l.append(jnp.exp(bcum[last:last + 1, :]))
                v_l.append(v_ref[0, rows, vcols(hh)])
    scores = _bdot_nt(jnp.stack(ql), jnp.stack(kl))
    a_l = []
    for n in range(2 * hs * nc):
        m_ref = dirs[n // (hs * nc)][7]
        a = m_ref[0] * scores[n * (n_lev + 1)]
        for lv in range(1, n_lev + 1):
            a = a + m_ref[lv] * scores[n * (n_lev + 1) + lv]
        a_l.append(a)
    v = jnp.stack(v_l)
    x = _bdot_tn(v, jnp.stack(kt_l))
    s_l = [None] * (2 * hs * nc)
    for ch in range(2 * hs):
        state = s_ref[ch]
        for c in (range(nc - 1, -1, -1) if dirs[ch // hs][8] else range(nc)):
            n = ch * nc + c
            s_l[n] = state
            state = state * e_l[n] + x[n]
        s_ref[ch] = state
    o = _bdot(jnp.stack(a_l), v) + _bdot_nt(jnp.stack(qg_l), jnp.stack(s_l))
    for n in range(2 * hs * nc):
        ch, c = divmod(n, nc)
        (of_ref, ob_ref)[ch // hs][0, chunk_rows[c], vcols(ch % hs)] = o[n]


def _gla_scan(p, w2cat, b2cat):
    nb, t, _ = p.shape
    n_blocks = t // SCAN_BLOCK
    hs = GLA_H_STEP
    h = GLA_HEADS // hs
    low_block = (2 * GLA_K_DIM + 2 * GLA_V_DIM) // LANES
    fwd = lambda s: s
    bwd = lambda s: jnp.where(s == 0, 0, n_blocks - s)
    n_sel = (2 + 2 * len(_GLA_LEVELS)) * CHUNK
    n_mask = len(_GLA_LEVELS) + 1

    def specs(order, d):
        return [pl.BlockSpec((1, SCAN_BLOCK, hs * GLA_HEAD_K), lambda b, hh, s: (b, order(s), hh)),
                pl.BlockSpec((1, SCAN_BLOCK, hs * GLA_HEAD_K), lambda b, hh, s: (b, order(s), h + hh)),
                pl.BlockSpec((1, SCAN_BLOCK, hs * GLA_HEAD_V), lambda b, hh, s: (b, order(s), h + hh)),
                pl.BlockSpec((1, SCAN_BLOCK, LANES), lambda b, hh, s: (b, order(s), low_block)),
                pl.BlockSpec((LANES, hs * GLA_HEAD_K), lambda b, hh, s: (0, d * h + hh)),
                pl.BlockSpec((1, hs * GLA_HEAD_K), lambda b, hh, s: (0, d * h + hh)),
                pl.BlockSpec((n_sel, CHUNK), lambda b, hh, s: (0, 0)),
                pl.BlockSpec((n_mask, CHUNK, CHUNK), lambda b, hh, s: (0, 0, 0))]

    out_spec = lambda order: pl.BlockSpec((1, SCAN_BLOCK, hs * GLA_HEAD_V), lambda b, hh, s: (b, order(s), hh))
    o_shape = jax.ShapeDtypeStruct((nb, t, GLA_V_DIM), F32)
    sel_f, mask_f = _gla_tables(False)
    sel_r, mask_r = _gla_tables(True)
    return pl.pallas_call(
        _gla_scan_kernel,
        grid=(nb, h, n_blocks),
        in_specs=specs(fwd, 0) + specs(bwd, 1),
        out_specs=[out_spec(fwd), out_spec(bwd)],
        out_shape=[o_shape, o_shape],
        scratch_shapes=[pltpu.VMEM((2 * hs, GLA_HEAD_V, GLA_HEAD_K), F32)],
        compiler_params=_params(("arbitrary", "arbitrary", "arbitrary")),
        name="gla_scan",
    )(p, p, p, p, w2cat, b2cat, sel_f, mask_f, p, p, p, p, w2cat, b2cat, sel_r, mask_r)


def _attn_prep_kernel(p_ref, qg_ref, kg_ref, cos_ref, sin_ref, q_ref, k_ref, v_ref):
    cos = cos_ref[...]
    sin = sin_ref[...]
    lane = lax.broadcasted_iota(jnp.int32, (1, ATTN_HEAD), 1)
    first = (lane % (ATTN_HEAD // 2)) < (ATTN_HEAD // 4)
    q_scale = ATTN_HEAD ** -0.5 * math.log2(math.e)

    def norm_rope(x, g):
        ms = jnp.mean(x * x, axis=-1, keepdims=True)
        y = x * lax.rsqrt(ms + NORM_EPS) * g
        partner = jnp.where(first, pltpu.roll(y, ATTN_HEAD - ATTN_HEAD // 4, 1), pltpu.roll(y, ATTN_HEAD // 4, 1))
        return y * cos + partner * sin

    qd = ATTN_Q_HEADS * ATTN_HEAD
    kd = ATTN_KV_HEADS * ATTN_HEAD
    q_pieces = [norm_rope(p_ref[0, :, h * ATTN_HEAD:(h + 1) * ATTN_HEAD], qg_ref[...]) * q_scale
                for h in range(ATTN_Q_HEADS)]
    q_ref[0] = jnp.concatenate(q_pieces, axis=-1).astype(BF16)
    k_pieces = [norm_rope(p_ref[0, :, qd + h * ATTN_HEAD:qd + (h + 1) * ATTN_HEAD], kg_ref[...])
                for h in range(ATTN_KV_HEADS)]
    k_ref[0] = jnp.concatenate(k_pieces, axis=-1).astype(BF16)
    v_ref[0] = p_ref[0, :, qd + kd:qd + 2 * kd].astype(BF16)


def _rope_tables(t, ctx_len):
    n_rows = (t - ctx_len) // GRID_W
    axis_dim = ATTN_HEAD // 2
    inv_freq = jnp.power(ROPE_THETA, -jnp.arange(0, axis_dim, 2, dtype=F32) / axis_dim)
    ar = jnp.arange(n_rows, dtype=F32)[:, None] * inv_freq
    ac = jnp.arange(GRID_W, dtype=F32)[:, None] * inv_freq
    on_rows = lambda x: jnp.broadcast_to(x[:, None, :], (n_rows, GRID_W, x.shape[-1]))
    on_cols = lambda x: jnp.broadcast_to(x[None, :, :], (n_rows, GRID_W, x.shape[-1]))
    cr, sr, cc, sc = on_rows(jnp.cos(ar)), on_rows(jnp.sin(ar)), on_cols(jnp.cos(ac)), on_cols(jnp.sin(ac))
    cos = jnp.concatenate([cr, cr, cc, cc], axis=-1).reshape(t - ctx_len, ATTN_HEAD)
    sin = jnp.concatenate([-sr, sr, -sc, sc], axis=-1).reshape(t - ctx_len, ATTN_HEAD)
    cos = jnp.concatenate([jnp.ones((ctx_len, ATTN_HEAD), F32), cos], axis=0)
    sin = jnp.concatenate([jnp.zeros((ctx_len, ATTN_HEAD), F32), sin], axis=0)
    return cos, sin


def _attn_prep(p, q_g, k_g, cos, sin):
    nb, t, n = p.shape
    tm = _row_tile(t, 640)
    qd = ATTN_Q_HEADS * ATTN_HEAD
    kd = ATTN_KV_HEADS * ATTN_HEAD
    row = lambda w: pl.BlockSpec((1, tm, w), lambda b, i: (b, i, 0))
    tab = pl.BlockSpec((tm, ATTN_HEAD), lambda b, i: (i, 0))
    vec = pl.BlockSpec((1, ATTN_HEAD), lambda b, i: (0, 0))
    return pl.pallas_call(
        _attn_prep_kernel,
        grid=(nb, t // tm),
        in_specs=[row(n), vec, vec, tab, tab],
        out_specs=[row(qd), row(kd), row(kd)],
        out_shape=[jax.ShapeDtypeStruct((nb, t, qd), BF16),
                   jax.ShapeDtypeStruct((nb, t, kd), BF16),
                   jax.ShapeDtypeStruct((nb, t, kd), BF16)],
        compiler_params=_params(("arbitrary", "arbitrary")),
        name="attn_prep",
    )(p, q_g.reshape(1, -1), k_g.reshape(1, -1), cos, sin)


def _flash_kernel(q_ref, k_ref, v_ref, o_ref, qs_ref, m_ref, l_ref, acc_ref, sa_ref, sb_ref, *, tq, tk, ctx_len, t_total):
    i = pl.program_id(2)
    g, hd = ATTN_GROUP, ATTN_HEAD
    for h in range(g):
        qs_ref[h * tq:(h + 1) * tq, :] = q_ref[0, :, h * hd:(h + 1) * hd]
    m_ref[...] = jnp.full_like(m_ref, -jnp.inf)
    l_ref[...] = jnp.zeros_like(l_ref)
    acc_ref[...] = jnp.zeros_like(acc_ref)

    def scores(start, width):
        kc = k_ref[0, pl.ds(start, width), :]
        return lax.dot_general(qs_ref[...], kc, (((1,), (1,)), ((), ())), preferred_element_type=F32)

    def update(s, start, width):
        vc = v_ref[0, pl.ds(start, width), :]
        m_old = m_ref[...]
        m_new = jnp.maximum(m_old, jnp.max(s, axis=-1, keepdims=True))
        alpha = jnp.exp2(m_old - m_new)
        p = jnp.exp2(s - jnp.tile(m_new, (1, width // LANES)))
        psum = p[:, 0:LANES]
        for n in range(1, width // LANES):
            psum = psum + p[:, n * LANES:(n + 1) * LANES]
        l_ref[...] = alpha * l_ref[...] + psum
        acc_ref[...] = alpha * acc_ref[...] + jnp.dot(p.astype(BF16), vc, preferred_element_type=F32)
        m_ref[...] = m_new

    @pl.when(i * tq < ctx_len)
    def _():
        update(scores(0, ctx_len), 0, ctx_len)

    n_kv = t_total // tk

    @pl.when(i * tq >= ctx_len)
    def _():
        sa_ref[...] = scores(0, tk)

        def pair(c):
            first = pl.multiple_of(2 * c * tk, tk)
            second = pl.multiple_of(first + tk, tk)
            third = pl.multiple_of(jnp.minimum(2 * c + 2, n_kv - 1) * tk, tk)
            sb_ref[...] = scores(second, tk)
            update(sa_ref[...], first, tk)
            sa_ref[...] = scores(third, tk)
            update(sb_ref[...], second, tk)

        n_pairs = n_kv // 2
        unroll = 4 if n_pairs % 4 == 1 else 2

        def body(c, carry):
            for u in range(unroll):
                pair(unroll * c + u)
            return carry
        lax.fori_loop(0, n_pairs // unroll, body, 0)
        for c in range(n_pairs - n_pairs % unroll, n_pairs):
            pair(c)

    out = acc_ref[...] / jnp.sum(l_ref[...], axis=-1, keepdims=True)
    o_ref[0] = jnp.concatenate([out[h * tq:(h + 1) * tq] for h in range(g)], axis=-1).astype(BF16)


def _flash_attention(q, k, v, ctx_len):
    nb, t, qd = q.shape
    tq = 256
    tk = _row_tile(t, 640, LANES)
    assert ctx_len == tq and t % tq == 0 and ctx_len % LANES == 0 and (t // tk) % 2 == 0
    gw = ATTN_GROUP * ATTN_HEAD
    return pl.pallas_call(
        functools.partial(_flash_kernel, tq=tq, tk=tk, ctx_len=ctx_len, t_total=t),
        grid=(nb, ATTN_KV_HEADS, t // tq),
        in_specs=[pl.BlockSpec((1, tq, gw), lambda b, kv, i: (b, i, kv)),
                  pl.BlockSpec((1, t, ATTN_HEAD), lambda b, kv, i: (b, 0, kv)),
                  pl.BlockSpec((1, t, ATTN_HEAD), lambda b, kv, i: (b, 0, kv))],
        out_specs=pl.BlockSpec((1, tq, gw), lambda b, kv, i: (b, i, kv)),
        out_shape=jax.ShapeDtypeStruct((nb, t, qd), BF16),
        scratch_shapes=[pltpu.VMEM((ATTN_GROUP * tq, ATTN_HEAD), BF16),
                        pltpu.VMEM((ATTN_GROUP * tq, LANES), F32),
                        pltpu.VMEM((ATTN_GROUP * tq, LANES), F32),
                        pltpu.VMEM((ATTN_GROUP * tq, ATTN_HEAD), F32),
                        pltpu.VMEM((ATTN_GROUP * tq, tk), F32),
                        pltpu.VMEM((ATTN_GROUP * tq, tk), F32)],
        compiler_params=_params(("arbitrary", "arbitrary", "arbitrary")),
        name="flash_attention",
    )(q, k, v)


def _pad_cols(w, n):
    return jnp.pad(w, ((0, 0), (0, n - w.shape[1])))


def _deltanet_layer(xc, mod, norm_g, w_in, conv_w, a_log, dt_bias, out_norm_g, w_out, ctx_len):
    nb, t, _ = xc.shape
    gate_col = DN_QKV_DIM + DN_V_DIM
    w_gate = _pad_cols(w_in[:, gate_col:][:, _dn_gate_lane_perm()], LANES)
    w = jnp.concatenate([w_in[:, :gate_col], w_gate], axis=1).astype(BF16)
    p = _in_projection(xc, mod, norm_g, w, MXU_CHUNK, ctx_len)
    conv_w_t = jnp.pad(conv_w.T.astype(F32), ((0, 8 - SHORT_CONV), (0, 0)))
    qkv = _dn_conv(p, conv_w_t, ctx_len)
    gb = _dn_gates(p, gate_col // LANES, a_log, dt_bias)
    per_head = 4 * DN_GROUP
    g_rows = gb[..., :DN_K_HEADS * per_head].reshape(nb, t // CHUNK, CHUNK, DN_K_HEADS, per_head)
    g_rows = g_rows.transpose(0, 3, 1, 4, 2)
    o_f, o_b = _dn_scan(qkv, gb, g_rows)
    return _out_projection([o_f, o_b], p, DN_QKV_DIM // DN_V_DIM, out_norm_g, DN_V_HEADS, DN_HEAD,
                           w_out.astype(BF16), xc, mod, ctx_len, 320)


def _gla_layer(xc, mod, norm_g, w_in, gate_w2, gate_b2, out_norm_g, w_out, ctx_len):
    n_pad = 2 * GLA_K_DIM + 2 * GLA_V_DIM + LANES
    p = _in_projection(xc, mod, norm_g, _pad_cols(w_in, n_pad).astype(BF16), MXU_CHUNK, ctx_len)
    r = GLA_GATE_RANK
    w2cat = jnp.zeros((LANES, 2 * GLA_K_DIM), F32)
    w2cat = w2cat.at[0:r, :GLA_K_DIM].set(gate_w2[0]).at[r:2 * r, GLA_K_DIM:].set(gate_w2[1]).astype(BF16)
    b2cat = gate_b2.reshape(1, 2 * GLA_K_DIM).astype(F32)
    o_f, o_b = _gla_scan(p, w2cat, b2cat)
    z_block = (2 * GLA_K_DIM + GLA_V_DIM) // GLA_V_DIM
    return _out_projection([o_f, o_b], p, z_block, out_norm_g, GLA_HEADS, GLA_HEAD_V,
                           w_out.astype(BF16), xc, mod, ctx_len, 640)


def _attention_layer(xc, mod, norm_g, w_in, q_g, k_g, w_out, rope, ctx_len):
    p = _in_projection(xc, mod, norm_g, w_in.astype(BF16), MXU_CHUNK, ctx_len)
    q, k, v = _attn_prep(p, q_g, k_g, *rope)
    o = _flash_attention(q, k, v, ctx_len)
    return _out_projection([o], None, 0, None, 0, 0, w_out.astype(BF16), xc, mod, ctx_len, 1280)


def kernel(x, c, ctx, c_ctx, ada_w, ada_b, norm_mix_g, norm_ffn_g, ffn_w1, ffn_w2, dn_w_in, dn_conv_w, dn_a_log, dn_dt_bias, dn_norm_g, dn_w_out, gla_w_in, gla_gate_w2, gla_gate_b2, gla_norm_g, gla_w_out, attn_w_in, attn_q_norm_g, attn_k_norm_g, attn_w_out):
    nb, seq, d = x.shape
    ctx_len = ctx.shape[1]
    depth = ada_w.shape[0]
    assert ctx_len == SCAN_BLOCK and seq % SCAN_BLOCK == 0 and nb < MOD_ROWS
    t = ctx_len + seq
    xc = jnp.concatenate([ctx, x], axis=1)
    cvec = jnp.zeros((MOD_ROWS, d), F32).at[:nb].set(c).at[nb].set(c_ctx)
    mods = _ada_vectors(cvec, ada_w, ada_b)
    rope = _rope_tables(t, ctx_len)
    for i in range(depth):
        mix, slot = i % 3, i // 3
        mod = mods[i]
        if mix == 0:
            xc = _deltanet_layer(xc, mod, norm_mix_g[i], dn_w_in[slot], dn_conv_w[slot], dn_a_log[slot],
                                 dn_dt_bias[slot], dn_norm_g[slot], dn_w_out[slot], ctx_len)
        elif mix == 1:
            xc = _gla_layer(xc, mod, norm_mix_g[i], gla_w_in[slot], gla_gate_w2[slot], gla_gate_b2[slot],
                            gla_norm_g[slot], gla_w_out[slot], ctx_len)
        else:
            xc = _attention_layer(xc, mod, norm_mix_g[i], attn_w_in[slot], attn_q_norm_g[slot],
                                  attn_k_norm_g[slot], attn_w_out[slot], rope, ctx_len)
        xc = _ffn(xc, mod, norm_ffn_g[i], ffn_w1[i].astype(BF16), ffn_w2[i].astype(BF16), ctx_len)
    return xc[:, ctx_len:, :]
```

```python
import functools
import math

import numpy as np
import jax
import jax.numpy as jnp
from jax import lax
from jax.experimental import pallas as pl
from jax.experimental.pallas import tpu as pltpu

F32 = jnp.float32
BF16 = jnp.bfloat16

NORM_EPS = 1e-6
GRID_W = 64
ROPE_THETA = 10000.0
SHORT_CONV = 5

DN_K_HEADS = 8
DN_V_HEADS = 16
DN_HEAD = 128
DN_GROUP = DN_V_HEADS // DN_K_HEADS
DN_K_DIM = DN_K_HEADS * DN_HEAD
DN_V_DIM = DN_V_HEADS * DN_HEAD
DN_QKV_DIM = 2 * DN_K_DIM + DN_V_DIM
DN_KH_STEP = 2

GLA_HEADS = 4
GLA_HEAD_K = 128
GLA_HEAD_V = 256
GLA_K_DIM = GLA_HEADS * GLA_HEAD_K
GLA_V_DIM = GLA_HEADS * GLA_HEAD_V
GLA_GATE_RANK = 16
GLA_GATE_NORMALIZER = 16.0
GLA_H_STEP = 2

ATTN_Q_HEADS = 8
ATTN_KV_HEADS = 2
ATTN_HEAD = 128
ATTN_GROUP = ATTN_Q_HEADS // ATTN_KV_HEADS

CHUNK = 64
SCAN_BLOCK = 256
LANES = 128
MXU_CHUNK = 1024
MOD_ROWS = 8
VMEM_LIMIT = 56 * 1024 * 1024


def _params(semantics, vmem=VMEM_LIMIT):
    return pltpu.CompilerParams(dimension_semantics=semantics, vmem_limit_bytes=vmem)


def _sigmoid(x):
    return 1.0 / (1.0 + jnp.exp(-x))


def _softplus(x):
    return jnp.maximum(x, 0.0) + jnp.log(1.0 + jnp.exp(-jnp.abs(x)))


def _split3(x):
    hi = x.astype(BF16)
    r1 = x - hi.astype(F32)
    mid = r1.astype(BF16)
    lo = (r1 - mid.astype(F32)).astype(BF16)
    return hi, mid, lo


def _dot(a, b):
    return jnp.dot(a.astype(BF16), b.astype(BF16), preferred_element_type=F32)


def _dot_nt(a, b):
    return lax.dot_general(a.astype(BF16), b.astype(BF16), (((1,), (1,)), ((), ())),
                           preferred_element_type=F32)


def _dot_tn(a, b):
    return lax.dot_general(a.astype(BF16), b.astype(BF16), (((0,), (0,)), ((), ())),
                           preferred_element_type=F32)


def _dot_sel(p_bf16, x):
    hi, mid, lo = _split3(x)
    d = lambda y: jnp.dot(p_bf16, y, preferred_element_type=F32)
    return d(hi) + d(mid) + d(lo)


def _row_tile(total, target, multiple=8):
    best = None
    for t in range(multiple, min(total, target) + 1, multiple):
        if total % t == 0:
            best = t
    assert best is not None, (total, target, multiple)
    return best


def _mod_norm(x, g, mod_ref, b, row0, ctx_len, nb, shift_idx, scale_idx):
    d = x.shape[-1]
    ms = jnp.mean(x * x, axis=-1, keepdims=True)
    y = x * lax.rsqrt(ms + NORM_EPS) * g
    rows = row0 + lax.broadcasted_iota(jnp.int32, (x.shape[0], 1), 0)
    is_ctx = rows < ctx_len

    def pick(idx):
        vx = mod_ref[pl.ds(b, 1), idx * d:(idx + 1) * d]
        vc = mod_ref[nb:nb + 1, idx * d:(idx + 1) * d]
        return jnp.where(is_ctx, vc, vx)

    return y * (1.0 + pick(scale_idx)) + pick(shift_idx), is_ctx, pick


def _ada_kernel(c_ref, w_ref, b_ref, o_ref):
    c = c_ref[...]
    s = c * _sigmoid(c)
    o_ref[0] = jnp.dot(s, w_ref[0], preferred_element_type=F32,
                       precision=lax.Precision.HIGHEST) + b_ref[0]


def _ada_vectors(cvec, ada_w, ada_b):
    depth, d, n = ada_w.shape
    tn = _row_tile(n, 1536, LANES)
    return pl.pallas_call(
        _ada_kernel,
        grid=(depth, n // tn),
        in_specs=[pl.BlockSpec((MOD_ROWS, d), lambda l, j: (0, 0)),
                  pl.BlockSpec((1, d, tn), lambda l, j: (l, 0, j)),
                  pl.BlockSpec((1, 1, tn), lambda l, j: (l, 0, j))],
        out_specs=pl.BlockSpec((1, MOD_ROWS, tn), lambda l, j: (l, 0, j)),
        out_shape=jax.ShapeDtypeStruct((depth, MOD_ROWS, n), F32),
        compiler_params=_params(("arbitrary", "arbitrary")),
        name="ada_vectors",
    )(cvec, ada_w, ada_b.reshape(depth, 1, n))


def _inproj_kernel(x_ref, mod_ref, g_ref, w_ref, o_ref, *, tm, tn, ctx_len, nb):
    b = pl.program_id(0)
    i = pl.program_id(1)
    h, _, _ = _mod_norm(x_ref[0], g_ref[...], mod_ref, b, i * tm, ctx_len, nb, 0, 1)
    h = h.astype(BF16)
    n = w_ref.shape[1]
    for start in range(0, n, tn):
        cols = slice(start, min(start + tn, n))
        o_ref[0, :, cols] = jnp.dot(h, w_ref[:, cols], preferred_element_type=F32)


def _in_projection(xc, mod, g, w_bf16, tn, ctx_len):
    nb, t, d = xc.shape
    n = w_bf16.shape[1]
    tm = _row_tile(t, 640)
    return pl.pallas_call(
        functools.partial(_inproj_kernel, tm=tm, tn=tn, ctx_len=ctx_len, nb=nb),
        grid=(nb, t // tm),
        in_specs=[pl.BlockSpec((1, tm, d), lambda b, i: (b, i, 0)),
                  pl.BlockSpec(mod.shape, lambda b, i: (0, 0)),
                  pl.BlockSpec((1, d), lambda b, i: (0, 0)),
                  pl.BlockSpec((d, n), lambda b, i: (0, 0), pipeline_mode=pl.Buffered(1))],
        out_specs=pl.BlockSpec((1, tm, n), lambda b, i: (b, i, 0)),
        out_shape=jax.ShapeDtypeStruct((nb, t, n), F32),
        compiler_params=_params(("arbitrary", "arbitrary")),
        name="in_projection",
    )(xc, mod, g.reshape(1, d), w_bf16)


def _outproj_kernel(*refs, n_o, gated, heads, head_dim, tm, ctx_len, nb):
    o_refs = refs[:n_o]
    pos = n_o
    if gated:
        z_ref, ng_ref = refs[pos], refs[pos + 1]
        pos += 2
    w_ref, x_ref, mod_ref, out_ref = refs[pos:pos + 4]
    b = pl.program_id(0)
    i = pl.program_id(1)
    if gated:
        pieces = []
        for h in range(heads):
            sl = slice(h * head_dim, (h + 1) * head_dim)
            o = o_refs[0][0, :, sl]
            for r in o_refs[1:]:
                o = o + r[0, :, sl]
            ms = jnp.mean(o * o, axis=-1, keepdims=True)
            o = o * lax.rsqrt(ms + NORM_EPS) * ng_ref[...]
            z = z_ref[0, :, sl]
            pieces.append((o * (z * _sigmoid(z))).astype(BF16))
        lhs = jnp.concatenate(pieces, axis=-1)
    else:
        lhs = o_refs[0][0]
    y = jnp.dot(lhs, w_ref[...], preferred_element_type=F32)
    d = y.shape[-1]
    rows = i * tm + lax.broadcasted_iota(jnp.int32, (tm, 1), 0)
    gate = jnp.where(rows < ctx_len, mod_ref[nb:nb + 1, 2 * d:3 * d], mod_ref[pl.ds(b, 1), 2 * d:3 * d])
    out_ref[0] = x_ref[0] + gate * y


def _out_projection(o_list, z_src, z_col_block, norm_g, heads, head_dim, w_bf16, xc, mod, ctx_len, tm_target):
    nb, t, d = xc.shape
    dv = w_bf16.shape[0]
    tm = _row_tile(t, tm_target)
    gated = z_src is not None
    row_spec = lambda width, col: pl.BlockSpec((1, tm, width), lambda b, i: (b, i, col))
    in_specs = [row_spec(dv, 0) for _ in o_list]
    args = list(o_list)
    if gated:
        in_specs += [row_spec(dv, z_col_block), pl.BlockSpec((1, head_dim), lambda b, i: (0, 0))]
        args += [z_src, norm_g.reshape(1, head_dim)]
    in_specs += [pl.BlockSpec((dv, d), lambda b, i: (0, 0)), row_spec(d, 0),
                 pl.BlockSpec(mod.shape, lambda b, i: (0, 0))]
    args += [w_bf16, xc, mod]
    return pl.pallas_call(
        functools.partial(_outproj_kernel, n_o=len(o_list), gated=gated, heads=heads, head_dim=head_dim,
                          tm=tm, ctx_len=ctx_len, nb=nb),
        grid=(nb, t // tm),
        in_specs=in_specs,
        out_specs=row_spec(d, 0),
        out_shape=jax.ShapeDtypeStruct((nb, t, d), F32),
        compiler_params=_params(("arbitrary", "arbitrary")),
        name="out_projection",
    )(*args)


def _ffn_kernel(x_ref, mod_ref, g_ref, w1_ref, w2_ref, o_ref, acc_ref, *, tm, fk, ctx_len, nb):
    b = pl.program_id(0)
    i = pl.program_id(1)
    x = x_ref[0]
    h, _, pick = _mod_norm(x, g_ref[...], mod_ref, b, i * tm, ctx_len, nb, 3, 4)
    h = h.astype(BF16)
    d_ff = w1_ref.shape[1]
    for k in range(d_ff // fk):
        u = jnp.dot(h, w1_ref[:, k * fk:(k + 1) * fk], preferred_element_type=F32)
        u = jnp.maximum(u, 0.0)
        u = (u * u).astype(BF16)
        contrib = jnp.dot(u, w2_ref[k * fk:(k + 1) * fk, :], preferred_element_type=F32)
        if k == 0:
            acc_ref[...] = contrib
        else:
            acc_ref[...] += contrib
    o_ref[0] = x + pick(5) * acc_ref[...]


def _ffn(xc, mod, g, w1_bf16, w2_bf16, ctx_len):
    nb, t, d = xc.shape
    d_ff = w1_bf16.shape[1]
    tm = _row_tile(t, 640)
    return pl.pallas_call(
        functools.partial(_ffn_kernel, tm=tm, fk=512, ctx_len=ctx_len, nb=nb),
        grid=(nb, t // tm),
        in_specs=[pl.BlockSpec((1, tm, d), lambda b, i: (b, i, 0)),
                  pl.BlockSpec(mod.shape, lambda b, i: (0, 0)),
                  pl.BlockSpec((1, d), lambda b, i: (0, 0)),
                  pl.BlockSpec((d, d_ff), lambda b, i: (0, 0)),
                  pl.BlockSpec((d_ff, d), lambda b, i: (0, 0))],
        out_specs=pl.BlockSpec((1, tm, d), lambda b, i: (b, i, 0)),
        out_shape=jax.ShapeDtypeStruct((nb, t, d), F32),
        scratch_shapes=[pltpu.VMEM((tm, d), F32)],
        compiler_params=_params(("arbitrary", "arbitrary")),
        name="ffn",
    )(xc, mod, g.reshape(1, d), w1_bf16, w2_bf16)


HALO = 8


def _dn_qkv_kernel(x_ref, xp_ref, xn_ref, mod_ref, g_ref, w_ref, cw_ref, o_ref, *,
                   tm, t_total, ctx_len, nb, fix_tile, fix_row):
    b = pl.program_id(0)
    i = pl.program_id(1)
    row0 = i * tm
    n_ext = tm + 2 * HALO
    pad = SHORT_CONV // 2
    x_ext = jnp.concatenate([xp_ref[0], x_ref[0], xn_ref[0]], axis=0)
    h, _, _ = _mod_norm(x_ext, g_ref[...], mod_ref, b, row0 - HALO, ctx_len, nb, 0, 1)
    t = row0 - HALO + lax.broadcasted_iota(jnp.int32, (n_ext, 1), 0)
    nearest = jnp.clip(t, row0, row0 + tm - 1)
    valid = (jnp.where(t >= 0, 1, 0) * jnp.where(t < t_total, 1, 0)
             * jnp.where(jnp.where(t >= ctx_len, 1, 0) == jnp.where(nearest >= ctx_len, 1, 0), 1, 0))
    h = jnp.where(valid > 0, h, 0.0).astype(BF16)
    if fix_row is not None:
        rr = lax.broadcasted_iota(jnp.int32, (2 * HALO, 1), 0)
        crosses = {}
        for d in range(-pad, pad + 1):
            if d:
                inside = jnp.where(rr + d >= 0, 1, 0) * jnp.where(rr + d < 2 * HALO, 1, 0)
                other = jnp.where(jnp.where(rr < HALO, 1, 0) != jnp.where(rr + d < HALO, 1, 0), 1, 0)
                crosses[d] = jnp.where(i == fix_tile, inside * other, 0) > 0
    n = w_ref.shape[1]
    for start in range(0, n, MXU_CHUNK):
        res = jnp.dot(h, w_ref[:, start:start + MXU_CHUNK], preferred_element_type=F32)
        for hd in range(MXU_CHUNK // DN_HEAD):
            col0 = start + hd * DN_HEAD
            cols = slice(col0, col0 + DN_HEAD)
            r = res[:, hd * DN_HEAD:(hd + 1) * DN_HEAD]
            acc = None
            for tap in range(SHORT_CONV):
                d = tap - pad
                xs = (r if d == 0 else pltpu.roll(r, (-d) % n_ext, 0))[HALO:HALO + tm]
                term = xs * cw_ref[tap:tap + 1, cols]
                acc = term if acc is None else acc + term
            if fix_row is not None:
                slab = r[fix_row:fix_row + 2 * HALO]
                wrong = None
                for d, mask in crosses.items():
                    term = jnp.where(mask, pltpu.roll(slab, (-d) % (2 * HALO), 0), 0.0) * cw_ref[d + pad:d + pad + 1, cols]
                    wrong = term if wrong is None else wrong + term
                acc = jnp.concatenate([acc[:fix_row - HALO], acc[fix_row - HALO:fix_row + HALO] - wrong,
                                       acc[fix_row + HALO:]], axis=0)
            half = 0.5 * acc
            y = half + half * jnp.tanh(half)
            if col0 < 2 * DN_K_DIM:
                ss = jnp.sum(y * y, axis=-1, keepdims=True)
                y = y * (lax.rsqrt(ss + NORM_EPS) * (DN_HEAD ** -0.5 if col0 < DN_K_DIM else 1.0))
            o_ref[0, :, cols] = y


def _dn_qkv(xc, mod, g, w_qkv_bf16, conv_w_t, ctx_len):
    nb, t, d = xc.shape
    n = w_qkv_bf16.shape[1]
    tm = _row_tile(t, 640)
    hb = tm // HALO
    last = t // HALO - 1
    fix_tile, fix_row = (ctx_len // tm, ctx_len % tm) if ctx_len % tm else (None, None)
    assert fix_row is None or (fix_row % HALO == 0 and HALO <= fix_row <= tm - HALO)
    return pl.pallas_call(
        functools.partial(_dn_qkv_kernel, tm=tm, t_total=t, ctx_len=ctx_len, nb=nb,
                          fix_tile=fix_tile, fix_row=fix_row),
        grid=(nb, t // tm),
        in_specs=[pl.BlockSpec((1, tm, d), lambda b, i: (b, i, 0)),
                  pl.BlockSpec((1, HALO, d), lambda b, i: (b, jnp.maximum(i * hb - 1, 0), 0)),
                  pl.BlockSpec((1, HALO, d), lambda b, i: (b, jnp.minimum((i + 1) * hb, last), 0)),
                  pl.BlockSpec(mod.shape, lambda b, i: (0, 0)),
                  pl.BlockSpec((1, d), lambda b, i: (0, 0)),
                  pl.BlockSpec((d, n), lambda b, i: (0, 0), pipeline_mode=pl.Buffered(1)),
                  pl.BlockSpec((8, n), lambda b, i: (0, 0))],
        out_specs=pl.BlockSpec((1, tm, n), lambda b, i: (b, i, 0)),
        out_shape=jax.ShapeDtypeStruct((nb, t, n), F32),
        compiler_params=_params(("arbitrary", "arbitrary")),
        name="dn_qkv",
    )(xc, xc, xc, mod, g.reshape(1, d), w_qkv_bf16, conv_w_t)


def _dn_gate_kernel(ab_ref, par_ref, o_ref, *, tm):
    x = ab_ref[0]
    g = -jnp.exp(par_ref[0:1, :]) * _softplus(x + par_ref[1:2, :])
    beta = _sigmoid(x)
    lane = lax.broadcasted_iota(jnp.int32, (1, LANES), 1)
    used = lane < 4 * DN_V_HEADS
    is_beta = (lane & (2 * DN_GROUP)) != 0
    is_reverse = (lane & DN_GROUP) != 0
    r = lax.broadcasted_iota(jnp.int32, (CHUNK, CHUNK), 0)
    c = lax.broadcasted_iota(jnp.int32, (CHUNK, CHUNK), 1)
    lower = jnp.where(c <= r, 1.0, 0.0).astype(BF16)
    upper = jnp.where(c >= r, 1.0, 0.0).astype(BF16)
    for k in range(tm // CHUNK):
        rows = slice(k * CHUNK, (k + 1) * CHUNK)
        gk = g[rows]
        fwd = _dot_sel(lower, gk)
        bwd = _dot_sel(upper, gk)
        o_ref[0, rows, :] = jnp.where(used, jnp.where(is_beta, beta[rows], jnp.where(is_reverse, bwd, fwd)), 0.0)


def _dn_gate_lane_perm():
    perm = np.zeros(4 * DN_V_HEADS, np.int32)
    for kh in range(DN_K_HEADS):
        for kind in range(2):
            for d in range(2):
                for j in range(DN_GROUP):
                    lane = ((kh * 2 + kind) * 2 + d) * DN_GROUP + j
                    perm[lane] = (kind * 2 + d) * DN_V_HEADS + kh * DN_GROUP + j
    return perm


def _dn_gates(p, ab_col_block, a_log, dt_bias):
    nb, t, _ = p.shape
    tm = _row_tile(t, 1280, CHUNK)
    perm = _dn_gate_lane_perm()
    on_lanes = lambda v: jnp.concatenate([v.reshape(-1).astype(F32), jnp.zeros(2 * DN_V_HEADS, F32)])[perm]
    par = jnp.zeros((8, LANES), F32)
    par = par.at[0, :4 * DN_V_HEADS].set(on_lanes(a_log))
    par = par.at[1, :4 * DN_V_HEADS].set(on_lanes(dt_bias))
    return pl.pallas_call(
        functools.partial(_dn_gate_kernel, tm=tm),
        grid=(nb, t // tm),
        in_specs=[pl.BlockSpec((1, tm, LANES), lambda b, i: (b, i, ab_col_block)),
                  pl.BlockSpec((8, LANES), lambda b, i: (0, 0))],
        out_specs=pl.BlockSpec((1, tm, LANES), lambda b, i: (b, i, 0)),
        out_shape=jax.ShapeDtypeStruct((nb, t, LANES), F32),
        compiler_params=_params(("arbitrary", "arbitrary")),
        name="dn_gates",
    )(p, par)


def _bdot(a, b):
    return jnp.einsum("nij,njk->nik", a.astype(BF16), b.astype(BF16), preferred_element_type=F32)


def _bdot_nt(a, b):
    return jnp.einsum("nid,njd->nij", a.astype(BF16), b.astype(BF16), preferred_element_type=F32)


def _bdot_tn(a, b):
    return jnp.einsum("nci,ncj->nij", a.astype(BF16), b.astype(BF16), preferred_element_type=F32)


N_LEVELS = int(math.log2(CHUNK))


def _unit_triangular_inverse(parts, eye):
    coupling = lambda lv: jnp.concatenate([jnp.where(masks[lv], m, 0.0) for m, masks in parts], axis=0)
    x = eye - coupling(0)
    for lv in range(1, N_LEVELS):
        x = x - _bdot(x, _bdot(coupling(lv), x))
    return x


def _coupling_masks(ri, ci, reverse):
    ti, tj = (CHUNK - 1 - ri, CHUNK - 1 - ci) if reverse else (ri, ci)
    masks = []
    for lv in range(N_LEVELS):
        bi, bj = lax.shift_right_logical(ti, lv), lax.shift_right_logical(tj, lv)
        masks.append(jnp.where((bi & 1) == 1, bi - 1, -1) == bj)
    return masks


def _dn_scan_kernel(qf_ref, kf_ref, vf_ref, gf_ref, gtf_ref,
                    qr_ref, kr_ref, vr_ref, gr_ref, gtr_ref,
                    of_ref, ob_ref, s_ref):
    @pl.when(pl.program_id(2) == 0)
    def _():
        s_ref[...] = jnp.zeros_like(s_ref)

    ri = lax.broadcasted_iota(jnp.int32, (CHUNK, CHUNK), 0)
    ci = lax.broadcasted_iota(jnp.int32, (CHUNK, CHUNK), 1)
    eye = jnp.where(ri == ci, 1.0, 0.0)
    nc = SCAN_BLOCK // CHUNK
    chunk_rows = [slice(c * CHUNK, (c + 1) * CHUNK) for c in range(nc)]
    dirs = ((qf_ref, kf_ref, vf_ref, gf_ref, gtf_ref, False),
            (qr_ref, kr_ref, vr_ref, gr_ref, gtr_ref, True))
    m_parts, rhs_l, a_l, qg_l, kt_l, egl_l = [], [], [], [], [], []
    head_cols = lambda n: slice(n * DN_HEAD, (n + 1) * DN_HEAD)
    for d, (q_ref, k_ref, v_ref, g_ref, gt_ref, reverse) in enumerate(dirs):
        incl = (ci >= ri) if reverse else (ci <= ri)
        strict = (ci > ri) if reverse else (ci < ri)
        last = 0 if reverse else CHUNK - 1
        m_l = []
        per_head = 4 * DN_GROUP
        first_lane = pl.program_id(1) * (DN_KH_STEP * per_head)
        g_all = pltpu.roll(g_ref[0], jnp.where(first_lane == 0, 0, LANES - first_lane), 1)
        for hh in range(DN_KH_STEP):
            q = jnp.stack([q_ref[0, r, head_cols(hh)] for r in chunk_rows])
            k = jnp.stack([k_ref[0, r, head_cols(hh)] for r in chunk_rows])
            kk = _bdot_nt(k, k)
            qk = _bdot_nt(q, k)
            for j in range(DN_GROUP):
                col = 2 * d + j
                lane = hh * per_head + col
                gc = jnp.stack([g_all[r, lane:lane + 1] for r in chunk_rows])
                bc = jnp.stack([g_all[r, lane + 2 * DN_GROUP:lane + 2 * DN_GROUP + 1] for r in chunk_rows])
                gr = jnp.stack([gt_ref[0, hh, c, col:col + 1, :] for c in range(nc)])
                v = jnp.stack([v_ref[0, r, head_cols(hh * DN_GROUP + j)] for r in chunk_rows])
                decay = jnp.where(incl, jnp.exp(jnp.where(incl, gc - gr, 0.0)), 0.0)
                eg = jnp.exp(gc)
                gl = gc[:, last:last + 1, :]
                m_l.append(jnp.where(strict, bc * kk * decay, 0.0))
                rhs_l.append(jnp.concatenate([k * (bc * eg), v * bc], axis=-1))
                a_l.append(qk * decay)
                qg_l.append(q * eg)
                kt_l.append(k * jnp.exp(gl - gc))
                egl_l.append(jnp.exp(gl))
        m_parts.append((jnp.concatenate(m_l, axis=0), _coupling_masks(ri, ci, reverse)))
    cat = lambda xs: jnp.concatenate(xs, axis=0)
    a, kt = cat(a_l), cat(kt_l)
    wu = _bdot(_unit_triangular_inverse(m_parts, eye), cat(rhs_l))
    kb = _bdot_tn(kt, wu)
    qo = _bdot(a, wu)
    qeff = cat(qg_l) - qo[:, :, :DN_HEAD]
    egl = cat(egl_l)
    state = s_ref[...]
    o_refs = (of_ref, ob_ref)
    vh_step = DN_KH_STEP * DN_GROUP
    for step in range(nc):
        chunk_of = [step if d == 0 else nc - 1 - step for d in range(2) for _ in range(vh_step)]
        idx = [ch * nc + c for ch, c in enumerate(chunk_of)]
        pick = lambda x: jnp.stack([x[n] for n in idx])
        o = _bdot(pick(qeff), state) + pick(qo)[:, :, DN_HEAD:]
        kb_s = pick(kb)
        state = pick(egl) * state + kb_s[:, :, DN_HEAD:] - _bdot(kb_s[:, :, :DN_HEAD], state)
        for ch, c in enumerate(chunk_of):
            o_refs[ch // vh_step][0, chunk_rows[c], head_cols(ch % vh_step)] = o[ch]
    s_ref[...] = state


def _dn_scan(qkv, gates, g_rows):
    nb, t, _ = qkv.shape
    n_blocks = t // SCAN_BLOCK
    n_chunks = SCAN_BLOCK // CHUNK
    ks = DN_KH_STEP
    groups = DN_K_HEADS // ks
    fwd = lambda s: s
    bwd = lambda s: jnp.where(s == 0, 0, n_blocks - s)

    def specs(order):
        return [pl.BlockSpec((1, SCAN_BLOCK, ks * DN_HEAD), lambda b, h, s: (b, order(s), h)),
                pl.BlockSpec((1, SCAN_BLOCK, ks * DN_HEAD), lambda b, h, s: (b, order(s), groups + h)),
                pl.BlockSpec((1, SCAN_BLOCK, ks * DN_GROUP * DN_HEAD), lambda b, h, s: (b, order(s), groups + h)),
                pl.BlockSpec((1, SCAN_BLOCK, LANES), lambda b, h, s: (b, order(s), 0)),
                pl.BlockSpec((1, ks, n_chunks, 8, CHUNK), lambda b, h, s: (b, h, order(s), 0, 0))]

    out_spec = lambda order: pl.BlockSpec((1, SCAN_BLOCK, ks * DN_GROUP * DN_HEAD), lambda b, h, s: (b, order(s), h))
    o_shape = jax.ShapeDtypeStruct((nb, t, DN_V_DIM), F32)
    return pl.pallas_call(
        _dn_scan_kernel,
        grid=(nb, groups, n_blocks),
        in_specs=specs(fwd) + specs(bwd),
        out_specs=[out_spec(fwd), out_spec(bwd)],
        out_shape=[o_shape, o_shape],
        scratch_shapes=[pltpu.VMEM((2 * ks * DN_GROUP, DN_HEAD, DN_HEAD), F32)],
        compiler_params=_params(("arbitrary", "arbitrary", "arbitrary")),
        name="dn_scan",
    )(qkv, qkv, qkv, gates, g_rows, qkv, qkv, qkv, gates, g_rows)


_GLA_LEVELS = (32, 16, 8, 4, 2, 1)


def _gla_tables(reverse):
    idx = np.arange(CHUNK)
    tau = (CHUNK - 1 - idx) if reverse else idx
    ti, tk = tau[:, None], tau[None, :]
    groups = [tk <= ti, tk > ti]
    masks = []
    for s in _GLA_LEVELS:
        bi, bk = ti // s, tk // s
        groups.append((bk == bi) & (tk <= ti) & (tk > bi * s))
        groups.append(((bk == bi) & (tk > ti)) | (tk == (bi + 1) * s))
        masks.append((bi % 2 == 1) & (bk == bi - 1))
    masks.append(ti == tk)
    sel = np.concatenate(groups, axis=0).astype(np.float32)
    return jnp.asarray(sel, BF16), jnp.asarray(np.stack(masks).astype(np.float32))


def _gla_scan_kernel(qf_ref, kf_ref, vf_ref, lf_ref, wf_ref, bf_ref, self_ref, mf_ref,
                     qr_ref, kr_ref, vr_ref, lr_ref, wr_ref, br_ref, selr_ref, mr_ref,
                     of_ref, ob_ref, s_ref):
    @pl.when(pl.program_id(2) == 0)
    def _():
        s_ref[...] = jnp.zeros_like(s_ref)

    nc = SCAN_BLOCK // CHUNK
    n_lev = len(_GLA_LEVELS)
    chunk_rows = [slice(c * CHUNK, (c + 1) * CHUNK) for c in range(nc)]
    dirs = ((qf_ref, kf_ref, vf_ref, lf_ref, wf_ref, bf_ref, self_ref, mf_ref, False),
            (qr_ref, kr_ref, vr_ref, lr_ref, wr_ref, br_ref, selr_ref, mr_ref, True))
    hs = GLA_H_STEP
    kcols = lambda hh: slice(hh * GLA_HEAD_K, (hh + 1) * GLA_HEAD_K)
    vcols = lambda hh: slice(hh * GLA_HEAD_V, (hh + 1) * GLA_HEAD_V)
    ql, kl, qg_l, kt_l, e_l, v_l = [], [], [], [], [], []
    for q_ref, k_ref, v_ref, l_ref, w_ref, b_ref, sel_ref, m_ref, reverse in dirs:
        last = 0 if reverse else CHUNK - 1
        logits = _dot(l_ref[0], w_ref[...]) + b_ref[...]
        gk_all = (jnp.minimum(logits, 0.0) - jnp.log(1.0 + jnp.exp(-jnp.abs(logits)))) * (1.0 / GLA_GATE_NORMALIZER)
        for hh in range(hs):
            for rows in chunk_rows:
                q = q_ref[0, rows, kcols(hh)] * (GLA_HEAD_K ** -0.5)
                k = k_ref[0, rows, kcols(hh)]
                gk = gk_all[rows, kcols(hh)]
                hi = gk.astype(BF16)
                mid = (gk - hi.astype(F32)).astype(BF16)
                both = jnp.dot(sel_ref[...], jnp.concatenate([hi, mid], axis=-1), preferred_element_type=F32)
                sums = both[:, :GLA_HEAD_K] + both[:, GLA_HEAD_K:]
                part = lambda n: sums[n * CHUNK:(n + 1) * CHUNK]
                bcum, tail = part(0), part(1)
                ql += [q * jnp.exp(part(2 + 2 * lv)) for lv in range(n_lev)] + [q]
                kl += [k * jnp.exp(part(3 + 2 * lv)) for lv in range(n_lev)] + [k]
                qg_l.append(q * jnp.exp(bcum))
                kt_l.append(k * jnp.exp(tail))
                e_l.append(jnp.exp(bcum[last:last + 1, :]))
                v_l.append(v_ref[0, rows, vcols(hh)])
    scores = _bdot_nt(jnp.stack(ql), jnp.stack(kl))
    a_l = []
    for n in range(2 * hs * nc):
        m_ref = dirs[n // (hs * nc)][7]
        a = m_ref[0] * scores[n * (n_lev + 1)]
        for lv in range(1, n_lev + 1):
            a = a + m_ref[lv] * scores[n * (n_lev + 1) + lv]
        a_l.append(a)
    v = jnp.stack(v_l)
    x = _bdot_tn(v, jnp.stack(kt_l))
    s_l = [None] * (2 * hs * nc)
    for ch in range(2 * hs):
        state = s_ref[ch]
        for c in (range(nc - 1, -1, -1) if dirs[ch // hs][8] else range(nc)):
            n = ch * nc + c
            s_l[n] = state
            state = state * e_l[n] + x[n]
        s_ref[ch] = state
    o = _bdot(jnp.stack(a_l), v) + _bdot_nt(jnp.stack(qg_l), jnp.stack(s_l))
    for n in range(2 * hs * nc):
        ch, c = divmod(n, nc)
        (of_ref, ob_ref)[ch // hs][0, chunk_rows[c], vcols(ch % hs)] = o[n]


def _gla_scan(p, w2cat, b2cat):
    nb, t, _ = p.shape
    n_blocks = t // SCAN_BLOCK
    hs = GLA_H_STEP
    h = GLA_HEADS // hs
    low_block = (2 * GLA_K_DIM + 2 * GLA_V_DIM) // LANES
    fwd = lambda s: s
    bwd = lambda s: jnp.where(s == 0, 0, n_blocks - s)
    n_sel = (2 + 2 * len(_GLA_LEVELS)) * CHUNK
    n_mask = len(_GLA_LEVELS) + 1

    def specs(order, d):
        return [pl.BlockSpec((1, SCAN_BLOCK, hs * GLA_HEAD_K), lambda b, hh, s: (b, order(s), hh)),
                pl.BlockSpec((1, SCAN_BLOCK, hs * GLA_HEAD_K), lambda b, hh, s: (b, order(s), h + hh)),
                pl.BlockSpec((1, SCAN_BLOCK, hs * GLA_HEAD_V), lambda b, hh, s: (b, order(s), h + hh)),
                pl.BlockSpec((1, SCAN_BLOCK, LANES), lambda b, hh, s: (b, order(s), low_block)),
                pl.BlockSpec((LANES, hs * GLA_HEAD_K), lambda b, hh, s: (0, d * h + hh)),
                pl.BlockSpec((1, hs * GLA_HEAD_K), lambda b, hh, s: (0, d * h + hh)),
                pl.BlockSpec((n_sel, CHUNK), lambda b, hh, s: (0, 0)),
                pl.BlockSpec((n_mask, CHUNK, CHUNK), lambda b, hh, s: (0, 0, 0))]

    out_spec = lambda order: pl.BlockSpec((1, SCAN_BLOCK, hs * GLA_HEAD_V), lambda b, hh, s: (b, order(s), hh))
    o_shape = jax.ShapeDtypeStruct((nb, t, GLA_V_DIM), F32)
    sel_f, mask_f = _gla_tables(False)
    sel_r, mask_r = _gla_tables(True)
    return pl.pallas_call(
        _gla_scan_kernel,
        grid=(nb, h, n_blocks),
        in_specs=specs(fwd, 0) + specs(bwd, 1),
        out_specs=[out_spec(fwd), out_spec(bwd)],
        out_shape=[o_shape, o_shape],
        scratch_shapes=[pltpu.VMEM((2 * hs, GLA_HEAD_V, GLA_HEAD_K), F32)],
        compiler_params=_params(("arbitrary", "arbitrary", "arbitrary")),
        name="gla_scan",
    )(p, p, p, p, w2cat, b2cat, sel_f, mask_f, p, p, p, p, w2cat, b2cat, sel_r, mask_r)


def _attn_prep_kernel(p_ref, qg_ref, kg_ref, cos_ref, sin_ref, q_ref, k_ref, v_ref):
    cos = cos_ref[...]
    sin = sin_ref[...]
    lane = lax.broadcasted_iota(jnp.int32, (1, ATTN_HEAD), 1)
    first = (lane % (ATTN_HEAD // 2)) < (ATTN_HEAD // 4)
    q_scale = ATTN_HEAD ** -0.5 * math.log2(math.e)

    def norm_rope(x, g):
        ms = jnp.mean(x * x, axis=-1, keepdims=True)
        y = x * lax.rsqrt(ms + NORM_EPS) * g
        partner = jnp.where(first, pltpu.roll(y, ATTN_HEAD - ATTN_HEAD // 4, 1), pltpu.roll(y, ATTN_HEAD // 4, 1))
        return y * cos + partner * sin

    qd = ATTN_Q_HEADS * ATTN_HEAD
    kd = ATTN_KV_HEADS * ATTN_HEAD
    q_pieces = [norm_rope(p_ref[0, :, h * ATTN_HEAD:(h + 1) * ATTN_HEAD], qg_ref[...]) * q_scale
                for h in range(ATTN_Q_HEADS)]
    q_ref[0] = jnp.concatenate(q_pieces, axis=-1).astype(BF16)
    k_pieces = [norm_rope(p_ref[0, :, qd + h * ATTN_HEAD:qd + (h + 1) * ATTN_HEAD], kg_ref[...])
                for h in range(ATTN_KV_HEADS)]
    k_ref[0] = jnp.concatenate(k_pieces, axis=-1).astype(BF16)
    v_ref[0] = p_ref[0, :, qd + kd:qd + 2 * kd].astype(BF16)


def _rope_tables(t, ctx_len):
    n_rows = (t - ctx_len) // GRID_W
    axis_dim = ATTN_HEAD // 2
    inv_freq = jnp.power(ROPE_THETA, -jnp.arange(0, axis_dim, 2, dtype=F32) / axis_dim)
    ar = jnp.arange(n_rows, dtype=F32)[:, None] * inv_freq
    ac = jnp.arange(GRID_W, dtype=F32)[:, None] * inv_freq
    on_rows = lambda x: jnp.broadcast_to(x[:, None, :], (n_rows, GRID_W, x.shape[-1]))
    on_cols = lambda x: jnp.broadcast_to(x[None, :, :], (n_rows, GRID_W, x.shape[-1]))
    cr, sr, cc, sc = on_rows(jnp.cos(ar)), on_rows(jnp.sin(ar)), on_cols(jnp.cos(ac)), on_cols(jnp.sin(ac))
    cos = jnp.concatenate([cr, cr, cc, cc], axis=-1).reshape(t - ctx_len, ATTN_HEAD)
    sin = jnp.concatenate([-sr, sr, -sc, sc], axis=-1).reshape(t - ctx_len, ATTN_HEAD)
    cos = jnp.concatenate([jnp.ones((ctx_len, ATTN_HEAD), F32), cos], axis=0)
    sin = jnp.concatenate([jnp.zeros((ctx_len, ATTN_HEAD), F32), sin], axis=0)
    return cos, sin


def _attn_prep(p, q_g, k_g, cos, sin):
    nb, t, n = p.shape
    tm = _row_tile(t, 640)
    qd = ATTN_Q_HEADS * ATTN_HEAD
    kd = ATTN_KV_HEADS * ATTN_HEAD
    row = lambda w: pl.BlockSpec((1, tm, w), lambda b, i: (b, i, 0))
    tab = pl.BlockSpec((tm, ATTN_HEAD), lambda b, i: (i, 0))
    vec = pl.BlockSpec((1, ATTN_HEAD), lambda b, i: (0, 0))
    return pl.pallas_call(
        _attn_prep_kernel,
        grid=(nb, t // tm),
        in_specs=[row(n), vec, vec, tab, tab],
        out_specs=[row(qd), row(kd), row(kd)],
        out_shape=[jax.ShapeDtypeStruct((nb, t, qd), BF16),
                   jax.ShapeDtypeStruct((nb, t, kd), BF16),
                   jax.ShapeDtypeStruct((nb, t, kd), BF16)],
        compiler_params=_params(("arbitrary", "arbitrary")),
        name="attn_prep",
    )(p, q_g.reshape(1, -1), k_g.reshape(1, -1), cos, sin)


def _flash_kernel(q_ref, k_ref, v_ref, o_ref, qs_ref, m_ref, l_ref, acc_ref, sa_ref, sb_ref, *, tq, tk, ctx_len, t_total):
    i = pl.program_id(2)
    g, hd = ATTN_GROUP, ATTN_HEAD
    for h in range(g):
        qs_ref[h * tq:(h + 1) * tq, :] = q_ref[0, :, h * hd:(h + 1) * hd]
    m_ref[...] = jnp.full_like(m_ref, -jnp.inf)
    l_ref[...] = jnp.zeros_like(l_ref)
    acc_ref[...] = jnp.zeros_like(acc_ref)

    def scores(start, width):
        kc = k_ref[0, pl.ds(start, width), :]
        return lax.dot_general(qs_ref[...], kc, (((1,), (1,)), ((), ())), preferred_element_type=F32)

    def update(s, start, width):
        vc = v_ref[0, pl.ds(start, width), :]
        m_old = m_ref[...]
        m_new = jnp.maximum(m_old, jnp.max(s, axis=-1, keepdims=True))
        alpha = jnp.exp2(m_old - m_new)
        p = jnp.exp2(s - jnp.tile(m_new, (1, width // LANES)))
        psum = p[:, 0:LANES]
        for n in range(1, width // LANES):
            psum = psum + p[:, n * LANES:(n + 1) * LANES]
        l_ref[...] = alpha * l_ref[...] + psum
        acc_ref[...] = alpha * acc_ref[...] + jnp.dot(p.astype(BF16), vc, preferred_element_type=F32)
        m_ref[...] = m_new

    @pl.when(i * tq < ctx_len)
    def _():
        update(scores(0, ctx_len), 0, ctx_len)

    n_kv = t_total // tk

    @pl.when(i * tq >= ctx_len)
    def _():
        sa_ref[...] = scores(0, tk)

        def pair(c):
            first = pl.multiple_of(2 * c * tk, tk)
            second = pl.multiple_of(first + tk, tk)
            third = pl.multiple_of(jnp.minimum(2 * c + 2, n_kv - 1) * tk, tk)
            sb_ref[...] = scores(second, tk)
            update(sa_ref[...], first, tk)
            sa_ref[...] = scores(third, tk)
            update(sb_ref[...], second, tk)

        n_pairs = n_kv // 2
        unroll = 4 if n_pairs % 4 == 1 else 2

        def body(c, carry):
            for u in range(unroll):
                pair(unroll * c + u)
            return carry
        lax.fori_loop(0, n_pairs // unroll, body, 0)
        for c in range(n_pairs - n_pairs % unroll, n_pairs):
            pair(c)

    out = acc_ref[...] / jnp.sum(l_ref[...], axis=-1, keepdims=True)
    o_ref[0] = jnp.concatenate([out[h * tq:(h + 1) * tq] for h in range(g)], axis=-1).astype(BF16)


def _flash_attention(q, k, v, ctx_len):
    nb, t, qd = q.shape
    tq = 256
    tk = _row_tile(t, 640, LANES)
    assert ctx_len == tq and t % tq == 0 and ctx_len % LANES == 0 and (t // tk) % 2 == 0
    gw = ATTN_GROUP * ATTN_HEAD
    return pl.pallas_call(
        functools.partial(_flash_kernel, tq=tq, tk=tk, ctx_len=ctx_len, t_total=t),
        grid=(nb, ATTN_KV_HEADS, t // tq),
        in_specs=[pl.BlockSpec((1, tq, gw), lambda b, kv, i: (b, i, kv)),
                  pl.BlockSpec((1, t, ATTN_HEAD), lambda b, kv, i: (b, 0, kv)),
                  pl.BlockSpec((1, t, ATTN_HEAD), lambda b, kv, i: (b, 0, kv))],
        out_specs=pl.BlockSpec((1, tq, gw), lambda b, kv, i: (b, i, kv)),
        out_shape=jax.ShapeDtypeStruct((nb, t, qd), BF16),
        scratch_shapes=[pltpu.VMEM((ATTN_GROUP * tq, ATTN_HEAD), BF16),
                        pltpu.VMEM((ATTN_GROUP * tq, LANES), F32),
                        pltpu.VMEM((ATTN_GROUP * tq, LANES), F32),
                        pltpu.VMEM((ATTN_GROUP * tq, ATTN_HEAD), F32),
                        pltpu.VMEM((ATTN_GROUP * tq, tk), F32),
                        pltpu.VMEM((ATTN_GROUP * tq, tk), F32)],
        compiler_params=_params(("arbitrary", "arbitrary", "arbitrary")),
        name="flash_attention",
    )(q, k, v)


def _pad_cols(w, n):
    return jnp.pad(w, ((0, 0), (0, n - w.shape[1])))


def _deltanet_layer(xc, mod, norm_g, w_in, conv_w, a_log, dt_bias, out_norm_g, w_out, ctx_len):
    nb, t, _ = xc.shape
    gate_col = DN_QKV_DIM + DN_V_DIM
    conv_w_t = jnp.pad(conv_w.T.astype(F32), ((0, 8 - SHORT_CONV), (0, 0)))
    qkv = _dn_qkv(xc, mod, norm_g, w_in[:, :DN_QKV_DIM].astype(BF16), conv_w_t, ctx_len)
    w_gate = _pad_cols(w_in[:, gate_col:][:, _dn_gate_lane_perm()], LANES)
    w_rest = jnp.concatenate([w_in[:, DN_QKV_DIM:gate_col], w_gate], axis=1).astype(BF16)
    p = _in_projection(xc, mod, norm_g, w_rest, MXU_CHUNK, ctx_len)
    gb = _dn_gates(p, DN_V_DIM // LANES, a_log, dt_bias)
    per_head = 4 * DN_GROUP
    g_rows = gb[..., :DN_K_HEADS * per_head].reshape(nb, t // CHUNK, CHUNK, DN_K_HEADS, per_head)
    g_rows = g_rows.transpose(0, 3, 1, 4, 2)
    o_f, o_b = _dn_scan(qkv, gb, g_rows)
    return _out_projection([o_f, o_b], p, 0, out_norm_g, DN_V_HEADS, DN_HEAD,
                           w_out.astype(BF16), xc, mod, ctx_len, 320)


def _gla_layer(xc, mod, norm_g, w_in, gate_w2, gate_b2, out_norm_g, w_out, ctx_len):
    n_pad = 2 * GLA_K_DIM + 2 * GLA_V_DIM + LANES
    p = _in_projection(xc, mod, norm_g, _pad_cols(w_in, n_pad).astype(BF16), MXU_CHUNK, ctx_len)
    r = GLA_GATE_RANK
    w2cat = jnp.zeros((LANES, 2 * GLA_K_DIM), F32)
    w2cat = w2cat.at[0:r, :GLA_K_DIM].set(gate_w2[0]).at[r:2 * r, GLA_K_DIM:].set(gate_w2[1]).astype(BF16)
    b2cat = gate_b2.reshape(1, 2 * GLA_K_DIM).astype(F32)
    o_f, o_b = _gla_scan(p, w2cat, b2cat)
    z_block = (2 * GLA_K_DIM + GLA_V_DIM) // GLA_V_DIM
    return _out_projection([o_f, o_b], p, z_block, out_norm_g, GLA_HEADS, GLA_HEAD_V,
                           w_out.astype(BF16), xc, mod, ctx_len, 640)


def _attention_layer(xc, mod, norm_g, w_in, q_g, k_g, w_out, rope, ctx_len):
    p = _in_projection(xc, mod, norm_g, w_in.astype(BF16), MXU_CHUNK, ctx_len)
    q, k, v = _attn_prep(p, q_g, k_g, *rope)
    o = _flash_attention(q, k, v, ctx_len)
    return _out_projection([o], None, 0, None, 0, 0, w_out.astype(BF16), xc, mod, ctx_len, 1280)


def kernel(x, c, ctx, c_ctx, ada_w, ada_b, norm_mix_g, norm_ffn_g, ffn_w1, ffn_w2, dn_w_in, dn_conv_w, dn_a_log, dn_dt_bias, dn_norm_g, dn_w_out, gla_w_in, gla_gate_w2, gla_gate_b2, gla_norm_g, gla_w_out, attn_w_in, attn_q_norm_g, attn_k_norm_g, attn_w_out):
    nb, seq, d = x.shape
    ctx_len = ctx.shape[1]
    depth = ada_w.shape[0]
    assert ctx_len == SCAN_BLOCK and seq % SCAN_BLOCK == 0 and nb < MOD_ROWS
    t = ctx_len + seq
    xc = jnp.concatenate([ctx, x], axis=1)
    cvec = jnp.zeros((MOD_ROWS, d), F32).at[:nb].set(c).at[nb].set(c_ctx)
    mods = _ada_vectors(cvec, ada_w, ada_b)
    rope = _rope_tables(t, ctx_len)
    for i in range(depth):
        mix, slot = i % 3, i // 3
        mod = mods[i]
        if mix == 0:
            xc = _deltanet_layer(xc, mod, norm_mix_g[i], dn_w_in[slot], dn_conv_w[slot], dn_a_log[slot],
                                 dn_dt_bias[slot], dn_norm_g[slot], dn_w_out[slot], ctx_len)
        elif mix == 1:
            xc = _gla_layer(xc, mod, norm_mix_g[i], gla_w_in[slot], gla_gate_w2[slot], gla_gate_b2[slot],
                            gla_norm_g[slot], gla_w_out[slot], ctx_len)
        else:
            xc = _attention_layer(xc, mod, norm_mix_g[i], attn_w_in[slot], attn_q_norm_g[slot],
                                  attn_k_norm_g[slot], attn_w_out[slot], rope, ctx_len)
        xc = _ffn(xc, mod, norm_ffn_g[i], ffn_w1[i].astype(BF16), ffn_w2[i].astype(BF16), ctx_len)
    return xc[:, ctx_len:, :]
```

```python
import functools
import math

import numpy as np
import jax
import jax.numpy as jnp
from jax import lax
from jax.experimental import pallas as pl
from jax.experimental.pallas import tpu as pltpu

F32 = jnp.float32
BF16 = jnp.bfloat16

NORM_EPS = 1e-6
GRID_W = 64
ROPE_THETA = 10000.0
SHORT_CONV = 5

DN_K_HEADS = 8
DN_V_HEADS = 16
DN_HEAD = 128
DN_GROUP = DN_V_HEADS // DN_K_HEADS
DN_K_DIM = DN_K_HEADS * DN_HEAD
DN_V_DIM = DN_V_HEADS * DN_HEAD
DN_QKV_DIM = 2 * DN_K_DIM + DN_V_DIM
DN_KH_STEP = 2

GLA_HEADS = 4
GLA_HEAD_K = 128
GLA_HEAD_V = 256
GLA_K_DIM = GLA_HEADS * GLA_HEAD_K
GLA_V_DIM = GLA_HEADS * GLA_HEAD_V
GLA_GATE_RANK = 16
GLA_GATE_NORMALIZER = 16.0
GLA_H_STEP = 2

ATTN_Q_HEADS = 8
ATTN_KV_HEADS = 2
ATTN_HEAD = 128
ATTN_GROUP = ATTN_Q_HEADS // ATTN_KV_HEADS

CHUNK = 64
SCAN_BLOCK = 256
LANES = 128
MXU_CHUNK = 1024
MOD_ROWS = 8
VMEM_LIMIT = 56 * 1024 * 1024


def _params(semantics, vmem=VMEM_LIMIT):
    return pltpu.CompilerParams(dimension_semantics=semantics, vmem_limit_bytes=vmem)


def _sigmoid(x):
    return 1.0 / (1.0 + jnp.exp(-x))


def _softplus(x):
    return jnp.maximum(x, 0.0) + jnp.log(1.0 + jnp.exp(-jnp.abs(x)))


def _split3(x):
    hi = x.astype(BF16)
    r1 = x - hi.astype(F32)
    mid = r1.astype(BF16)
    lo = (r1 - mid.astype(F32)).astype(BF16)
    return hi, mid, lo


def _dot(a, b):
    return jnp.dot(a.astype(BF16), b.astype(BF16), preferred_element_type=F32)


def _dot_nt(a, b):
    return lax.dot_general(a.astype(BF16), b.astype(BF16), (((1,), (1,)), ((), ())),
                           preferred_element_type=F32)


def _dot_tn(a, b):
    return lax.dot_general(a.astype(BF16), b.astype(BF16), (((0,), (0,)), ((), ())),
                           preferred_element_type=F32)


def _dot_sel(p_bf16, x):
    hi, mid, lo = _split3(x)
    d = lambda y: jnp.dot(p_bf16, y, preferred_element_type=F32)
    return d(hi) + d(mid) + d(lo)


def _row_tile(total, target, multiple=8):
    best = None
    for t in range(multiple, min(total, target) + 1, multiple):
        if total % t == 0:
            best = t
    assert best is not None, (total, target, multiple)
    return best


def _mod_norm(x, g, mod_ref, b, row0, ctx_len, nb, shift_idx, scale_idx):
    d = x.shape[-1]
    ms = jnp.mean(x * x, axis=-1, keepdims=True)
    y = x * lax.rsqrt(ms + NORM_EPS) * g
    rows = row0 + lax.broadcasted_iota(jnp.int32, (x.shape[0], 1), 0)
    is_ctx = rows < ctx_len

    def pick(idx):
        vx = mod_ref[pl.ds(b, 1), idx * d:(idx + 1) * d]
        vc = mod_ref[nb:nb + 1, idx * d:(idx + 1) * d]
        return jnp.where(is_ctx, vc, vx)

    return y * (1.0 + pick(scale_idx)) + pick(shift_idx), is_ctx, pick


def _ada_kernel(c_ref, w_ref, b_ref, o_ref):
    c = c_ref[...]
    s = c * _sigmoid(c)
    o_ref[0] = jnp.dot(s, w_ref[0], preferred_element_type=F32,
                       precision=lax.Precision.HIGHEST) + b_ref[0]


def _ada_vectors(cvec, ada_w, ada_b):
    depth, d, n = ada_w.shape
    tn = _row_tile(n, 1536, LANES)
    return pl.pallas_call(
        _ada_kernel,
        grid=(depth, n // tn),
        in_specs=[pl.BlockSpec((MOD_ROWS, d), lambda l, j: (0, 0)),
                  pl.BlockSpec((1, d, tn), lambda l, j: (l, 0, j)),
                  pl.BlockSpec((1, 1, tn), lambda l, j: (l, 0, j))],
        out_specs=pl.BlockSpec((1, MOD_ROWS, tn), lambda l, j: (l, 0, j)),
        out_shape=jax.ShapeDtypeStruct((depth, MOD_ROWS, n), F32),
        compiler_params=_params(("arbitrary", "arbitrary")),
        name="ada_vectors",
    )(cvec, ada_w, ada_b.reshape(depth, 1, n))


def _inproj_kernel(x_ref, mod_ref, g_ref, w_ref, o_ref, *, tm, tn, ctx_len, nb):
    b = pl.program_id(0)
    i = pl.program_id(1)
    h, _, _ = _mod_norm(x_ref[0], g_ref[...], mod_ref, b, i * tm, ctx_len, nb, 0, 1)
    h = h.astype(BF16)
    n = w_ref.shape[1]
    for start in range(0, n, tn):
        cols = slice(start, min(start + tn, n))
        o_ref[0, :, cols] = jnp.dot(h, w_ref[:, cols], preferred_element_type=F32)


def _in_projection(xc, mod, g, w_bf16, tn, ctx_len):
    nb, t, d = xc.shape
    n = w_bf16.shape[1]
    tm = _row_tile(t, 640)
    return pl.pallas_call(
        functools.partial(_inproj_kernel, tm=tm, tn=tn, ctx_len=ctx_len, nb=nb),
        grid=(nb, t // tm),
        in_specs=[pl.BlockSpec((1, tm, d), lambda b, i: (b, i, 0)),
                  pl.BlockSpec(mod.shape, lambda b, i: (0, 0)),
                  pl.BlockSpec((1, d), lambda b, i: (0, 0)),
                  pl.BlockSpec((d, n), lambda b, i: (0, 0), pipeline_mode=pl.Buffered(1))],
        out_specs=pl.BlockSpec((1, tm, n), lambda b, i: (b, i, 0)),
        out_shape=jax.ShapeDtypeStruct((nb, t, n), F32),
        compiler_params=_params(("arbitrary", "arbitrary")),
        name="in_projection",
    )(xc, mod, g.reshape(1, d), w_bf16)


def _outproj_kernel(*refs, n_o, gated, heads, head_dim, tm, ctx_len, nb):
    o_refs = refs[:n_o]
    pos = n_o
    if gated:
        z_ref, ng_ref = refs[pos], refs[pos + 1]
        pos += 2
    w_ref, x_ref, mod_ref, out_ref = refs[pos:pos + 4]
    b = pl.program_id(0)
    i = pl.program_id(1)
    if gated:
        pieces = []
        for h in range(heads):
            sl = slice(h * head_dim, (h + 1) * head_dim)
            o = o_refs[0][0, :, sl]
            for r in o_refs[1:]:
                o = o + r[0, :, sl]
            ms = jnp.mean(o * o, axis=-1, keepdims=True)
            o = o * lax.rsqrt(ms + NORM_EPS) * ng_ref[...]
            z = z_ref[0, :, sl]
            pieces.append((o * (z * _sigmoid(z))).astype(BF16))
        lhs = jnp.concatenate(pieces, axis=-1)
    else:
        lhs = o_refs[0][0]
    y = jnp.dot(lhs, w_ref[...], preferred_element_type=F32)
    d = y.shape[-1]
    rows = i * tm + lax.broadcasted_iota(jnp.int32, (tm, 1), 0)
    gate = jnp.where(rows < ctx_len, mod_ref[nb:nb + 1, 2 * d:3 * d], mod_ref[pl.ds(b, 1), 2 * d:3 * d])
    out_ref[0] = x_ref[0] + gate * y


def _out_projection(o_list, z_src, z_col_block, norm_g, heads, head_dim, w_bf16, xc, mod, ctx_len, tm_target):
    nb, t, d = xc.shape
    dv = w_bf16.shape[0]
    tm = _row_tile(t, tm_target)
    gated = z_src is not None
    row_spec = lambda width, col: pl.BlockSpec((1, tm, width), lambda b, i: (b, i, col))
    in_specs = [row_spec(dv, 0) for _ in o_list]
    args = list(o_list)
    if gated:
        in_specs += [row_spec(dv, z_col_block), pl.BlockSpec((1, head_dim), lambda b, i: (0, 0))]
        args += [z_src, norm_g.reshape(1, head_dim)]
    in_specs += [pl.BlockSpec((dv, d), lambda b, i: (0, 0)), row_spec(d, 0),
                 pl.BlockSpec(mod.shape, lambda b, i: (0, 0))]
    args += [w_bf16, xc, mod]
    return pl.pallas_call(
        functools.partial(_outproj_kernel, n_o=len(o_list), gated=gated, heads=heads, head_dim=head_dim,
                          tm=tm, ctx_len=ctx_len, nb=nb),
        grid=(nb, t // tm),
        in_specs=in_specs,
        out_specs=row_spec(d, 0),
        out_shape=jax.ShapeDtypeStruct((nb, t, d), F32),
        compiler_params=_params(("arbitrary", "arbitrary")),
        name="out_projection",
    )(*args)


def _ffn_kernel(x_ref, mod_ref, g_ref, w1_ref, w2_ref, o_ref, acc_ref, *, tm, fk, ctx_len, nb):
    b = pl.program_id(0)
    i = pl.program_id(1)
    x = x_ref[0]
    h, _, pick = _mod_norm(x, g_ref[...], mod_ref, b, i * tm, ctx_len, nb, 3, 4)
    h = h.astype(BF16)
    d_ff = w1_ref.shape[1]
    for k in range(d_ff // fk):
        u = jnp.dot(h, w1_ref[:, k * fk:(k + 1) * fk], preferred_element_type=F32)
        u = jnp.maximum(u, 0.0)
        u = (u * u).astype(BF16)
        contrib = jnp.dot(u, w2_ref[k * fk:(k + 1) * fk, :], preferred_element_type=F32)
        if k == 0:
            acc_ref[...] = contrib
        else:
            acc_ref[...] += contrib
    o_ref[0] = x + pick(5) * acc_ref[...]


def _ffn(xc, mod, g, w1_bf16, w2_bf16, ctx_len):
    nb, t, d = xc.shape
    d_ff = w1_bf16.shape[1]
    tm = _row_tile(t, 640)
    return pl.pallas_call(
        functools.partial(_ffn_kernel, tm=tm, fk=512, ctx_len=ctx_len, nb=nb),
        grid=(nb, t // tm),
        in_specs=[pl.BlockSpec((1, tm, d), lambda b, i: (b, i, 0)),
                  pl.BlockSpec(mod.shape, lambda b, i: (0, 0)),
                  pl.BlockSpec((1, d), lambda b, i: (0, 0)),
                  pl.BlockSpec((d, d_ff), lambda b, i: (0, 0)),
                  pl.BlockSpec((d_ff, d), lambda b, i: (0, 0))],
        out_specs=pl.BlockSpec((1, tm, d), lambda b, i: (b, i, 0)),
        out_shape=jax.ShapeDtypeStruct((nb, t, d), F32),
        scratch_shapes=[pltpu.VMEM((tm, d), F32)],
        compiler_params=_params(("arbitrary", "arbitrary")),
        name="ffn",
    )(xc, mod, g.reshape(1, d), w1_bf16, w2_bf16)


HALO = 8


def _dn_qkv_kernel(x_ref, xp_ref, xn_ref, mod_ref, g_ref, w_ref, cw_ref, o_ref, *,
                   tm, t_total, ctx_len, nb, fix_tile, fix_row):
    b = pl.program_id(0)
    i = pl.program_id(1)
    row0 = i * tm
    n_ext = tm + 2 * HALO
    pad = SHORT_CONV // 2
    x_ext = jnp.concatenate([xp_ref[0], x_ref[0], xn_ref[0]], axis=0)
    h, _, _ = _mod_norm(x_ext, g_ref[...], mod_ref, b, row0 - HALO, ctx_len, nb, 0, 1)
    t = row0 - HALO + lax.broadcasted_iota(jnp.int32, (n_ext, 1), 0)
    nearest = jnp.clip(t, row0, row0 + tm - 1)
    valid = (jnp.where(t >= 0, 1, 0) * jnp.where(t < t_total, 1, 0)
             * jnp.where(jnp.where(t >= ctx_len, 1, 0) == jnp.where(nearest >= ctx_len, 1, 0), 1, 0))
    h = jnp.where(valid > 0, h, 0.0).astype(BF16)
    if fix_row is not None:
        rr = lax.broadcasted_iota(jnp.int32, (2 * HALO, 1), 0)
        crosses = {}
        for d in range(-pad, pad + 1):
            if d:
                inside = jnp.where(rr + d >= 0, 1, 0) * jnp.where(rr + d < 2 * HALO, 1, 0)
                other = jnp.where(jnp.where(rr < HALO, 1, 0) != jnp.where(rr + d < HALO, 1, 0), 1, 0)
                crosses[d] = jnp.where(i == fix_tile, inside * other, 0) > 0
    n = w_ref.shape[1]
    for start in range(0, n, MXU_CHUNK):
        res = jnp.dot(h, w_ref[:, start:start + MXU_CHUNK], preferred_element_type=F32)
        for hd in range(MXU_CHUNK // DN_HEAD):
            col0 = start + hd * DN_HEAD
            cols = slice(col0, col0 + DN_HEAD)
            r = res[:, hd * DN_HEAD:(hd + 1) * DN_HEAD]
            acc = None
            for tap in range(SHORT_CONV):
                d = tap - pad
                xs = (r if d == 0 else pltpu.roll(r, (-d) % n_ext, 0))[HALO:HALO + tm]
                term = xs * cw_ref[tap:tap + 1, cols]
                acc = term if acc is None else acc + term
            if fix_row is not None:
                slab = r[fix_row:fix_row + 2 * HALO]
                wrong = None
                for d, mask in crosses.items():
                    term = jnp.where(mask, pltpu.roll(slab, (-d) % (2 * HALO), 0), 0.0) * cw_ref[d + pad:d + pad + 1, cols]
                    wrong = term if wrong is None else wrong + term
                acc = jnp.concatenate([acc[:fix_row - HALO], acc[fix_row - HALO:fix_row + HALO] - wrong,
                                       acc[fix_row + HALO:]], axis=0)
            half = 0.5 * acc
            y = half + half * jnp.tanh(half)
            if col0 < 2 * DN_K_DIM:
                ss = jnp.sum(y * y, axis=-1, keepdims=True)
                y = y * (lax.rsqrt(ss + NORM_EPS) * (DN_HEAD ** -0.5 if col0 < DN_K_DIM else 1.0))
            o_ref[0, :, cols] = y


def _dn_qkv(xc, mod, g, w_qkv_bf16, conv_w_t, ctx_len):
    nb, t, d = xc.shape
    n = w_qkv_bf16.shape[1]
    tm = _row_tile(t, 640)
    hb = tm // HALO
    last = t // HALO - 1
    fix_tile, fix_row = (ctx_len // tm, ctx_len % tm) if ctx_len % tm else (None, None)
    assert fix_row is None or (fix_row % HALO == 0 and HALO <= fix_row <= tm - HALO)
    return pl.pallas_call(
        functools.partial(_dn_qkv_kernel, tm=tm, t_total=t, ctx_len=ctx_len, nb=nb,
                          fix_tile=fix_tile, fix_row=fix_row),
        grid=(nb, t // tm),
        in_specs=[pl.BlockSpec((1, tm, d), lambda b, i: (b, i, 0)),
                  pl.BlockSpec((1, HALO, d), lambda b, i: (b, jnp.maximum(i * hb - 1, 0), 0)),
                  pl.BlockSpec((1, HALO, d), lambda b, i: (b, jnp.minimum((i + 1) * hb, last), 0)),
                  pl.BlockSpec(mod.shape, lambda b, i: (0, 0)),
                  pl.BlockSpec((1, d), lambda b, i: (0, 0)),
                  pl.BlockSpec((d, n), lambda b, i: (0, 0), pipeline_mode=pl.Buffered(1)),
                  pl.BlockSpec((8, n), lambda b, i: (0, 0))],
        out_specs=pl.BlockSpec((1, tm, n), lambda b, i: (b, i, 0)),
        out_shape=jax.ShapeDtypeStruct((nb, t, n), F32),
        compiler_params=_params(("arbitrary", "arbitrary")),
        name="dn_qkv",
    )(xc, xc, xc, mod, g.reshape(1, d), w_qkv_bf16, conv_w_t)


def _dn_gate_kernel(ab_ref, par_ref, o_ref, *, tm):
    x = ab_ref[0]
    g = -jnp.exp(par_ref[0:1, :]) * _softplus(x + par_ref[1:2, :])
    beta = _sigmoid(x)
    lane = lax.broadcasted_iota(jnp.int32, (1, LANES), 1)
    used = lane < 4 * DN_V_HEADS
    is_beta = (lane & (2 * DN_GROUP)) != 0
    is_reverse = (lane & DN_GROUP) != 0
    r = lax.broadcasted_iota(jnp.int32, (CHUNK, CHUNK), 0)
    c = lax.broadcasted_iota(jnp.int32, (CHUNK, CHUNK), 1)
    lower = jnp.where(c <= r, 1.0, 0.0).astype(BF16)
    upper = jnp.where(c >= r, 1.0, 0.0).astype(BF16)
    for k in range(tm // CHUNK):
        rows = slice(k * CHUNK, (k + 1) * CHUNK)
        gk = g[rows]
        fwd = _dot_sel(lower, gk)
        bwd = _dot_sel(upper, gk)
        o_ref[0, rows, :] = jnp.where(used, jnp.where(is_beta, beta[rows], jnp.where(is_reverse, bwd, fwd)), 0.0)


def _dn_gate_lane_perm():
    perm = np.zeros(4 * DN_V_HEADS, np.int32)
    for kh in range(DN_K_HEADS):
        for kind in range(2):
            for d in range(2):
                for j in range(DN_GROUP):
                    lane = ((kh * 2 + kind) * 2 + d) * DN_GROUP + j
                    perm[lane] = (kind * 2 + d) * DN_V_HEADS + kh * DN_GROUP + j
    return perm


def _dn_gates(p, ab_col_block, a_log, dt_bias):
    nb, t, _ = p.shape
    tm = _row_tile(t, 1280, CHUNK)
    perm = _dn_gate_lane_perm()
    on_lanes = lambda v: jnp.concatenate([v.reshape(-1).astype(F32), jnp.zeros(2 * DN_V_HEADS, F32)])[perm]
    par = jnp.zeros((8, LANES), F32)
    par = par.at[0, :4 * DN_V_HEADS].set(on_lanes(a_log))
    par = par.at[1, :4 * DN_V_HEADS].set(on_lanes(dt_bias))
    return pl.pallas_call(
        functools.partial(_dn_gate_kernel, tm=tm),
        grid=(nb, t // tm),
        in_specs=[pl.BlockSpec((1, tm, LANES), lambda b, i: (b, i, ab_col_block)),
                  pl.BlockSpec((8, LANES), lambda b, i: (0, 0))],
        out_specs=pl.BlockSpec((1, tm, LANES), lambda b, i: (b, i, 0)),
        out_shape=jax.ShapeDtypeStruct((nb, t, LANES), F32),
        compiler_params=_params(("arbitrary", "arbitrary")),
        name="dn_gates",
    )(p, par)


def _bdot(a, b):
    return jnp.einsum("nij,njk->nik", a.astype(BF16), b.astype(BF16), preferred_element_type=F32)


def _bdot_nt(a, b):
    return jnp.einsum("nid,njd->nij", a.astype(BF16), b.astype(BF16), preferred_element_type=F32)


def _bdot_tn(a, b):
    return jnp.einsum("nci,ncj->nij", a.astype(BF16), b.astype(BF16), preferred_element_type=F32)


N_LEVELS = int(math.log2(CHUNK))


def _unit_triangular_inverse(parts, eye):
    coupling = lambda lv: jnp.concatenate([jnp.where(masks[lv], m, 0.0) for m, masks in parts], axis=0)
    x = eye - coupling(0)
    for lv in range(1, N_LEVELS):
        x = x - _bdot(x, _bdot(coupling(lv), x))
    return x


def _coupling_masks(ri, ci, reverse):
    ti, tj = (CHUNK - 1 - ri, CHUNK - 1 - ci) if reverse else (ri, ci)
    masks = []
    for lv in range(N_LEVELS):
        bi, bj = lax.shift_right_logical(ti, lv), lax.shift_right_logical(tj, lv)
        masks.append(jnp.where((bi & 1) == 1, bi - 1, -1) == bj)
    return masks


def _dn_scan_kernel(qf_ref, kf_ref, vf_ref, gf_ref, gtf_ref,
                    qr_ref, kr_ref, vr_ref, gr_ref, gtr_ref,
                    of_ref, ob_ref, s_ref):
    @pl.when(pl.program_id(2) == 0)
    def _():
        s_ref[...] = jnp.zeros_like(s_ref)

    ri = lax.broadcasted_iota(jnp.int32, (CHUNK, CHUNK), 0)
    ci = lax.broadcasted_iota(jnp.int32, (CHUNK, CHUNK), 1)
    eye = jnp.where(ri == ci, 1.0, 0.0)
    nc = SCAN_BLOCK // CHUNK
    chunk_rows = [slice(c * CHUNK, (c + 1) * CHUNK) for c in range(nc)]
    dirs = ((qf_ref, kf_ref, vf_ref, gf_ref, gtf_ref, False),
            (qr_ref, kr_ref, vr_ref, gr_ref, gtr_ref, True))
    m_parts, rhs_l, a_l, qg_l, kt_l, egl_l = [], [], [], [], [], []
    head_cols = lambda n: slice(n * DN_HEAD, (n + 1) * DN_HEAD)
    for d, (q_ref, k_ref, v_ref, g_ref, gt_ref, reverse) in enumerate(dirs):
        incl = (ci >= ri) if reverse else (ci <= ri)
        strict = (ci > ri) if reverse else (ci < ri)
        last = 0 if reverse else CHUNK - 1
        m_l = []
        per_head = 4 * DN_GROUP
        first_lane = pl.program_id(1) * (DN_KH_STEP * per_head)
        g_all = pltpu.roll(g_ref[0], jnp.where(first_lane == 0, 0, LANES - first_lane), 1)
        for hh in range(DN_KH_STEP):
            q = jnp.stack([q_ref[0, r, head_cols(hh)] for r in chunk_rows])
            k = jnp.stack([k_ref[0, r, head_cols(hh)] for r in chunk_rows])
            qk_kk = _bdot_nt(jnp.concatenate([q, k], axis=1), k)
            qk, kk = qk_kk[:, :CHUNK], qk_kk[:, CHUNK:]
            for j in range(DN_GROUP):
                col = 2 * d + j
                lane = hh * per_head + col
                gc = jnp.stack([g_all[r, lane:lane + 1] for r in chunk_rows])
                bc = jnp.stack([g_all[r, lane + 2 * DN_GROUP:lane + 2 * DN_GROUP + 1] for r in chunk_rows])
                gr = jnp.stack([gt_ref[0, hh, c, col:col + 1, :] for c in range(nc)])
                v = jnp.stack([v_ref[0, r, head_cols(hh * DN_GROUP + j)] for r in chunk_rows])
                decay = jnp.where(incl, jnp.exp(jnp.where(incl, gc - gr, 0.0)), 0.0)
                eg = jnp.exp(gc)
                gl = gc[:, last:last + 1, :]
                m_l.append(jnp.where(strict, bc * kk * decay, 0.0))
                rhs_l.append(jnp.concatenate([k * (bc * eg), v * bc], axis=-1))
                a_l.append(qk * decay)
                qg_l.append(q * eg)
                kt_l.append(k * jnp.exp(gl - gc))
                egl_l.append(jnp.exp(gl))
        m_parts.append((jnp.concatenate(m_l, axis=0), _coupling_masks(ri, ci, reverse)))
    cat = lambda xs: jnp.concatenate(xs, axis=0)
    a, kt = cat(a_l), cat(kt_l)
    wu = _bdot(_unit_triangular_inverse(m_parts, eye), cat(rhs_l))
    kb = _bdot_tn(kt, wu)
    qo = _bdot(a, wu)
    qeff = cat(qg_l) - qo[:, :, :DN_HEAD]
    egl = cat(egl_l)
    state = s_ref[...]
    o_refs = (of_ref, ob_ref)
    vh_step = DN_KH_STEP * DN_GROUP
    for step in range(nc):
        chunk_of = [step if d == 0 else nc - 1 - step for d in range(2) for _ in range(vh_step)]
        idx = [ch * nc + c for ch, c in enumerate(chunk_of)]
        pick = lambda x: jnp.stack([x[n] for n in idx])
        kb_s = pick(kb)
        qs_ks = _bdot(jnp.concatenate([pick(qeff), kb_s[:, :, :DN_HEAD]], axis=1), state)
        o = qs_ks[:, :CHUNK] + pick(qo)[:, :, DN_HEAD:]
        state = pick(egl) * state + kb_s[:, :, DN_HEAD:] - qs_ks[:, CHUNK:]
        for ch, c in enumerate(chunk_of):
            o_refs[ch // vh_step][0, chunk_rows[c], head_cols(ch % vh_step)] = o[ch]
    s_ref[...] = state


def _dn_scan(qkv, gates, g_rows):
    nb, t, _ = qkv.shape
    n_blocks = t // SCAN_BLOCK
    n_chunks = SCAN_BLOCK // CHUNK
    ks = DN_KH_STEP
    groups = DN_K_HEADS // ks
    fwd = lambda s: s
    bwd = lambda s: jnp.where(s == 0, 0, n_blocks - s)

    def specs(order):
        return [pl.BlockSpec((1, SCAN_BLOCK, ks * DN_HEAD), lambda b, h, s: (b, order(s), h)),
                pl.BlockSpec((1, SCAN_BLOCK, ks * DN_HEAD), lambda b, h, s: (b, order(s), groups + h)),
                pl.BlockSpec((1, SCAN_BLOCK, ks * DN_GROUP * DN_HEAD), lambda b, h, s: (b, order(s), groups + h)),
                pl.BlockSpec((1, SCAN_BLOCK, LANES), lambda b, h, s: (b, order(s), 0)),
                pl.BlockSpec((1, ks, n_chunks, 8, CHUNK), lambda b, h, s: (b, h, order(s), 0, 0))]

    out_spec = lambda order: pl.BlockSpec((1, SCAN_BLOCK, ks * DN_GROUP * DN_HEAD), lambda b, h, s: (b, order(s), h))
    o_shape = jax.ShapeDtypeStruct((nb, t, DN_V_DIM), F32)
    return pl.pallas_call(
        _dn_scan_kernel,
        grid=(nb, groups, n_blocks),
        in_specs=specs(fwd) + specs(bwd),
        out_specs=[out_spec(fwd), out_spec(bwd)],
        out_shape=[o_shape, o_shape],
        scratch_shapes=[pltpu.VMEM((2 * ks * DN_GROUP, DN_HEAD, DN_HEAD), F32)],
        compiler_params=_params(("arbitrary", "arbitrary", "arbitrary")),
        name="dn_scan",
    )(qkv, qkv, qkv, gates, g_rows, qkv, qkv, qkv, gates, g_rows)


_GLA_LEVELS = (32, 16, 8, 4, 2, 1)


def _gla_tables(reverse):
    idx = np.arange(CHUNK)
    tau = (CHUNK - 1 - idx) if reverse else idx
    ti, tk = tau[:, None], tau[None, :]
    groups = [tk <= ti, tk > ti]
    masks = []
    for s in _GLA_LEVELS:
        bi, bk = ti // s, tk // s
        if s > 1:
            groups.append((bk == bi) & (tk <= ti) & (tk > bi * s))
        groups.append(((bk == bi) & (tk > ti)) | (tk == (bi + 1) * s))
        masks.append((bi % 2 == 1) & (bk == bi - 1))
    masks.append(ti == tk)
    sel = np.concatenate(groups, axis=0).astype(np.float32)
    return jnp.asarray(sel, BF16), jnp.asarray(np.stack(masks).astype(np.float32))


def _gla_scan_kernel(qf_ref, kf_ref, vf_ref, lf_ref, wf_ref, bf_ref, self_ref, mf_ref,
                     qr_ref, kr_ref, vr_ref, lr_ref, wr_ref, br_ref, selr_ref, mr_ref,
                     of_ref, ob_ref, s_ref):
    @pl.when(pl.program_id(2) == 0)
    def _():
        s_ref[...] = jnp.zeros_like(s_ref)

    nc = SCAN_BLOCK // CHUNK
    n_lev = len(_GLA_LEVELS)
    chunk_rows = [slice(c * CHUNK, (c + 1) * CHUNK) for c in range(nc)]
    dirs = ((qf_ref, kf_ref, vf_ref, lf_ref, wf_ref, bf_ref, self_ref, mf_ref, False),
            (qr_ref, kr_ref, vr_ref, lr_ref, wr_ref, br_ref, selr_ref, mr_ref, True))
    hs = GLA_H_STEP
    kcols = lambda hh: slice(hh * GLA_HEAD_K, (hh + 1) * GLA_HEAD_K)
    vcols = lambda hh: slice(hh * GLA_HEAD_V, (hh + 1) * GLA_HEAD_V)
    ql, kl, qg_l, kt_l, e_l, v_l = [], [], [], [], [], []
    for q_ref, k_ref, v_ref, l_ref, w_ref, b_ref, sel_ref, m_ref, reverse in dirs:
        last = 0 if reverse else CHUNK - 1
        logits = _dot(l_ref[0], w_ref[...]) + b_ref[...]
        gk_all = (jnp.minimum(logits, 0.0) - jnp.log(1.0 + jnp.exp(-jnp.abs(logits)))) * (1.0 / GLA_GATE_NORMALIZER)
        for hh in range(hs):
            for rows in chunk_rows:
                q = q_ref[0, rows, kcols(hh)] * (GLA_HEAD_K ** -0.5)
                k = k_ref[0, rows, kcols(hh)]
                gk = gk_all[rows, kcols(hh)]
                hi = gk.astype(BF16)
                mid = (gk - hi.astype(F32)).astype(BF16)
                both = jnp.dot(sel_ref[...], jnp.concatenate([hi, mid], axis=-1), preferred_element_type=F32)
                sums = both[:, :GLA_HEAD_K] + both[:, GLA_HEAD_K:]
                part = lambda n: sums[n * CHUNK:(n + 1) * CHUNK]
                bcum, tail = part(0), part(1)
                ql += [q * jnp.exp(part(2 + 2 * lv)) for lv in range(n_lev - 1)] + [q, q]
                kl += [k * jnp.exp(part(3 + 2 * lv)) for lv in range(n_lev - 1)] + [k * jnp.exp(part(2 * n_lev)), k]
                qg_l.append(q * jnp.exp(bcum))
                kt_l.append(k * jnp.exp(tail))
                e_l.append(jnp.exp(bcum[last:last + 1, :]))
                v_l.append(v_ref[0, rows, vcols(hh)])
    scores = _bdot_nt(jnp.stack(ql), jnp.stack(kl))
    a_l = []
    for n in range(2 * hs * nc):
        m_ref = dirs[n // (hs * nc)][7]
        a = m_ref[0] * scores[n * (n_lev + 1)]
        for lv in range(1, n_lev + 1):
            a = a + m_ref[lv] * scores[n * (n_lev + 1) + lv]
        a_l.append(a)
    v = jnp.stack(v_l)
    x = _bdot_tn(v, jnp.stack(kt_l))
    s_l = [None] * (2 * hs * nc)
    for ch in range(2 * hs):
        state = s_ref[ch]
        for c in (range(nc - 1, -1, -1) if dirs[ch // hs][8] else range(nc)):
            n = ch * nc + c
            s_l[n] = state
            state = state * e_l[n] + x[n]
        s_ref[ch] = state
    o = _bdot(jnp.stack(a_l), v) + _bdot_nt(jnp.stack(qg_l), jnp.stack(s_l))
    for n in range(2 * hs * nc):
        ch, c = divmod(n, nc)
        (of_ref, ob_ref)[ch // hs][0, chunk_rows[c], vcols(ch % hs)] = o[n]


def _gla_scan(p, w2cat, b2cat):
    nb, t, _ = p.shape
    n_blocks = t // SCAN_BLOCK
    hs = GLA_H_STEP
    h = GLA_HEADS // hs
    low_block = (2 * GLA_K_DIM + 2 * GLA_V_DIM) // LANES
    fwd = lambda s: s
    bwd = lambda s: jnp.where(s == 0, 0, n_blocks - s)
    assert _GLA_LEVELS[-1] == 1
    n_sel = (1 + 2 * len(_GLA_LEVELS)) * CHUNK
    n_mask = len(_GLA_LEVELS) + 1

    def specs(order, d):
        return [pl.BlockSpec((1, SCAN_BLOCK, hs * GLA_HEAD_K), lambda b, hh, s: (b, order(s), hh)),
                pl.BlockSpec((1, SCAN_BLOCK, hs * GLA_HEAD_K), lambda b, hh, s: (b, order(s), h + hh)),
                pl.BlockSpec((1, SCAN_BLOCK, hs * GLA_HEAD_V), lambda b, hh, s: (b, order(s), h + hh)),
                pl.BlockSpec((1, SCAN_BLOCK, LANES), lambda b, hh, s: (b, order(s), low_block)),
                pl.BlockSpec((LANES, hs * GLA_HEAD_K), lambda b, hh, s: (0, d * h + hh)),
                pl.BlockSpec((1, hs * GLA_HEAD_K), lambda b, hh, s: (0, d * h + hh)),
                pl.BlockSpec((n_sel, CHUNK), lambda b, hh, s: (0, 0)),
                pl.BlockSpec((n_mask, CHUNK, CHUNK), lambda b, hh, s: (0, 0, 0))]

    out_spec = lambda order: pl.BlockSpec((1, SCAN_BLOCK, hs * GLA_HEAD_V), lambda b, hh, s: (b, order(s), hh))
    o_shape = jax.ShapeDtypeStruct((nb, t, GLA_V_DIM), F32)
    sel_f, mask_f = _gla_tables(False)
    sel_r, mask_r = _gla_tables(True)
    return pl.pallas_call(
        _gla_scan_kernel,
        grid=(nb, h, n_blocks),
        in_specs=specs(fwd, 0) + specs(bwd, 1),
        out_specs=[out_spec(fwd), out_spec(bwd)],
        out_shape=[o_shape, o_shape],
        scratch_shapes=[pltpu.VMEM((2 * hs, GLA_HEAD_V, GLA_HEAD_K), F32)],
        compiler_params=_params(("arbitrary", "arbitrary", "arbitrary")),
        name="gla_scan",
    )(p, p, p, p, w2cat, b2cat, sel_f, mask_f, p, p, p, p, w2cat, b2cat, sel_r, mask_r)


def _attn_prep_kernel(p_ref, qg_ref, kg_ref, cos_ref, sin_ref, q_ref, k_ref, v_ref):
    cos = cos_ref[...]
    sin = sin_ref[...]
    lane = lax.broadcasted_iota(jnp.int32, (1, ATTN_HEAD), 1)
    first = (lane % (ATTN_HEAD // 2)) < (ATTN_HEAD // 4)
    q_scale = ATTN_HEAD ** -0.5 * math.log2(math.e)

    def norm_rope(x, g):
        ms = jnp.mean(x * x, axis=-1, keepdims=True)
        y = x * lax.rsqrt(ms + NORM_EPS) * g
        partner = jnp.where(first, pltpu.roll(y, ATTN_HEAD - ATTN_HEAD // 4, 1), pltpu.roll(y, ATTN_HEAD // 4, 1))
        return y * cos + partner * sin

    qd = ATTN_Q_HEADS * ATTN_HEAD
    kd = ATTN_KV_HEADS * ATTN_HEAD
    q_pieces = [norm_rope(p_ref[0, :, h * ATTN_HEAD:(h + 1) * ATTN_HEAD], qg_ref[...]) * q_scale
                for h in range(ATTN_Q_HEADS)]
    q_ref[0] = jnp.concatenate(q_pieces, axis=-1).astype(BF16)
    k_pieces = [norm_rope(p_ref[0, :, qd + h * ATTN_HEAD:qd + (h + 1) * ATTN_HEAD], kg_ref[...])
                for h in range(ATTN_KV_HEADS)]
    k_ref[0] = jnp.concatenate(k_pieces, axis=-1).astype(BF16)
    v_ref[0] = p_ref[0, :, qd + kd:qd + 2 * kd].astype(BF16)


def _rope_tables(t, ctx_len):
    n_rows = (t - ctx_len) // GRID_W
    axis_dim = ATTN_HEAD // 2
    inv_freq = jnp.power(ROPE_THETA, -jnp.arange(0, axis_dim, 2, dtype=F32) / axis_dim)
    ar = jnp.arange(n_rows, dtype=F32)[:, None] * inv_freq
    ac = jnp.arange(GRID_W, dtype=F32)[:, None] * inv_freq
    on_rows = lambda x: jnp.broadcast_to(x[:, None, :], (n_rows, GRID_W, x.shape[-1]))
    on_cols = lambda x: jnp.broadcast_to(x[None, :, :], (n_rows, GRID_W, x.shape[-1]))
    cr, sr, cc, sc = on_rows(jnp.cos(ar)), on_rows(jnp.sin(ar)), on_cols(jnp.cos(ac)), on_cols(jnp.sin(ac))
    cos = jnp.concatenate([cr, cr, cc, cc], axis=-1).reshape(t - ctx_len, ATTN_HEAD)
    sin = jnp.concatenate([-sr, sr, -sc, sc], axis=-1).reshape(t - ctx_len, ATTN_HEAD)
    cos = jnp.concatenate([jnp.ones((ctx_len, ATTN_HEAD), F32), cos], axis=0)
    sin = jnp.concatenate([jnp.zeros((ctx_len, ATTN_HEAD), F32), sin], axis=0)
    return cos, sin


def _attn_prep(p, q_g, k_g, cos, sin):
    nb, t, n = p.shape
    tm = _row_tile(t, 640)
    qd = ATTN_Q_HEADS * ATTN_HEAD
    kd = ATTN_KV_HEADS * ATTN_HEAD
    row = lambda w: pl.BlockSpec((1, tm, w), lambda b, i: (b, i, 0))
    tab = pl.BlockSpec((tm, ATTN_HEAD), lambda b, i: (i, 0))
    vec = pl.BlockSpec((1, ATTN_HEAD), lambda b, i: (0, 0))
    return pl.pallas_call(
        _attn_prep_kernel,
        grid=(nb, t // tm),
        in_specs=[row(n), vec, vec, tab, tab],
        out_specs=[row(qd), row(kd), row(kd)],
        out_shape=[jax.ShapeDtypeStruct((nb, t, qd), BF16),
                   jax.ShapeDtypeStruct((nb, t, kd), BF16),
                   jax.ShapeDtypeStruct((nb, t, kd), BF16)],
        compiler_params=_params(("arbitrary", "arbitrary")),
        name="attn_prep",
    )(p, q_g.reshape(1, -1), k_g.reshape(1, -1), cos, sin)


def _flash_kernel(q_ref, k_ref, v_ref, o_ref, qs_ref, m_ref, l_ref, acc_ref, sa_ref, sb_ref, *, tq, tk, ctx_len, t_total):
    i = pl.program_id(2)
    g, hd = ATTN_GROUP, ATTN_HEAD
    for h in range(g):
        qs_ref[h * tq:(h + 1) * tq, :] = q_ref[0, :, h * hd:(h + 1) * hd]
    m_ref[...] = jnp.full_like(m_ref, -jnp.inf)
    l_ref[...] = jnp.zeros_like(l_ref)
    acc_ref[...] = jnp.zeros_like(acc_ref)

    def scores(start, width):
        kc = k_ref[0, pl.ds(start, width), :]
        return lax.dot_general(qs_ref[...], kc, (((1,), (1,)), ((), ())), preferred_element_type=F32)

    def update(s, start, width):
        vc = v_ref[0, pl.ds(start, width), :]
        m_old = m_ref[...]
        m_new = jnp.maximum(m_old, jnp.max(s, axis=-1, keepdims=True))
        alpha = jnp.exp2(m_old - m_new)
        p = jnp.exp2(s - jnp.tile(m_new, (1, width // LANES)))
        psum = p[:, 0:LANES]
        for n in range(1, width // LANES):
            psum = psum + p[:, n * LANES:(n + 1) * LANES]
        l_ref[...] = alpha * l_ref[...] + psum
        acc_ref[...] = alpha * acc_ref[...] + jnp.dot(p.astype(BF16), vc, preferred_element_type=F32)
        m_ref[...] = m_new

    @pl.when(i * tq < ctx_len)
    def _():
        update(scores(0, ctx_len), 0, ctx_len)

    n_kv = t_total // tk

    @pl.when(i * tq >= ctx_len)
    def _():
        sa_ref[...] = scores(0, tk)

        def pair(c):
            first = pl.multiple_of(2 * c * tk, tk)
            second = pl.multiple_of(first + tk, tk)
            third = pl.multiple_of(jnp.minimum(2 * c + 2, n_kv - 1) * tk, tk)
            sb_ref[...] = scores(second, tk)
            update(sa_ref[...], first, tk)
            sa_ref[...] = scores(third, tk)
            update(sb_ref[...], second, tk)

        n_pairs = n_kv // 2
        unroll = 4 if n_pairs % 4 == 1 else 2

        def body(c, carry):
            for u in range(unroll):
                pair(unroll * c + u)
            return carry
        lax.fori_loop(0, n_pairs // unroll, body, 0)
        for c in range(n_pairs - n_pairs % unroll, n_pairs):
            pair(c)

    out = acc_ref[...] / jnp.sum(l_ref[...], axis=-1, keepdims=True)
    o_ref[0] = jnp.concatenate([out[h * tq:(h + 1) * tq] for h in range(g)], axis=-1).astype(BF16)


def _flash_attention(q, k, v, ctx_len):
    nb, t, qd = q.shape
    tq = 256
    tk = _row_tile(t, 640, LANES)
    assert ctx_len == tq and t % tq == 0 and ctx_len % LANES == 0 and (t // tk) % 2 == 0
    gw = ATTN_GROUP * ATTN_HEAD
    return pl.pallas_call(
        functools.partial(_flash_kernel, tq=tq, tk=tk, ctx_len=ctx_len, t_total=t),
        grid=(nb, ATTN_KV_HEADS, t // tq),
        in_specs=[pl.BlockSpec((1, tq, gw), lambda b, kv, i: (b, i, kv)),
                  pl.BlockSpec((1, t, ATTN_HEAD), lambda b, kv, i: (b, 0, kv)),
                  pl.BlockSpec((1, t, ATTN_HEAD), lambda b, kv, i: (b, 0, kv))],
        out_specs=pl.BlockSpec((1, tq, gw), lambda b, kv, i: (b, i, kv)),
        out_shape=jax.ShapeDtypeStruct((nb, t, qd), BF16),
        scratch_shapes=[pltpu.VMEM((ATTN_GROUP * tq, ATTN_HEAD), BF16),
                        pltpu.VMEM((ATTN_GROUP * tq, LANES), F32),
                        pltpu.VMEM((ATTN_GROUP * tq, LANES), F32),
                        pltpu.VMEM((ATTN_GROUP * tq, ATTN_HEAD), F32),
                        pltpu.VMEM((ATTN_GROUP * tq, tk), F32),
                        pltpu.VMEM((ATTN_GROUP * tq, tk), F32)],
        compiler_params=_params(("arbitrary", "arbitrary", "arbitrary")),
        name="flash_attention",
    )(q, k, v)


def _pad_cols(w, n):
    return jnp.pad(w, ((0, 0), (0, n - w.shape[1])))


def _deltanet_layer(xc, mod, norm_g, w_in, conv_w, a_log, dt_bias, out_norm_g, w_out, ctx_len):
    nb, t, _ = xc.shape
    gate_col = DN_QKV_DIM + DN_V_DIM
    conv_w_t = jnp.pad(conv_w.T.astype(F32), ((0, 8 - SHORT_CONV), (0, 0)))
    qkv = _dn_qkv(xc, mod, norm_g, w_in[:, :DN_QKV_DIM].astype(BF16), conv_w_t, ctx_len)
    w_gate = _pad_cols(w_in[:, gate_col:][:, _dn_gate_lane_perm()], LANES)
    w_rest = jnp.concatenate([w_in[:, DN_QKV_DIM:gate_col], w_gate], axis=1).astype(BF16)
    p = _in_projection(xc, mod, norm_g, w_rest, MXU_CHUNK, ctx_len)
    gb = _dn_gates(p, DN_V_DIM // LANES, a_log, dt_bias)
    per_head = 4 * DN_GROUP
    g_rows = gb[..., :DN_K_HEADS * per_head].reshape(nb, t // CHUNK, CHUNK, DN_K_HEADS, per_head)
    g_rows = g_rows.transpose(0, 3, 1, 4, 2)
    o_f, o_b = _dn_scan(qkv, gb, g_rows)
    return _out_projection([o_f, o_b], p, 0, out_norm_g, DN_V_HEADS, DN_HEAD,
                           w_out.astype(BF16), xc, mod, ctx_len, 320)


def _gla_layer(xc, mod, norm_g, w_in, gate_w2, gate_b2, out_norm_g, w_out, ctx_len):
    n_pad = 2 * GLA_K_DIM + 2 * GLA_V_DIM + LANES
    p = _in_projection(xc, mod, norm_g, _pad_cols(w_in, n_pad).astype(BF16), MXU_CHUNK, ctx_len)
    r = GLA_GATE_RANK
    w2cat = jnp.zeros((LANES, 2 * GLA_K_DIM), F32)
    w2cat = w2cat.at[0:r, :GLA_K_DIM].set(gate_w2[0]).at[r:2 * r, GLA_K_DIM:].set(gate_w2[1]).astype(BF16)
    b2cat = gate_b2.reshape(1, 2 * GLA_K_DIM).astype(F32)
    o_f, o_b = _gla_scan(p, w2cat, b2cat)
    z_block = (2 * GLA_K_DIM + GLA_V_DIM) // GLA_V_DIM
    return _out_projection([o_f, o_b], p, z_block, out_norm_g, GLA_HEADS, GLA_HEAD_V,
                           w_out.astype(BF16), xc, mod, ctx_len, 640)


def _attention_layer(xc, mod, norm_g, w_in, q_g, k_g, w_out, rope, ctx_len):
    p = _in_projection(xc, mod, norm_g, w_in.astype(BF16), MXU_CHUNK, ctx_len)
    q, k, v = _attn_prep(p, q_g, k_g, *rope)
    o = _flash_attention(q, k, v, ctx_len)
    return _out_projection([o], None, 0, None, 0, 0, w_out.astype(BF16), xc, mod, ctx_len, 1280)


def kernel(x, c, ctx, c_ctx, ada_w, ada_b, norm_mix_g, norm_ffn_g, ffn_w1, ffn_w2, dn_w_in, dn_conv_w, dn_a_log, dn_dt_bias, dn_norm_g, dn_w_out, gla_w_in, gla_gate_w2, gla_gate_b2, gla_norm_g, gla_w_out, attn_w_in, attn_q_norm_g, attn_k_norm_g, attn_w_out):
    nb, seq, d = x.shape
    ctx_len = ctx.shape[1]
    depth = ada_w.shape[0]
    assert ctx_len == SCAN_BLOCK and seq % SCAN_BLOCK == 0 and nb < MOD_ROWS
    t = ctx_len + seq
    xc = jnp.concatenate([ctx, x], axis=1)
    cvec = jnp.zeros((MOD_ROWS, d), F32).at[:nb].set(c).at[nb].set(c_ctx)
    mods = _ada_vectors(cvec, ada_w, ada_b)
    rope = _rope_tables(t, ctx_len)
    for i in range(depth):
        mix, slot = i % 3, i // 3
        mod = mods[i]
        if mix == 0:
            xc = _deltanet_layer(xc, mod, norm_mix_g[i], dn_w_in[slot], dn_conv_w[slot], dn_a_log[slot],
                                 dn_dt_bias[slot], dn_norm_g[slot], dn_w_out[slot], ctx_len)
        elif mix == 1:
            xc = _gla_layer(xc, mod, norm_mix_g[i], gla_w_in[slot], gla_gate_w2[slot], gla_gate_b2[slot],
                            gla_norm_g[slot], gla_w_out[slot], ctx_len)
        else:
            xc = _attention_layer(xc, mod, norm_mix_g[i], attn_w_in[slot], attn_q_norm_g[slot],
                                  attn_k_norm_g[slot], attn_w_out[slot], rope, ctx_len)
        xc = _ffn(xc, mod, norm_ffn_g[i], ffn_w1[i].astype(BF16), ffn_w2[i].astype(BF16), ctx_len)
    return xc[:, ctx_len:, :]
```

```python
import functools
import math

import numpy as np
import jax
import jax.numpy as jnp
from jax import lax
from jax.experimental import pallas as pl
from jax.experimental.pallas import tpu as pltpu

F32 = jnp.float32
BF16 = jnp.bfloat16

NORM_EPS = 1e-6
GRID_W = 64
ROPE_THETA = 10000.0
SHORT_CONV = 5

DN_K_HEADS = 8
DN_V_HEADS = 16
DN_HEAD = 128
DN_GROUP = DN_V_HEADS // DN_K_HEADS
DN_K_DIM = DN_K_HEADS * DN_HEAD
DN_V_DIM = DN_V_HEADS * DN_HEAD
DN_QKV_DIM = 2 * DN_K_DIM + DN_V_DIM
DN_KH_STEP = 2

GLA_HEADS = 4
GLA_HEAD_K = 128
GLA_HEAD_V = 256
GLA_K_DIM = GLA_HEADS * GLA_HEAD_K
GLA_V_DIM = GLA_HEADS * GLA_HEAD_V
GLA_GATE_RANK = 16
GLA_GATE_NORMALIZER = 16.0
GLA_H_STEP = 2

ATTN_Q_HEADS = 8
ATTN_KV_HEADS = 2
ATTN_HEAD = 128
ATTN_GROUP = ATTN_Q_HEADS // ATTN_KV_HEADS

CHUNK = 64
SCAN_BLOCK = 256
LANES = 128
MXU_CHUNK = 1024
MOD_ROWS = 8
VMEM_LIMIT = 56 * 1024 * 1024


def _params(semantics, vmem=VMEM_LIMIT):
    return pltpu.CompilerParams(dimension_semantics=semantics, vmem_limit_bytes=vmem)


def _sigmoid(x):
    return 1.0 / (1.0 + jnp.exp(-x))


def _softplus(x):
    return jnp.maximum(x, 0.0) + jnp.log(1.0 + jnp.exp(-jnp.abs(x)))


def _split3(x):
    hi = x.astype(BF16)
    r1 = x - hi.astype(F32)
    mid = r1.astype(BF16)
    lo = (r1 - mid.astype(F32)).astype(BF16)
    return hi, mid, lo


def _dot(a, b):
    return jnp.dot(a.astype(BF16), b.astype(BF16), preferred_element_type=F32)


def _dot_nt(a, b):
    return lax.dot_general(a.astype(BF16), b.astype(BF16), (((1,), (1,)), ((), ())),
                           preferred_element_type=F32)


def _dot_tn(a, b):
    return lax.dot_general(a.astype(BF16), b.astype(BF16), (((0,), (0,)), ((), ())),
                           preferred_element_type=F32)


def _dot_sel(p_bf16, x):
    hi, mid, lo = _split3(x)
    d = lambda y: jnp.dot(p_bf16, y, preferred_element_type=F32)
    return d(hi) + d(mid) + d(lo)


def _row_tile(total, target, multiple=8):
    best = None
    for t in range(multiple, min(total, target) + 1, multiple):
        if total % t == 0:
            best = t
    assert best is not None, (total, target, multiple)
    return best


def _mod_norm(x, g, mod_ref, b, row0, ctx_len, nb, shift_idx, scale_idx):
    d = x.shape[-1]
    ms = jnp.mean(x * x, axis=-1, keepdims=True)
    y = x * lax.rsqrt(ms + NORM_EPS) * g
    rows = row0 + lax.broadcasted_iota(jnp.int32, (x.shape[0], 1), 0)
    is_ctx = rows < ctx_len

    def pick(idx):
        vx = mod_ref[pl.ds(b, 1), idx * d:(idx + 1) * d]
        vc = mod_ref[nb:nb + 1, idx * d:(idx + 1) * d]
        return jnp.where(is_ctx, vc, vx)

    return y * (1.0 + pick(scale_idx)) + pick(shift_idx), is_ctx, pick


def _ada_kernel(c_ref, w_ref, b_ref, o_ref):
    c = c_ref[...]
    s = c * _sigmoid(c)
    o_ref[0] = jnp.dot(s, w_ref[0], preferred_element_type=F32,
                       precision=lax.Precision.HIGHEST) + b_ref[0]


def _ada_vectors(cvec, ada_w, ada_b):
    depth, d, n = ada_w.shape
    tn = _row_tile(n, 1536, LANES)
    return pl.pallas_call(
        _ada_kernel,
        grid=(depth, n // tn),
        in_specs=[pl.BlockSpec((MOD_ROWS, d), lambda l, j: (0, 0)),
                  pl.BlockSpec((1, d, tn), lambda l, j: (l, 0, j)),
                  pl.BlockSpec((1, 1, tn), lambda l, j: (l, 0, j))],
        out_specs=pl.BlockSpec((1, MOD_ROWS, tn), lambda l, j: (l, 0, j)),
        out_shape=jax.ShapeDtypeStruct((depth, MOD_ROWS, n), F32),
        compiler_params=_params(("arbitrary", "arbitrary")),
        name="ada_vectors",
    )(cvec, ada_w, ada_b.reshape(depth, 1, n))


def _inproj_kernel(x_ref, mod_ref, g_ref, w_ref, *o_refs, tm, tn, ctx_len, nb):
    b = pl.program_id(0)
    i = pl.program_id(1)
    h, _, _ = _mod_norm(x_ref[0], g_ref[...], mod_ref, b, i * tm, ctx_len, nb, 0, 1)
    h = h.astype(BF16)
    first = 0
    for o_ref in o_refs:
        n = o_ref.shape[2]
        for start in range(0, n, tn):
            stop = min(start + tn, n)
            res = jnp.dot(h, w_ref[:, first + start:first + stop], preferred_element_type=F32)
            o_ref[0, :, start:stop] = res.astype(o_ref.dtype)
        first += n


def _in_projection(xc, mod, g, w_bf16, tn, ctx_len, bf16_cols=0):
    nb, t, d = xc.shape
    n = w_bf16.shape[1]
    tm = _row_tile(t, 640)
    assert bf16_cols % tn == 0
    widths = [(bf16_cols, BF16)] * (bf16_cols > 0) + [(n - bf16_cols, F32)]
    out = pl.pallas_call(
        functools.partial(_inproj_kernel, tm=tm, tn=tn, ctx_len=ctx_len, nb=nb),
        grid=(nb, t // tm),
        in_specs=[pl.BlockSpec((1, tm, d), lambda b, i: (b, i, 0)),
                  pl.BlockSpec(mod.shape, lambda b, i: (0, 0)),
                  pl.BlockSpec((1, d), lambda b, i: (0, 0)),
                  pl.BlockSpec((d, n), lambda b, i: (0, 0), pipeline_mode=pl.Buffered(1))],
        out_specs=[pl.BlockSpec((1, tm, w), lambda b, i: (b, i, 0)) for w, _ in widths],
        out_shape=[jax.ShapeDtypeStruct((nb, t, w), dt) for w, dt in widths],
        compiler_params=_params(("arbitrary", "arbitrary")),
        name="in_projection",
    )(xc, mod, g.reshape(1, d), w_bf16)
    return out if bf16_cols else out[0]


def _outproj_kernel(*refs, n_o, gated, heads, head_dim, tm, ctx_len, nb):
    o_refs = refs[:n_o]
    pos = n_o
    if gated:
        z_ref, ng_ref = refs[pos], refs[pos + 1]
        pos += 2
    w_ref, x_ref, mod_ref, out_ref = refs[pos:pos + 4]
    b = pl.program_id(0)
    i = pl.program_id(1)
    if gated:
        pieces = []
        for h in range(heads):
            sl = slice(h * head_dim, (h + 1) * head_dim)
            o = o_refs[0][0, :, sl].astype(F32)
            for r in o_refs[1:]:
                o = o + r[0, :, sl].astype(F32)
            ms = jnp.mean(o * o, axis=-1, keepdims=True)
            o = o * lax.rsqrt(ms + NORM_EPS) * ng_ref[...]
            z = z_ref[0, :, sl].astype(F32)
            pieces.append((o * (z * _sigmoid(z))).astype(BF16))
        lhs = jnp.concatenate(pieces, axis=-1)
    else:
        lhs = o_refs[0][0]
    y = jnp.dot(lhs, w_ref[...], preferred_element_type=F32)
    d = y.shape[-1]
    rows = i * tm + lax.broadcasted_iota(jnp.int32, (tm, 1), 0)
    gate = jnp.where(rows < ctx_len, mod_ref[nb:nb + 1, 2 * d:3 * d], mod_ref[pl.ds(b, 1), 2 * d:3 * d])
    out_ref[0] = x_ref[0] + gate * y


def _out_projection(o_list, z_src, z_col_block, norm_g, heads, head_dim, w_bf16, xc, mod, ctx_len, tm_target):
    nb, t, d = xc.shape
    dv = w_bf16.shape[0]
    tm = _row_tile(t, tm_target)
    gated = z_src is not None
    row_spec = lambda width, col: pl.BlockSpec((1, tm, width), lambda b, i: (b, i, col))
    in_specs = [row_spec(dv, 0) for _ in o_list]
    args = list(o_list)
    if gated:
        in_specs += [row_spec(dv, z_col_block), pl.BlockSpec((1, head_dim), lambda b, i: (0, 0))]
        args += [z_src, norm_g.reshape(1, head_dim)]
    in_specs += [pl.BlockSpec((dv, d), lambda b, i: (0, 0)), row_spec(d, 0),
                 pl.BlockSpec(mod.shape, lambda b, i: (0, 0))]
    args += [w_bf16, xc, mod]
    return pl.pallas_call(
        functools.partial(_outproj_kernel, n_o=len(o_list), gated=gated, heads=heads, head_dim=head_dim,
                          tm=tm, ctx_len=ctx_len, nb=nb),
        grid=(nb, t // tm),
        in_specs=in_specs,
        out_specs=row_spec(d, 0),
        out_shape=jax.ShapeDtypeStruct((nb, t, d), F32),
        compiler_params=_params(("arbitrary", "arbitrary")),
        name="out_projection",
    )(*args)


def _ffn_kernel(x_ref, mod_ref, g_ref, w1_ref, w2_ref, o_ref, acc_ref, *, tm, fk, ctx_len, nb):
    b = pl.program_id(0)
    i = pl.program_id(1)
    x = x_ref[0]
    h, _, pick = _mod_norm(x, g_ref[...], mod_ref, b, i * tm, ctx_len, nb, 3, 4)
    h = h.astype(BF16)
    d_ff = w1_ref.shape[1]
    for k in range(d_ff // fk):
        u = jnp.dot(h, w1_ref[:, k * fk:(k + 1) * fk], preferred_element_type=F32)
        u = jnp.maximum(u, 0.0)
        u = (u * u).astype(BF16)
        contrib = jnp.dot(u, w2_ref[k * fk:(k + 1) * fk, :], preferred_element_type=F32)
        if k == 0:
            acc_ref[...] = contrib
        else:
            acc_ref[...] += contrib
    o_ref[0] = x + pick(5) * acc_ref[...]


def _ffn(xc, mod, g, w1_bf16, w2_bf16, ctx_len):
    nb, t, d = xc.shape
    d_ff = w1_bf16.shape[1]
    tm = _row_tile(t, 640)
    return pl.pallas_call(
        functools.partial(_ffn_kernel, tm=tm, fk=512, ctx_len=ctx_len, nb=nb),
        grid=(nb, t // tm),
        in_specs=[pl.BlockSpec((1, tm, d), lambda b, i: (b, i, 0)),
                  pl.BlockSpec(mod.shape, lambda b, i: (0, 0)),
                  pl.BlockSpec((1, d), lambda b, i: (0, 0)),
                  pl.BlockSpec((d, d_ff), lambda b, i: (0, 0)),
                  pl.BlockSpec((d_ff, d), lambda b, i: (0, 0))],
        out_specs=pl.BlockSpec((1, tm, d), lambda b, i: (b, i, 0)),
        out_shape=jax.ShapeDtypeStruct((nb, t, d), F32),
        scratch_shapes=[pltpu.VMEM((tm, d), F32)],
        compiler_params=_params(("arbitrary", "arbitrary")),
        name="ffn",
    )(xc, mod, g.reshape(1, d), w1_bf16, w2_bf16)


HALO = 8


def _dn_qkv_kernel(x_ref, xp_ref, xn_ref, mod_ref, g_ref, w_ref, cw_ref, o_ref, *,
                   tm, t_total, ctx_len, nb, fix_tile, fix_row):
    b = pl.program_id(0)
    i = pl.program_id(1)
    row0 = i * tm
    n_ext = tm + 2 * HALO
    pad = SHORT_CONV // 2
    x_ext = jnp.concatenate([xp_ref[0], x_ref[0], xn_ref[0]], axis=0)
    h, _, _ = _mod_norm(x_ext, g_ref[...], mod_ref, b, row0 - HALO, ctx_len, nb, 0, 1)
    t = row0 - HALO + lax.broadcasted_iota(jnp.int32, (n_ext, 1), 0)
    nearest = jnp.clip(t, row0, row0 + tm - 1)
    valid = (jnp.where(t >= 0, 1, 0) * jnp.where(t < t_total, 1, 0)
             * jnp.where(jnp.where(t >= ctx_len, 1, 0) == jnp.where(nearest >= ctx_len, 1, 0), 1, 0))
    h = jnp.where(valid > 0, h, 0.0).astype(BF16)
    if fix_row is not None:
        rr = lax.broadcasted_iota(jnp.int32, (2 * HALO, 1), 0)
        crosses = {}
        for d in range(-pad, pad + 1):
            if d:
                inside = jnp.where(rr + d >= 0, 1, 0) * jnp.where(rr + d < 2 * HALO, 1, 0)
                other = jnp.where(jnp.where(rr < HALO, 1, 0) != jnp.where(rr + d < HALO, 1, 0), 1, 0)
                crosses[d] = jnp.where(i == fix_tile, inside * other, 0) > 0
    n = w_ref.shape[1]
    for start in range(0, n, MXU_CHUNK):
        res = jnp.dot(h, w_ref[:, start:start + MXU_CHUNK], preferred_element_type=F32)
        for hd in range(MXU_CHUNK // DN_HEAD):
            col0 = start + hd * DN_HEAD
            cols = slice(col0, col0 + DN_HEAD)
            r = res[:, hd * DN_HEAD:(hd + 1) * DN_HEAD]
            acc = None
            for tap in range(SHORT_CONV):
                d = tap - pad
                xs = (r if d == 0 else pltpu.roll(r, (-d) % n_ext, 0))[HALO:HALO + tm]
                term = xs * cw_ref[tap:tap + 1, cols]
                acc = term if acc is None else acc + term
            if fix_row is not None:
                slab = r[fix_row:fix_row + 2 * HALO]
                wrong = None
                for d, mask in crosses.items():
                    term = jnp.where(mask, pltpu.roll(slab, (-d) % (2 * HALO), 0), 0.0) * cw_ref[d + pad:d + pad + 1, cols]
                    wrong = term if wrong is None else wrong + term
                acc = jnp.concatenate([acc[:fix_row - HALO], acc[fix_row - HALO:fix_row + HALO] - wrong,
                                       acc[fix_row + HALO:]], axis=0)
            half = 0.5 * acc
            y = half + half * jnp.tanh(half)
            if col0 < 2 * DN_K_DIM:
                ss = jnp.sum(y * y, axis=-1, keepdims=True)
                y = y * (lax.rsqrt(ss + NORM_EPS) * (DN_HEAD ** -0.5 if col0 < DN_K_DIM else 1.0))
            o_ref[0, :, cols] = y


def _dn_qkv(xc, mod, g, w_qkv_bf16, conv_w_t, ctx_len):
    nb, t, d = xc.shape
    n = w_qkv_bf16.shape[1]
    tm = _row_tile(t, 640)
    hb = tm // HALO
    last = t // HALO - 1
    fix_tile, fix_row = (ctx_len // tm, ctx_len % tm) if ctx_len % tm else (None, None)
    assert fix_row is None or (fix_row % HALO == 0 and HALO <= fix_row <= tm - HALO)
    return pl.pallas_call(
        functools.partial(_dn_qkv_kernel, tm=tm, t_total=t, ctx_len=ctx_len, nb=nb,
                          fix_tile=fix_tile, fix_row=fix_row),
        grid=(nb, t // tm),
        in_specs=[pl.BlockSpec((1, tm, d), lambda b, i: (b, i, 0)),
                  pl.BlockSpec((1, HALO, d), lambda b, i: (b, jnp.maximum(i * hb - 1, 0), 0)),
                  pl.BlockSpec((1, HALO, d), lambda b, i: (b, jnp.minimum((i + 1) * hb, last), 0)),
                  pl.BlockSpec(mod.shape, lambda b, i: (0, 0)),
                  pl.BlockSpec((1, d), lambda b, i: (0, 0)),
                  pl.BlockSpec((d, n), lambda b, i: (0, 0), pipeline_mode=pl.Buffered(1)),
                  pl.BlockSpec((8, n), lambda b, i: (0, 0))],
        out_specs=pl.BlockSpec((1, tm, n), lambda b, i: (b, i, 0)),
        out_shape=jax.ShapeDtypeStruct((nb, t, n), F32),
        compiler_params=_params(("arbitrary", "arbitrary")),
        name="dn_qkv",
    )(xc, xc, xc, mod, g.reshape(1, d), w_qkv_bf16, conv_w_t)


def _dn_gate_kernel(ab_ref, par_ref, o_ref, *, tm):
    x = ab_ref[0]
    g = -jnp.exp(par_ref[0:1, :]) * _softplus(x + par_ref[1:2, :])
    beta = _sigmoid(x)
    lane = lax.broadcasted_iota(jnp.int32, (1, LANES), 1)
    used = lane < 4 * DN_V_HEADS
    is_beta = (lane & (2 * DN_GROUP)) != 0
    is_reverse = (lane & DN_GROUP) != 0
    r = lax.broadcasted_iota(jnp.int32, (CHUNK, CHUNK), 0)
    c = lax.broadcasted_iota(jnp.int32, (CHUNK, CHUNK), 1)
    lower = jnp.where(c <= r, 1.0, 0.0).astype(BF16)
    upper = jnp.where(c >= r, 1.0, 0.0).astype(BF16)
    for k in range(tm // CHUNK):
        rows = slice(k * CHUNK, (k + 1) * CHUNK)
        gk = g[rows]
        fwd = _dot_sel(lower, gk)
        bwd = _dot_sel(upper, gk)
        o_ref[0, rows, :] = jnp.where(used, jnp.where(is_beta, beta[rows], jnp.where(is_reverse, bwd, fwd)), 0.0)


def _dn_gate_lane_perm():
    perm = np.zeros(4 * DN_V_HEADS, np.int32)
    for kh in range(DN_K_HEADS):
        for kind in range(2):
            for d in range(2):
                for j in range(DN_GROUP):
                    lane = ((kh * 2 + kind) * 2 + d) * DN_GROUP + j
                    perm[lane] = (kind * 2 + d) * DN_V_HEADS + kh * DN_GROUP + j
    return perm


def _dn_gates(p, ab_col_block, a_log, dt_bias):
    nb, t, _ = p.shape
    tm = _row_tile(t, 1280, CHUNK)
    perm = _dn_gate_lane_perm()
    on_lanes = lambda v: jnp.concatenate([v.reshape(-1).astype(F32), jnp.zeros(2 * DN_V_HEADS, F32)])[perm]
    par = jnp.zeros((8, LANES), F32)
    par = par.at[0, :4 * DN_V_HEADS].set(on_lanes(a_log))
    par = par.at[1, :4 * DN_V_HEADS].set(on_lanes(dt_bias))
    return pl.pallas_call(
        functools.partial(_dn_gate_kernel, tm=tm),
        grid=(nb, t // tm),
        in_specs=[pl.BlockSpec((1, tm, LANES), lambda b, i: (b, i, ab_col_block)),
                  pl.BlockSpec((8, LANES), lambda b, i: (0, 0))],
        out_specs=pl.BlockSpec((1, tm, LANES), lambda b, i: (b, i, 0)),
        out_shape=jax.ShapeDtypeStruct((nb, t, LANES), F32),
        compiler_params=_params(("arbitrary", "arbitrary")),
        name="dn_gates",
    )(p, par)


def _bdot(a, b):
    return jnp.einsum("nij,njk->nik", a.astype(BF16), b.astype(BF16), preferred_element_type=F32)


def _bdot_nt(a, b):
    return jnp.einsum("nid,njd->nij", a.astype(BF16), b.astype(BF16), preferred_element_type=F32)


def _bdot_tn(a, b):
    return jnp.einsum("nci,ncj->nij", a.astype(BF16), b.astype(BF16), preferred_element_type=F32)


N_LEVELS = int(math.log2(CHUNK))


def _unit_triangular_inverse(parts, eye):
    coupling = lambda lv: jnp.concatenate([jnp.where(masks[lv], m, 0.0) for m, masks in parts], axis=0)
    x = eye - coupling(0)
    for lv in range(1, N_LEVELS):
        x = x - _bdot(x, _bdot(coupling(lv), x))
    return x


def _coupling_masks(ri, ci, reverse):
    ti, tj = (CHUNK - 1 - ri, CHUNK - 1 - ci) if reverse else (ri, ci)
    masks = []
    for lv in range(N_LEVELS):
        bi, bj = lax.shift_right_logical(ti, lv), lax.shift_right_logical(tj, lv)
        masks.append(jnp.where((bi & 1) == 1, bi - 1, -1) == bj)
    return masks


def _dn_scan_kernel(qf_ref, kf_ref, vf_ref, gf_ref, gtf_ref,
                    qr_ref, kr_ref, vr_ref, gr_ref, gtr_ref,
                    of_ref, ob_ref, s_ref):
    @pl.when(pl.program_id(2) == 0)
    def _():
        s_ref[...] = jnp.zeros_like(s_ref)

    ri = lax.broadcasted_iota(jnp.int32, (CHUNK, CHUNK), 0)
    ci = lax.broadcasted_iota(jnp.int32, (CHUNK, CHUNK), 1)
    eye = jnp.where(ri == ci, 1.0, 0.0)
    nc = SCAN_BLOCK // CHUNK
    chunk_rows = [slice(c * CHUNK, (c + 1) * CHUNK) for c in range(nc)]
    dirs = ((qf_ref, kf_ref, vf_ref, gf_ref, gtf_ref, False),
            (qr_ref, kr_ref, vr_ref, gr_ref, gtr_ref, True))
    m_parts, rhs_l, a_l, qg_l, kt_l, egl_l = [], [], [], [], [], []
    head_cols = lambda n: slice(n * DN_HEAD, (n + 1) * DN_HEAD)
    for d, (q_ref, k_ref, v_ref, g_ref, gt_ref, reverse) in enumerate(dirs):
        incl = (ci >= ri) if reverse else (ci <= ri)
        strict = (ci > ri) if reverse else (ci < ri)
        last = 0 if reverse else CHUNK - 1
        m_l = []
        per_head = 4 * DN_GROUP
        first_lane = pl.program_id(1) * (DN_KH_STEP * per_head)
        g_all = pltpu.roll(g_ref[0], jnp.where(first_lane == 0, 0, LANES - first_lane), 1)
        for hh in range(DN_KH_STEP):
            q = jnp.stack([q_ref[0, r, head_cols(hh)] for r in chunk_rows])
            k = jnp.stack([k_ref[0, r, head_cols(hh)] for r in chunk_rows])
            qk_kk = _bdot_nt(jnp.concatenate([q, k], axis=1), k)
            qk, kk = qk_kk[:, :CHUNK], qk_kk[:, CHUNK:]
            for j in range(DN_GROUP):
                col = 2 * d + j
                lane = hh * per_head + col
                gc = jnp.stack([g_all[r, lane:lane + 1] for r in chunk_rows])
                bc = jnp.stack([g_all[r, lane + 2 * DN_GROUP:lane + 2 * DN_GROUP + 1] for r in chunk_rows])
                gr = jnp.stack([gt_ref[0, hh, c, col:col + 1, :] for c in range(nc)])
                v = jnp.stack([v_ref[0, r, head_cols(hh * DN_GROUP + j)] for r in chunk_rows])
                decay = jnp.where(incl, jnp.exp(jnp.where(incl, gc - gr, 0.0)), 0.0)
                eg = jnp.exp(gc)
                gl = gc[:, last:last + 1, :]
                m_l.append(jnp.where(strict, bc * kk * decay, 0.0))
                rhs_l.append(jnp.concatenate([k * (bc * eg), v * bc], axis=-1))
                a_l.append(qk * decay)
                qg_l.append(q * eg)
                kt_l.append(k * jnp.exp(gl - gc))
                egl_l.append(jnp.exp(gl))
        m_parts.append((jnp.concatenate(m_l, axis=0), _coupling_masks(ri, ci, reverse)))
    cat = lambda xs: jnp.concatenate(xs, axis=0)
    a, kt = cat(a_l), cat(kt_l)
    wu = _bdot(_unit_triangular_inverse(m_parts, eye), cat(rhs_l))
    kb = _bdot_tn(kt, wu)
    qo = _bdot(a, wu)
    qeff = cat(qg_l) - qo[:, :, :DN_HEAD]
    egl = cat(egl_l)
    state = s_ref[...]
    o_refs = (of_ref, ob_ref)
    vh_step = DN_KH_STEP * DN_GROUP
    for step in range(nc):
        chunk_of = [step if d == 0 else nc - 1 - step for d in range(2) for _ in range(vh_step)]
        idx = [ch * nc + c for ch, c in enumerate(chunk_of)]
        pick = lambda x: jnp.stack([x[n] for n in idx])
        kb_s = pick(kb)
        qs_ks = _bdot(jnp.concatenate([pick(qeff), kb_s[:, :, :DN_HEAD]], axis=1), state)
        o = qs_ks[:, :CHUNK] + pick(qo)[:, :, DN_HEAD:]
        state = pick(egl) * state + kb_s[:, :, DN_HEAD:] - qs_ks[:, CHUNK:]
        for ch, c in enumerate(chunk_of):
            o_refs[ch // vh_step][0, chunk_rows[c], head_cols(ch % vh_step)] = o[ch].astype(BF16)
    s_ref[...] = state


def _dn_scan(qkv, gates, g_rows):
    nb, t, _ = qkv.shape
    n_blocks = t // SCAN_BLOCK
    n_chunks = SCAN_BLOCK // CHUNK
    ks = DN_KH_STEP
    groups = DN_K_HEADS // ks
    fwd = lambda s: s
    bwd = lambda s: jnp.where(s == 0, 0, n_blocks - s)

    def specs(order):
        return [pl.BlockSpec((1, SCAN_BLOCK, ks * DN_HEAD), lambda b, h, s: (b, order(s), h)),
                pl.BlockSpec((1, SCAN_BLOCK, ks * DN_HEAD), lambda b, h, s: (b, order(s), groups + h)),
                pl.BlockSpec((1, SCAN_BLOCK, ks * DN_GROUP * DN_HEAD), lambda b, h, s: (b, order(s), groups + h)),
                pl.BlockSpec((1, SCAN_BLOCK, LANES), lambda b, h, s: (b, order(s), 0)),
                pl.BlockSpec((1, ks, n_chunks, 8, CHUNK), lambda b, h, s: (b, h, order(s), 0, 0))]

    out_spec = lambda order: pl.BlockSpec((1, SCAN_BLOCK, ks * DN_GROUP * DN_HEAD), lambda b, h, s: (b, order(s), h))
    o_shape = jax.ShapeDtypeStruct((nb, t, DN_V_DIM), BF16)
    return pl.pallas_call(
        _dn_scan_kernel,
        grid=(nb, groups, n_blocks),
        in_specs=specs(fwd) + specs(bwd),
        out_specs=[out_spec(fwd), out_spec(bwd)],
        out_shape=[o_shape, o_shape],
        scratch_shapes=[pltpu.VMEM((2 * ks * DN_GROUP, DN_HEAD, DN_HEAD), F32)],
        compiler_params=_params(("arbitrary", "arbitrary", "arbitrary")),
        name="dn_scan",
    )(qkv, qkv, qkv, gates, g_rows, qkv, qkv, qkv, gates, g_rows)


_GLA_LEVELS = (32, 16, 8, 4, 2, 1)


def _gla_tables(reverse):
    idx = np.arange(CHUNK)
    tau = (CHUNK - 1 - idx) if reverse else idx
    ti, tk = tau[:, None], tau[None, :]
    groups = [tk <= ti, tk > ti]
    masks = []
    for s in _GLA_LEVELS:
        bi, bk = ti // s, tk // s
        if s > 1:
            groups.append((bk == bi) & (tk <= ti) & (tk > bi * s))
        groups.append(((bk == bi) & (tk > ti)) | (tk == (bi + 1) * s))
        masks.append((bi % 2 == 1) & (bk == bi - 1))
    masks.append(ti == tk)
    sel = np.concatenate(groups, axis=0).astype(np.float32)
    return jnp.asarray(sel, BF16), jnp.asarray(np.stack(masks).astype(np.float32))


def _gla_scan_kernel(qf_ref, kf_ref, vf_ref, lf_ref, wf_ref, bf_ref, self_ref, mf_ref,
                     qr_ref, kr_ref, vr_ref, lr_ref, wr_ref, br_ref, selr_ref, mr_ref,
                     of_ref, ob_ref, s_ref):
    @pl.when(pl.program_id(2) == 0)
    def _():
        s_ref[...] = jnp.zeros_like(s_ref)

    nc = SCAN_BLOCK // CHUNK
    n_lev = len(_GLA_LEVELS)
    chunk_rows = [slice(c * CHUNK, (c + 1) * CHUNK) for c in range(nc)]
    dirs = ((qf_ref, kf_ref, vf_ref, lf_ref, wf_ref, bf_ref, self_ref, mf_ref, False),
            (qr_ref, kr_ref, vr_ref, lr_ref, wr_ref, br_ref, selr_ref, mr_ref, True))
    hs = GLA_H_STEP
    kcols = lambda hh: slice(hh * GLA_HEAD_K, (hh + 1) * GLA_HEAD_K)
    vcols = lambda hh: slice(hh * GLA_HEAD_V, (hh + 1) * GLA_HEAD_V)
    ql, kl, qg_l, kt_l, e_l, v_l = [], [], [], [], [], []
    for q_ref, k_ref, v_ref, l_ref, w_ref, b_ref, sel_ref, m_ref, reverse in dirs:
        last = 0 if reverse else CHUNK - 1
        logits = _dot(l_ref[0], w_ref[...]) + b_ref[...]
        gk_all = (jnp.minimum(logits, 0.0) - jnp.log(1.0 + jnp.exp(-jnp.abs(logits)))) * (1.0 / GLA_GATE_NORMALIZER)
        for hh in range(hs):
            for rows in chunk_rows:
                q = q_ref[0, rows, kcols(hh)] * (GLA_HEAD_K ** -0.5)
                k = k_ref[0, rows, kcols(hh)]
                gk = gk_all[rows, kcols(hh)]
                hi = gk.astype(BF16)
                mid = (gk - hi.astype(F32)).astype(BF16)
                both = jnp.dot(sel_ref[...], jnp.concatenate([hi, mid], axis=-1), preferred_element_type=F32)
                sums = both[:, :GLA_HEAD_K] + both[:, GLA_HEAD_K:]
                part = lambda n: sums[n * CHUNK:(n + 1) * CHUNK]
                bcum, tail = part(0), part(1)
                ql += [q * jnp.exp(part(2 + 2 * lv)) for lv in range(n_lev - 1)] + [q, q]
                kl += [k * jnp.exp(part(3 + 2 * lv)) for lv in range(n_lev - 1)] + [k * jnp.exp(part(2 * n_lev)), k]
                qg_l.append(q * jnp.exp(bcum))
                kt_l.append(k * jnp.exp(tail))
                e_l.append(jnp.exp(bcum[last:last + 1, :]))
                v_l.append(v_ref[0, rows, vcols(hh)])
    scores = _bdot_nt(jnp.stack(ql), jnp.stack(kl))
    a_l = []
    for n in range(2 * hs * nc):
        m_ref = dirs[n // (hs * nc)][7]
        a = m_ref[0] * scores[n * (n_lev + 1)]
        for lv in range(1, n_lev + 1):
            a = a + m_ref[lv] * scores[n * (n_lev + 1) + lv]
        a_l.append(a)
    v = jnp.stack(v_l)
    x = _bdot_tn(v, jnp.stack(kt_l))
    s_l = [None] * (2 * hs * nc)
    for ch in range(2 * hs):
        state = s_ref[ch]
        for c in (range(nc - 1, -1, -1) if dirs[ch // hs][8] else range(nc)):
            n = ch * nc + c
            s_l[n] = state
            state = state * e_l[n] + x[n]
        s_ref[ch] = state
    o = _bdot(jnp.stack(a_l), v) + _bdot_nt(jnp.stack(qg_l), jnp.stack(s_l))
    for n in range(2 * hs * nc):
        ch, c = divmod(n, nc)
        (of_ref, ob_ref)[ch // hs][0, chunk_rows[c], vcols(ch % hs)] = o[n].astype(BF16)


def _gla_scan(p, w2cat, b2cat):
    nb, t, _ = p.shape
    n_blocks = t // SCAN_BLOCK
    hs = GLA_H_STEP
    h = GLA_HEADS // hs
    low_block = (2 * GLA_K_DIM + 2 * GLA_V_DIM) // LANES
    fwd = lambda s: s
    bwd = lambda s: jnp.where(s == 0, 0, n_blocks - s)
    assert _GLA_LEVELS[-1] == 1
    n_sel = (1 + 2 * len(_GLA_LEVELS)) * CHUNK
    n_mask = len(_GLA_LEVELS) + 1

    def specs(order, d):
        return [pl.BlockSpec((1, SCAN_BLOCK, hs * GLA_HEAD_K), lambda b, hh, s: (b, order(s), hh)),
                pl.BlockSpec((1, SCAN_BLOCK, hs * GLA_HEAD_K), lambda b, hh, s: (b, order(s), h + hh)),
                pl.BlockSpec((1, SCAN_BLOCK, hs * GLA_HEAD_V), lambda b, hh, s: (b, order(s), h + hh)),
                pl.BlockSpec((1, SCAN_BLOCK, LANES), lambda b, hh, s: (b, order(s), low_block)),
                pl.BlockSpec((LANES, hs * GLA_HEAD_K), lambda b, hh, s: (0, d * h + hh)),
                pl.BlockSpec((1, hs * GLA_HEAD_K), lambda b, hh, s: (0, d * h + hh)),
                pl.BlockSpec((n_sel, CHUNK), lambda b, hh, s: (0, 0)),
                pl.BlockSpec((n_mask, CHUNK, CHUNK), lambda b, hh, s: (0, 0, 0))]

    out_spec = lambda order: pl.BlockSpec((1, SCAN_BLOCK, hs * GLA_HEAD_V), lambda b, hh, s: (b, order(s), hh))
    o_shape = jax.ShapeDtypeStruct((nb, t, GLA_V_DIM), BF16)
    sel_f, mask_f = _gla_tables(False)
    sel_r, mask_r = _gla_tables(True)
    return pl.pallas_call(
        _gla_scan_kernel,
        grid=(nb, h, n_blocks),
        in_specs=specs(fwd, 0) + specs(bwd, 1),
        out_specs=[out_spec(fwd), out_spec(bwd)],
        out_shape=[o_shape, o_shape],
        scratch_shapes=[pltpu.VMEM((2 * hs, GLA_HEAD_V, GLA_HEAD_K), F32)],
        compiler_params=_params(("arbitrary", "arbitrary", "arbitrary")),
        name="gla_scan",
    )(p, p, p, p, w2cat, b2cat, sel_f, mask_f, p, p, p, p, w2cat, b2cat, sel_r, mask_r)


def _attn_prep_kernel(p_ref, qg_ref, kg_ref, cos_ref, sin_ref, q_ref, k_ref, v_ref):
    cos = cos_ref[...]
    sin = sin_ref[...]
    lane = lax.broadcasted_iota(jnp.int32, (1, ATTN_HEAD), 1)
    first = (lane % (ATTN_HEAD // 2)) < (ATTN_HEAD // 4)
    q_scale = ATTN_HEAD ** -0.5 * math.log2(math.e)

    def norm_rope(x, g):
        ms = jnp.mean(x * x, axis=-1, keepdims=True)
        y = x * lax.rsqrt(ms + NORM_EPS) * g
        partner = jnp.where(first, pltpu.roll(y, ATTN_HEAD - ATTN_HEAD // 4, 1), pltpu.roll(y, ATTN_HEAD // 4, 1))
        return y * cos + partner * sin

    qd = ATTN_Q_HEADS * ATTN_HEAD
    kd = ATTN_KV_HEADS * ATTN_HEAD
    q_pieces = [norm_rope(p_ref[0, :, h * ATTN_HEAD:(h + 1) * ATTN_HEAD], qg_ref[...]) * q_scale
                for h in range(ATTN_Q_HEADS)]
    q_ref[0] = jnp.concatenate(q_pieces, axis=-1).astype(BF16)
    k_pieces = [norm_rope(p_ref[0, :, qd + h * ATTN_HEAD:qd + (h + 1) * ATTN_HEAD], kg_ref[...])
                for h in range(ATTN_KV_HEADS)]
    k_ref[0] = jnp.concatenate(k_pieces, axis=-1).astype(BF16)
    v_ref[0] = p_ref[0, :, qd + kd:qd + 2 * kd].astype(BF16)


def _rope_tables(t, ctx_len):
    n_rows = (t - ctx_len) // GRID_W
    axis_dim = ATTN_HEAD // 2
    inv_freq = jnp.power(ROPE_THETA, -jnp.arange(0, axis_dim, 2, dtype=F32) / axis_dim)
    ar = jnp.arange(n_rows, dtype=F32)[:, None] * inv_freq
    ac = jnp.arange(GRID_W, dtype=F32)[:, None] * inv_freq
    on_rows = lambda x: jnp.broadcast_to(x[:, None, :], (n_rows, GRID_W, x.shape[-1]))
    on_cols = lambda x: jnp.broadcast_to(x[None, :, :], (n_rows, GRID_W, x.shape[-1]))
    cr, sr, cc, sc = on_rows(jnp.cos(ar)), on_rows(jnp.sin(ar)), on_cols(jnp.cos(ac)), on_cols(jnp.sin(ac))
    cos = jnp.concatenate([cr, cr, cc, cc], axis=-1).reshape(t - ctx_len, ATTN_HEAD)
    sin = jnp.concatenate([-sr, sr, -sc, sc], axis=-1).reshape(t - ctx_len, ATTN_HEAD)
    cos = jnp.concatenate([jnp.ones((ctx_len, ATTN_HEAD), F32), cos], axis=0)
    sin = jnp.concatenate([jnp.zeros((ctx_len, ATTN_HEAD), F32), sin], axis=0)
    return cos, sin


def _attn_prep(p, q_g, k_g, cos, sin):
    nb, t, n = p.shape
    tm = _row_tile(t, 640)
    qd = ATTN_Q_HEADS * ATTN_HEAD
    kd = ATTN_KV_HEADS * ATTN_HEAD
    row = lambda w: pl.BlockSpec((1, tm, w), lambda b, i: (b, i, 0))
    tab = pl.BlockSpec((tm, ATTN_HEAD), lambda b, i: (i, 0))
    vec = pl.BlockSpec((1, ATTN_HEAD), lambda b, i: (0, 0))
    return pl.pallas_call(
        _attn_prep_kernel,
        grid=(nb, t // tm),
        in_specs=[row(n), vec, vec, tab, tab],
        out_specs=[row(qd), row(kd), row(kd)],
        out_shape=[jax.ShapeDtypeStruct((nb, t, qd), BF16),
                   jax.ShapeDtypeStruct((nb, t, kd), BF16),
                   jax.ShapeDtypeStruct((nb, t, kd), BF16)],
        compiler_params=_params(("arbitrary", "arbitrary")),
        name="attn_prep",
    )(p, q_g.reshape(1, -1), k_g.reshape(1, -1), cos, sin)


def _flash_kernel(q_ref, k_ref, v_ref, o_ref, qs_ref, m_ref, l_ref, acc_ref, sa_ref, sb_ref, *, tq, tk, ctx_len, t_total):
    i = pl.program_id(2)
    g, hd = ATTN_GROUP, ATTN_HEAD
    for h in range(g):
        qs_ref[h * tq:(h + 1) * tq, :] = q_ref[0, :, h * hd:(h + 1) * hd]
    m_ref[...] = jnp.full_like(m_ref, -jnp.inf)
    l_ref[...] = jnp.zeros_like(l_ref)
    acc_ref[...] = jnp.zeros_like(acc_ref)

    def scores(start, width):
        kc = k_ref[0, pl.ds(start, width), :]
        return lax.dot_general(qs_ref[...], kc, (((1,), (1,)), ((), ())), preferred_element_type=F32)

    def update(s, start, width):
        vc = v_ref[0, pl.ds(start, width), :]
        m_old = m_ref[...]
        m_new = jnp.maximum(m_old, jnp.max(s, axis=-1, keepdims=True))
        alpha = jnp.exp2(m_old - m_new)
        p = jnp.exp2(s - jnp.tile(m_new, (1, width // LANES)))
        psum = p[:, 0:LANES]
        for n in range(1, width // LANES):
            psum = psum + p[:, n * LANES:(n + 1) * LANES]
        l_ref[...] = alpha * l_ref[...] + psum
        acc_ref[...] = alpha * acc_ref[...] + jnp.dot(p.astype(BF16), vc, preferred_element_type=F32)
        m_ref[...] = m_new

    @pl.when(i * tq < ctx_len)
    def _():
        update(scores(0, ctx_len), 0, ctx_len)

    n_kv = t_total // tk

    @pl.when(i * tq >= ctx_len)
    def _():
        sa_ref[...] = scores(0, tk)

        def pair(c):
            first = pl.multiple_of(2 * c * tk, tk)
            second = pl.multiple_of(first + tk, tk)
            third = pl.multiple_of(jnp.minimum(2 * c + 2, n_kv - 1) * tk, tk)
            sb_ref[...] = scores(second, tk)
            update(sa_ref[...], first, tk)
            sa_ref[...] = scores(third, tk)
            update(sb_ref[...], second, tk)

        n_pairs = n_kv // 2
        unroll = 4 if n_pairs % 4 == 1 else 2

        def body(c, carry):
            for u in range(unroll):
                pair(unroll * c + u)
            return carry
        lax.fori_loop(0, n_pairs // unroll, body, 0)
        for c in range(n_pairs - n_pairs % unroll, n_pairs):
            pair(c)

    out = acc_ref[...] / jnp.sum(l_ref[...], axis=-1, keepdims=True)
    o_ref[0] = jnp.concatenate([out[h * tq:(h + 1) * tq] for h in range(g)], axis=-1).astype(BF16)


def _flash_attention(q, k, v, ctx_len):
    nb, t, qd = q.shape
    tq = 256
    tk = _row_tile(t, 640, LANES)
    assert ctx_len == tq and t % tq == 0 and ctx_len % LANES == 0 and (t // tk) % 2 == 0
    gw = ATTN_GROUP * ATTN_HEAD
    return pl.pallas_call(
        functools.partial(_flash_kernel, tq=tq, tk=tk, ctx_len=ctx_len, t_total=t),
        grid=(nb, ATTN_KV_HEADS, t // tq),
        in_specs=[pl.BlockSpec((1, tq, gw), lambda b, kv, i: (b, i, kv)),
                  pl.BlockSpec((1, t, ATTN_HEAD), lambda b, kv, i: (b, 0, kv)),
                  pl.BlockSpec((1, t, ATTN_HEAD), lambda b, kv, i: (b, 0, kv))],
        out_specs=pl.BlockSpec((1, tq, gw), lambda b, kv, i: (b, i, kv)),
        out_shape=jax.ShapeDtypeStruct((nb, t, qd), BF16),
        scratch_shapes=[pltpu.VMEM((ATTN_GROUP * tq, ATTN_HEAD), BF16),
                        pltpu.VMEM((ATTN_GROUP * tq, LANES), F32),
                        pltpu.VMEM((ATTN_GROUP * tq, LANES), F32),
                        pltpu.VMEM((ATTN_GROUP * tq, ATTN_HEAD), F32),
                        pltpu.VMEM((ATTN_GROUP * tq, tk), F32),
                        pltpu.VMEM((ATTN_GROUP * tq, tk), F32)],
        compiler_params=_params(("arbitrary", "arbitrary", "arbitrary")),
        name="flash_attention",
    )(q, k, v)


def _pad_cols(w, n):
    return jnp.pad(w, ((0, 0), (0, n - w.shape[1])))


def _deltanet_layer(xc, mod, norm_g, w_in, conv_w, a_log, dt_bias, out_norm_g, w_out, ctx_len):
    nb, t, _ = xc.shape
    gate_col = DN_QKV_DIM + DN_V_DIM
    conv_w_t = jnp.pad(conv_w.T.astype(F32), ((0, 8 - SHORT_CONV), (0, 0)))
    qkv = _dn_qkv(xc, mod, norm_g, w_in[:, :DN_QKV_DIM].astype(BF16), conv_w_t, ctx_len)
    w_gate = _pad_cols(w_in[:, gate_col:][:, _dn_gate_lane_perm()], LANES)
    w_rest = jnp.concatenate([w_in[:, DN_QKV_DIM:gate_col], w_gate], axis=1).astype(BF16)
    z, gate_logits = _in_projection(xc, mod, norm_g, w_rest, MXU_CHUNK, ctx_len, bf16_cols=DN_V_DIM)
    gb = _dn_gates(gate_logits, 0, a_log, dt_bias)
    per_head = 4 * DN_GROUP
    g_rows = gb[..., :DN_K_HEADS * per_head].reshape(nb, t // CHUNK, CHUNK, DN_K_HEADS, per_head)
    g_rows = g_rows.transpose(0, 3, 1, 4, 2)
    o_f, o_b = _dn_scan(qkv, gb, g_rows)
    return _out_projection([o_f, o_b], z, 0, out_norm_g, DN_V_HEADS, DN_HEAD,
                           w_out.astype(BF16), xc, mod, ctx_len, 640)


def _gla_layer(xc, mod, norm_g, w_in, gate_w2, gate_b2, out_norm_g, w_out, ctx_len):
    n_pad = 2 * GLA_K_DIM + 2 * GLA_V_DIM + LANES
    p = _in_projection(xc, mod, norm_g, _pad_cols(w_in, n_pad).astype(BF16), MXU_CHUNK, ctx_len)
    r = GLA_GATE_RANK
    w2cat = jnp.zeros((LANES, 2 * GLA_K_DIM), F32)
    w2cat = w2cat.at[0:r, :GLA_K_DIM].set(gate_w2[0]).at[r:2 * r, GLA_K_DIM:].set(gate_w2[1]).astype(BF16)
    b2cat = gate_b2.reshape(1, 2 * GLA_K_DIM).astype(F32)
    o_f, o_b = _gla_scan(p, w2cat, b2cat)
    z_block = (2 * GLA_K_DIM + GLA_V_DIM) // GLA_V_DIM
    return _out_projection([o_f, o_b], p, z_block, out_norm_g, GLA_HEADS, GLA_HEAD_V,
                           w_out.astype(BF16), xc, mod, ctx_len, 640)


def _attention_layer(xc, mod, norm_g, w_in, q_g, k_g, w_out, rope, ctx_len):
    p = _in_projection(xc, mod, norm_g, w_in.astype(BF16), MXU_CHUNK, ctx_len)
    q, k, v = _attn_prep(p, q_g, k_g, *rope)
    o = _flash_attention(q, k, v, ctx_len)
    return _out_projection([o], None, 0, None, 0, 0, w_out.astype(BF16), xc, mod, ctx_len, 1280)


def kernel(x, c, ctx, c_ctx, ada_w, ada_b, norm_mix_g, norm_ffn_g, ffn_w1, ffn_w2, dn_w_in, dn_conv_w, dn_a_log, dn_dt_bias, dn_norm_g, dn_w_out, gla_w_in, gla_gate_w2, gla_gate_b2, gla_norm_g, gla_w_out, attn_w_in, attn_q_norm_g, attn_k_norm_g, attn_w_out):
    nb, seq, d = x.shape
    ctx_len = ctx.shape[1]
    depth = ada_w.shape[0]
    assert ctx_len == SCAN_BLOCK and seq % SCAN_BLOCK == 0 and nb < MOD_ROWS
    t = ctx_len + seq
    xc = jnp.concatenate([ctx, x], axis=1)
    cvec = jnp.zeros((MOD_ROWS, d), F32).at[:nb].set(c).at[nb].set(c_ctx)
    mods = _ada_vectors(cvec, ada_w, ada_b)
    rope = _rope_tables(t, ctx_len)
    for i in range(depth):
        mix, slot = i % 3, i // 3
        mod = mods[i]
        if mix == 0:
            xc = _deltanet_layer(xc, mod, norm_mix_g[i], dn_w_in[slot], dn_conv_w[slot], dn_a_log[slot],
                                 dn_dt_bias[slot], dn_norm_g[slot], dn_w_out[slot], ctx_len)
        elif mix == 1:
            xc = _gla_layer(xc, mod, norm_mix_g[i], gla_w_in[slot], gla_gate_w2[slot], gla_gate_b2[slot],
                            gla_norm_g[slot], gla_w_out[slot], ctx_len)
        else:
            xc = _attention_layer(xc, mod, norm_mix_g[i], attn_w_in[slot], attn_q_norm_g[slot],
                                  attn_k_norm_g[slot], attn_w_out[slot], rope, ctx_len)
        xc = _ffn(xc, mod, norm_ffn_g[i], ffn_w1[i].astype(BF16), ffn_w2[i].astype(BF16), ctx_len)
    return xc[:, ctx_len:, :]
```

```python
import functools
import math

import numpy as np
import jax
import jax.numpy as jnp
from jax import lax
from jax.experimental import pallas as pl
from jax.experimental.pallas import tpu as pltpu

F32 = jnp.float32
BF16 = jnp.bfloat16

NORM_EPS = 1e-6
GRID_W = 64
ROPE_THETA = 10000.0
SHORT_CONV = 5

DN_K_HEADS = 8
DN_V_HEADS = 16
DN_HEAD = 128
DN_GROUP = DN_V_HEADS // DN_K_HEADS
DN_K_DIM = DN_K_HEADS * DN_HEAD
DN_V_DIM = DN_V_HEADS * DN_HEAD
DN_QKV_DIM = 2 * DN_K_DIM + DN_V_DIM
DN_KH_STEP = 4

GLA_HEADS = 4
GLA_HEAD_K = 128
GLA_HEAD_V = 256
GLA_K_DIM = GLA_HEADS * GLA_HEAD_K
GLA_V_DIM = GLA_HEADS * GLA_HEAD_V
GLA_GATE_RANK = 16
GLA_GATE_NORMALIZER = 16.0
GLA_H_STEP = 4

ATTN_Q_HEADS = 8
ATTN_KV_HEADS = 2
ATTN_HEAD = 128
ATTN_GROUP = ATTN_Q_HEADS // ATTN_KV_HEADS

CHUNK = 64
SCAN_BLOCK = 256
LANES = 128
MXU_CHUNK = 1024
MOD_ROWS = 8
VMEM_LIMIT = 56 * 1024 * 1024


def _params(semantics, vmem=VMEM_LIMIT):
    return pltpu.CompilerParams(dimension_semantics=semantics, vmem_limit_bytes=vmem)


def _sigmoid(x):
    return 1.0 / (1.0 + jnp.exp(-x))


def _softplus(x):
    return jnp.maximum(x, 0.0) + jnp.log(1.0 + jnp.exp(-jnp.abs(x)))


def _split3(x):
    hi = x.astype(BF16)
    r1 = x - hi.astype(F32)
    mid = r1.astype(BF16)
    lo = (r1 - mid.astype(F32)).astype(BF16)
    return hi, mid, lo


def _dot(a, b):
    return jnp.dot(a.astype(BF16), b.astype(BF16), preferred_element_type=F32)


def _dot_nt(a, b):
    return lax.dot_general(a.astype(BF16), b.astype(BF16), (((1,), (1,)), ((), ())),
                           preferred_element_type=F32)


def _dot_tn(a, b):
    return lax.dot_general(a.astype(BF16), b.astype(BF16), (((0,), (0,)), ((), ())),
                           preferred_element_type=F32)


def _dot_sel(p_bf16, x):
    hi, mid, lo = _split3(x)
    d = lambda y: jnp.dot(p_bf16, y, preferred_element_type=F32)
    return d(hi) + d(mid) + d(lo)


def _row_tile(total, target, multiple=8):
    best = None
    for t in range(multiple, min(total, target) + 1, multiple):
        if total % t == 0:
            best = t
    assert best is not None, (total, target, multiple)
    return best


def _mod_norm(x, g, mod_ref, b, row0, ctx_len, nb, shift_idx, scale_idx):
    d = x.shape[-1]
    ms = jnp.mean(x * x, axis=-1, keepdims=True)
    y = x * lax.rsqrt(ms + NORM_EPS) * g
    rows = row0 + lax.broadcasted_iota(jnp.int32, (x.shape[0], 1), 0)
    is_ctx = rows < ctx_len

    def pick(idx):
        vx = mod_ref[pl.ds(b, 1), idx * d:(idx + 1) * d]
        vc = mod_ref[nb:nb + 1, idx * d:(idx + 1) * d]
        return jnp.where(is_ctx, vc, vx)

    return y * (1.0 + pick(scale_idx)) + pick(shift_idx), is_ctx, pick


def _ada_kernel(c_ref, w_ref, b_ref, o_ref):
    c = c_ref[...]
    s = c * _sigmoid(c)
    o_ref[0] = jnp.dot(s, w_ref[0], preferred_element_type=F32,
                       precision=lax.Precision.HIGHEST) + b_ref[0]


def _ada_vectors(cvec, ada_w, ada_b):
    depth, d, n = ada_w.shape
    tn = _row_tile(n, 1536, LANES)
    return pl.pallas_call(
        _ada_kernel,
        grid=(depth, n // tn),
        in_specs=[pl.BlockSpec((MOD_ROWS, d), lambda l, j: (0, 0)),
                  pl.BlockSpec((1, d, tn), lambda l, j: (l, 0, j)),
                  pl.BlockSpec((1, 1, tn), lambda l, j: (l, 0, j))],
        out_specs=pl.BlockSpec((1, MOD_ROWS, tn), lambda l, j: (l, 0, j)),
        out_shape=jax.ShapeDtypeStruct((depth, MOD_ROWS, n), F32),
        compiler_params=_params(("arbitrary", "arbitrary")),
        name="ada_vectors",
    )(cvec, ada_w, ada_b.reshape(depth, 1, n))


def _inproj_kernel(x_ref, mod_ref, g_ref, w_ref, *o_refs, tm, tn, ctx_len, nb):
    b = pl.program_id(0)
    i = pl.program_id(1)
    h, _, _ = _mod_norm(x_ref[0], g_ref[...], mod_ref, b, i * tm, ctx_len, nb, 0, 1)
    h = h.astype(BF16)
    first = 0
    for o_ref in o_refs:
        n = o_ref.shape[2]
        for start in range(0, n, tn):
            stop = min(start + tn, n)
            res = jnp.dot(h, w_ref[:, first + start:first + stop], preferred_element_type=F32)
            o_ref[0, :, start:stop] = res.astype(o_ref.dtype)
        first += n


def _in_projection(xc, mod, g, w_bf16, tn, ctx_len, bf16_cols=0):
    nb, t, d = xc.shape
    n = w_bf16.shape[1]
    tm = _row_tile(t, 640)
    assert bf16_cols % tn == 0
    widths = [(bf16_cols, BF16)] * (bf16_cols > 0) + [(n - bf16_cols, F32)]
    out = pl.pallas_call(
        functools.partial(_inproj_kernel, tm=tm, tn=tn, ctx_len=ctx_len, nb=nb),
        grid=(nb, t // tm),
        in_specs=[pl.BlockSpec((1, tm, d), lambda b, i: (b, i, 0)),
                  pl.BlockSpec(mod.shape, lambda b, i: (0, 0)),
                  pl.BlockSpec((1, d), lambda b, i: (0, 0)),
                  pl.BlockSpec((d, n), lambda b, i: (0, 0), pipeline_mode=pl.Buffered(1))],
        out_specs=[pl.BlockSpec((1, tm, w), lambda b, i: (b, i, 0)) for w, _ in widths],
        out_shape=[jax.ShapeDtypeStruct((nb, t, w), dt) for w, dt in widths],
        compiler_params=_params(("arbitrary", "arbitrary")),
        name="in_projection",
    )(xc, mod, g.reshape(1, d), w_bf16)
    return out if bf16_cols else out[0]


def _outproj_kernel(*refs, n_o, gated, heads, head_dim, tm, ctx_len, nb):
    o_refs = refs[:n_o]
    pos = n_o
    if gated:
        z_ref, ng_ref = refs[pos], refs[pos + 1]
        pos += 2
    w_ref, x_ref, mod_ref, out_ref = refs[pos:pos + 4]
    b = pl.program_id(0)
    i = pl.program_id(1)
    if gated:
        pieces = []
        for h in range(heads):
            sl = slice(h * head_dim, (h + 1) * head_dim)
            o = o_refs[0][0, :, sl].astype(F32)
            for r in o_refs[1:]:
                o = o + r[0, :, sl].astype(F32)
            ms = jnp.mean(o * o, axis=-1, keepdims=True)
            o = o * lax.rsqrt(ms + NORM_EPS) * ng_ref[...]
            z = z_ref[0, :, sl].astype(F32)
            pieces.append((o * (z * _sigmoid(z))).astype(BF16))
        lhs = jnp.concatenate(pieces, axis=-1)
    else:
        lhs = o_refs[0][0]
    y = jnp.dot(lhs, w_ref[...], preferred_element_type=F32)
    d = y.shape[-1]
    rows = i * tm + lax.broadcasted_iota(jnp.int32, (tm, 1), 0)
    gate = jnp.where(rows < ctx_len, mod_ref[nb:nb + 1, 2 * d:3 * d], mod_ref[pl.ds(b, 1), 2 * d:3 * d])
    out_ref[0] = x_ref[0] + gate * y


def _out_projection(o_list, z_src, z_col_block, norm_g, heads, head_dim, w_bf16, xc, mod, ctx_len, tm_target):
    nb, t, d = xc.shape
    dv = w_bf16.shape[0]
    tm = _row_tile(t, tm_target)
    gated = z_src is not None
    row_spec = lambda width, col: pl.BlockSpec((1, tm, width), lambda b, i: (b, i, col))
    in_specs = [row_spec(dv, 0) for _ in o_list]
    args = list(o_list)
    if gated:
        in_specs += [row_spec(dv, z_col_block), pl.BlockSpec((1, head_dim), lambda b, i: (0, 0))]
        args += [z_src, norm_g.reshape(1, head_dim)]
    in_specs += [pl.BlockSpec((dv, d), lambda b, i: (0, 0)), row_spec(d, 0),
                 pl.BlockSpec(mod.shape, lambda b, i: (0, 0))]
    args += [w_bf16, xc, mod]
    return pl.pallas_call(
        functools.partial(_outproj_kernel, n_o=len(o_list), gated=gated, heads=heads, head_dim=head_dim,
                          tm=tm, ctx_len=ctx_len, nb=nb),
        grid=(nb, t // tm),
        in_specs=in_specs,
        out_specs=row_spec(d, 0),
        out_shape=jax.ShapeDtypeStruct((nb, t, d), F32),
        compiler_params=_params(("arbitrary", "arbitrary")),
        name="out_projection",
    )(*args)


def _ffn_kernel(x_ref, mod_ref, g_ref, w1_ref, w2_ref, o_ref, acc_ref, *, tm, fk, ctx_len, nb):
    b = pl.program_id(0)
    i = pl.program_id(1)
    x = x_ref[0]
    h, _, pick = _mod_norm(x, g_ref[...], mod_ref, b, i * tm, ctx_len, nb, 3, 4)
    h = h.astype(BF16)
    d_ff = w1_ref.shape[1]
    for k in range(d_ff // fk):
        u = jnp.dot(h, w1_ref[:, k * fk:(k + 1) * fk], preferred_element_type=F32)
        u = jnp.maximum(u, 0.0)
        u = (u * u).astype(BF16)
        contrib = jnp.dot(u, w2_ref[k * fk:(k + 1) * fk, :], preferred_element_type=F32)
        if k == 0:
            acc_ref[...] = contrib
        else:
            acc_ref[...] += contrib
    o_ref[0] = x + pick(5) * acc_ref[...]


def _ffn(xc, mod, g, w1_bf16, w2_bf16, ctx_len):
    nb, t, d = xc.shape
    d_ff = w1_bf16.shape[1]
    tm = _row_tile(t, 640)
    return pl.pallas_call(
        functools.partial(_ffn_kernel, tm=tm, fk=512, ctx_len=ctx_len, nb=nb),
        grid=(nb, t // tm),
        in_specs=[pl.BlockSpec((1, tm, d), lambda b, i: (b, i, 0)),
                  pl.BlockSpec(mod.shape, lambda b, i: (0, 0)),
                  pl.BlockSpec((1, d), lambda b, i: (0, 0)),
                  pl.BlockSpec((d, d_ff), lambda b, i: (0, 0)),
                  pl.BlockSpec((d_ff, d), lambda b, i: (0, 0))],
        out_specs=pl.BlockSpec((1, tm, d), lambda b, i: (b, i, 0)),
        out_shape=jax.ShapeDtypeStruct((nb, t, d), F32),
        scratch_shapes=[pltpu.VMEM((tm, d), F32)],
        compiler_params=_params(("arbitrary", "arbitrary")),
        name="ffn",
    )(xc, mod, g.reshape(1, d), w1_bf16, w2_bf16)


HALO = 8


def _dn_qkv_kernel(x_ref, xp_ref, xn_ref, mod_ref, g_ref, w_ref, cw_ref, o_ref, *,
                   tm, t_total, ctx_len, nb, fix_tile, fix_row):
    b = pl.program_id(0)
    i = pl.program_id(1)
    row0 = i * tm
    n_ext = tm + 2 * HALO
    pad = SHORT_CONV // 2
    x_ext = jnp.concatenate([xp_ref[0], x_ref[0], xn_ref[0]], axis=0)
    h, _, _ = _mod_norm(x_ext, g_ref[...], mod_ref, b, row0 - HALO, ctx_len, nb, 0, 1)
    t = row0 - HALO + lax.broadcasted_iota(jnp.int32, (n_ext, 1), 0)
    nearest = jnp.clip(t, row0, row0 + tm - 1)
    valid = (jnp.where(t >= 0, 1, 0) * jnp.where(t < t_total, 1, 0)
             * jnp.where(jnp.where(t >= ctx_len, 1, 0) == jnp.where(nearest >= ctx_len, 1, 0), 1, 0))
    h = jnp.where(valid > 0, h, 0.0).astype(BF16)
    if fix_row is not None:
        rr = lax.broadcasted_iota(jnp.int32, (2 * HALO, 1), 0)
        crosses = {}
        for d in range(-pad, pad + 1):
            if d:
                inside = jnp.where(rr + d >= 0, 1, 0) * jnp.where(rr + d < 2 * HALO, 1, 0)
                other = jnp.where(jnp.where(rr < HALO, 1, 0) != jnp.where(rr + d < HALO, 1, 0), 1, 0)
                crosses[d] = jnp.where(i == fix_tile, inside * other, 0) > 0
    n = w_ref.shape[1]
    for start in range(0, n, MXU_CHUNK):
        res = jnp.dot(h, w_ref[:, start:start + MXU_CHUNK], preferred_element_type=F32)
        for hd in range(MXU_CHUNK // DN_HEAD):
            col0 = start + hd * DN_HEAD
            cols = slice(col0, col0 + DN_HEAD)
            r = res[:, hd * DN_HEAD:(hd + 1) * DN_HEAD]
            acc = None
            for tap in range(SHORT_CONV):
                d = tap - pad
                xs = (r if d == 0 else pltpu.roll(r, (-d) % n_ext, 0))[HALO:HALO + tm]
                term = xs * cw_ref[tap:tap + 1, cols]
                acc = term if acc is None else acc + term
            if fix_row is not None:
                slab = r[fix_row:fix_row + 2 * HALO]
                wrong = None
                for d, mask in crosses.items():
                    term = jnp.where(mask, pltpu.roll(slab, (-d) % (2 * HALO), 0), 0.0) * cw_ref[d + pad:d + pad + 1, cols]
                    wrong = term if wrong is None else wrong + term
                acc = jnp.concatenate([acc[:fix_row - HALO], acc[fix_row - HALO:fix_row + HALO] - wrong,
                                       acc[fix_row + HALO:]], axis=0)
            half = 0.5 * acc
            y = half + half * jnp.tanh(half)
            if col0 < 2 * DN_K_DIM:
                ss = jnp.sum(y * y, axis=-1, keepdims=True)
                y = y * (lax.rsqrt(ss + NORM_EPS) * (DN_HEAD ** -0.5 if col0 < DN_K_DIM else 1.0))
            o_ref[0, :, cols] = y


def _dn_qkv(xc, mod, g, w_qkv_bf16, conv_w_t, ctx_len):
    nb, t, d = xc.shape
    n = w_qkv_bf16.shape[1]
    tm = _row_tile(t, 640)
    hb = tm // HALO
    last = t // HALO - 1
    fix_tile, fix_row = (ctx_len // tm, ctx_len % tm) if ctx_len % tm else (None, None)
    assert fix_row is None or (fix_row % HALO == 0 and HALO <= fix_row <= tm - HALO)
    return pl.pallas_call(
        functools.partial(_dn_qkv_kernel, tm=tm, t_total=t, ctx_len=ctx_len, nb=nb,
                          fix_tile=fix_tile, fix_row=fix_row),
        grid=(nb, t // tm),
        in_specs=[pl.BlockSpec((1, tm, d), lambda b, i: (b, i, 0)),
                  pl.BlockSpec((1, HALO, d), lambda b, i: (b, jnp.maximum(i * hb - 1, 0), 0)),
                  pl.BlockSpec((1, HALO, d), lambda b, i: (b, jnp.minimum((i + 1) * hb, last), 0)),
                  pl.BlockSpec(mod.shape, lambda b, i: (0, 0)),
                  pl.BlockSpec((1, d), lambda b, i: (0, 0)),
                  pl.BlockSpec((d, n), lambda b, i: (0, 0), pipeline_mode=pl.Buffered(1)),
                  pl.BlockSpec((8, n), lambda b, i: (0, 0))],
        out_specs=pl.BlockSpec((1, tm, n), lambda b, i: (b, i, 0)),
        out_shape=jax.ShapeDtypeStruct((nb, t, n), F32),
        compiler_params=_params(("arbitrary", "arbitrary")),
        name="dn_qkv",
    )(xc, xc, xc, mod, g.reshape(1, d), w_qkv_bf16, conv_w_t)


def _dn_gate_kernel(ab_ref, par_ref, o_ref, *, tm):
    x = ab_ref[0]
    g = -jnp.exp(par_ref[0:1, :]) * _softplus(x + par_ref[1:2, :])
    beta = _sigmoid(x)
    lane = lax.broadcasted_iota(jnp.int32, (1, LANES), 1)
    used = lane < 4 * DN_V_HEADS
    is_beta = (lane & (2 * DN_GROUP)) != 0
    is_reverse = (lane & DN_GROUP) != 0
    r = lax.broadcasted_iota(jnp.int32, (CHUNK, CHUNK), 0)
    c = lax.broadcasted_iota(jnp.int32, (CHUNK, CHUNK), 1)
    lower = jnp.where(c <= r, 1.0, 0.0).astype(BF16)
    upper = jnp.where(c >= r, 1.0, 0.0).astype(BF16)
    for k in range(tm // CHUNK):
        rows = slice(k * CHUNK, (k + 1) * CHUNK)
        gk = g[rows]
        fwd = _dot_sel(lower, gk)
        bwd = _dot_sel(upper, gk)
        o_ref[0, rows, :] = jnp.where(used, jnp.where(is_beta, beta[rows], jnp.where(is_reverse, bwd, fwd)), 0.0)


def _dn_gate_lane_perm():
    perm = np.zeros(4 * DN_V_HEADS, np.int32)
    for kh in range(DN_K_HEADS):
        for kind in range(2):
            for d in range(2):
                for j in range(DN_GROUP):
                    lane = ((kh * 2 + kind) * 2 + d) * DN_GROUP + j
                    perm[lane] = (kind * 2 + d) * DN_V_HEADS + kh * DN_GROUP + j
    return perm


def _dn_gates(p, ab_col_block, a_log, dt_bias):
    nb, t, _ = p.shape
    tm = _row_tile(t, 1280, CHUNK)
    perm = _dn_gate_lane_perm()
    on_lanes = lambda v: jnp.concatenate([v.reshape(-1).astype(F32), jnp.zeros(2 * DN_V_HEADS, F32)])[perm]
    par = jnp.zeros((8, LANES), F32)
    par = par.at[0, :4 * DN_V_HEADS].set(on_lanes(a_log))
    par = par.at[1, :4 * DN_V_HEADS].set(on_lanes(dt_bias))
    return pl.pallas_call(
        functools.partial(_dn_gate_kernel, tm=tm),
        grid=(nb, t // tm),
        in_specs=[pl.BlockSpec((1, tm, LANES), lambda b, i: (b, i, ab_col_block)),
                  pl.BlockSpec((8, LANES), lambda b, i: (0, 0))],
        out_specs=pl.BlockSpec((1, tm, LANES), lambda b, i: (b, i, 0)),
        out_shape=jax.ShapeDtypeStruct((nb, t, LANES), F32),
        compiler_params=_params(("arbitrary", "arbitrary")),
        name="dn_gates",
    )(p, par)


def _bdot(a, b):
    return jnp.einsum("nij,njk->nik", a.astype(BF16), b.astype(BF16), preferred_element_type=F32)


def _bdot_nt(a, b):
    return jnp.einsum("nid,njd->nij", a.astype(BF16), b.astype(BF16), preferred_element_type=F32)


def _bdot_tn(a, b):
    return jnp.einsum("nci,ncj->nij", a.astype(BF16), b.astype(BF16), preferred_element_type=F32)


N_LEVELS = int(math.log2(CHUNK))


def _unit_triangular_inverse(parts, eye):
    coupling = lambda lv: jnp.concatenate([jnp.where(masks[lv], m, 0.0) for m, masks in parts], axis=0)
    x = eye - coupling(0)
    for lv in range(1, N_LEVELS):
        x = x - _bdot(x, _bdot(coupling(lv), x))
    return x


def _coupling_masks(ri, ci, reverse):
    ti, tj = (CHUNK - 1 - ri, CHUNK - 1 - ci) if reverse else (ri, ci)
    masks = []
    for lv in range(N_LEVELS):
        bi, bj = lax.shift_right_logical(ti, lv), lax.shift_right_logical(tj, lv)
        masks.append(jnp.where((bi & 1) == 1, bi - 1, -1) == bj)
    return masks


def _dn_scan_kernel(qf_ref, kf_ref, vf_ref, gf_ref, gtf_ref,
                    qr_ref, kr_ref, vr_ref, gr_ref, gtr_ref,
                    of_ref, ob_ref, s_ref):
    @pl.when(pl.program_id(2) == 0)
    def _():
        s_ref[...] = jnp.zeros_like(s_ref)

    ri = lax.broadcasted_iota(jnp.int32, (CHUNK, CHUNK), 0)
    ci = lax.broadcasted_iota(jnp.int32, (CHUNK, CHUNK), 1)
    eye = jnp.where(ri == ci, 1.0, 0.0)
    nc = SCAN_BLOCK // CHUNK
    chunk_rows = [slice(c * CHUNK, (c + 1) * CHUNK) for c in range(nc)]
    dirs = ((qf_ref, kf_ref, vf_ref, gf_ref, gtf_ref, False),
            (qr_ref, kr_ref, vr_ref, gr_ref, gtr_ref, True))
    m_parts, rhs_l, a_l, qg_l, kt_l, egl_l = [], [], [], [], [], []
    head_cols = lambda n: slice(n * DN_HEAD, (n + 1) * DN_HEAD)
    for d, (q_ref, k_ref, v_ref, g_ref, gt_ref, reverse) in enumerate(dirs):
        incl = (ci >= ri) if reverse else (ci <= ri)
        strict = (ci > ri) if reverse else (ci < ri)
        last = 0 if reverse else CHUNK - 1
        m_l = []
        per_head = 4 * DN_GROUP
        first_lane = pl.program_id(1) * (DN_KH_STEP * per_head)
        g_all = pltpu.roll(g_ref[0], jnp.where(first_lane == 0, 0, LANES - first_lane), 1)
        for hh in range(DN_KH_STEP):
            q = jnp.stack([q_ref[0, r, head_cols(hh)] for r in chunk_rows])
            k = jnp.stack([k_ref[0, r, head_cols(hh)] for r in chunk_rows])
            qk_kk = _bdot_nt(jnp.concatenate([q, k], axis=1), k)
            qk, kk = qk_kk[:, :CHUNK], qk_kk[:, CHUNK:]
            for j in range(DN_GROUP):
                col = 2 * d + j
                lane = hh * per_head + col
                gc = jnp.stack([g_all[r, lane:lane + 1] for r in chunk_rows])
                bc = jnp.stack([g_all[r, lane + 2 * DN_GROUP:lane + 2 * DN_GROUP + 1] for r in chunk_rows])
                gr = jnp.stack([gt_ref[0, hh, c, col:col + 1, :] for c in range(nc)])
                v = jnp.stack([v_ref[0, r, head_cols(hh * DN_GROUP + j)] for r in chunk_rows])
                decay = jnp.where(incl, jnp.exp(jnp.where(incl, gc - gr, 0.0)), 0.0)
                eg = jnp.exp(gc)
                gl = gc[:, last:last + 1, :]
                m_l.append(jnp.where(strict, bc * kk * decay, 0.0))
                rhs_l.append(jnp.concatenate([k * (bc * eg), v * bc], axis=-1))
                a_l.append(qk * decay)
                qg_l.append(q * eg)
                kt_l.append(k * jnp.exp(gl - gc))
                egl_l.append(jnp.exp(gl))
        m_parts.append((jnp.concatenate(m_l, axis=0), _coupling_masks(ri, ci, reverse)))
    cat = lambda xs: jnp.concatenate(xs, axis=0)
    a, kt = cat(a_l), cat(kt_l)
    wu = _bdot(_unit_triangular_inverse(m_parts, eye), cat(rhs_l))
    kb = _bdot_tn(kt, wu)
    qo = _bdot(a, wu)
    qeff = cat(qg_l) - qo[:, :, :DN_HEAD]
    egl = cat(egl_l)
    state = s_ref[...]
    o_refs = (of_ref, ob_ref)
    vh_step = DN_KH_STEP * DN_GROUP
    for step in range(nc):
        chunk_of = [step if d == 0 else nc - 1 - step for d in range(2) for _ in range(vh_step)]
        idx = [ch * nc + c for ch, c in enumerate(chunk_of)]
        pick = lambda x: jnp.stack([x[n] for n in idx])
        kb_s = pick(kb)
        qs_ks = _bdot(jnp.concatenate([pick(qeff), kb_s[:, :, :DN_HEAD]], axis=1), state)
        o = qs_ks[:, :CHUNK] + pick(qo)[:, :, DN_HEAD:]
        state = pick(egl) * state + kb_s[:, :, DN_HEAD:] - qs_ks[:, CHUNK:]
        for ch, c in enumerate(chunk_of):
            o_refs[ch // vh_step][0, chunk_rows[c], head_cols(ch % vh_step)] = o[ch].astype(BF16)
    s_ref[...] = state


def _dn_scan(qkv, gates, g_rows):
    nb, t, _ = qkv.shape
    n_blocks = t // SCAN_BLOCK
    n_chunks = SCAN_BLOCK // CHUNK
    ks = DN_KH_STEP
    groups = DN_K_HEADS // ks
    fwd = lambda s: s
    bwd = lambda s: jnp.where(s == 0, 0, n_blocks - s)

    def specs(order):
        return [pl.BlockSpec((1, SCAN_BLOCK, ks * DN_HEAD), lambda b, h, s: (b, order(s), h)),
                pl.BlockSpec((1, SCAN_BLOCK, ks * DN_HEAD), lambda b, h, s: (b, order(s), groups + h)),
                pl.BlockSpec((1, SCAN_BLOCK, ks * DN_GROUP * DN_HEAD), lambda b, h, s: (b, order(s), groups + h)),
                pl.BlockSpec((1, SCAN_BLOCK, LANES), lambda b, h, s: (b, order(s), 0)),
                pl.BlockSpec((1, ks, n_chunks, 8, CHUNK), lambda b, h, s: (b, h, order(s), 0, 0))]

    out_spec = lambda order: pl.BlockSpec((1, SCAN_BLOCK, ks * DN_GROUP * DN_HEAD), lambda b, h, s: (b, order(s), h))
    o_shape = jax.ShapeDtypeStruct((nb, t, DN_V_DIM), BF16)
    return pl.pallas_call(
        _dn_scan_kernel,
        grid=(nb, groups, n_blocks),
        in_specs=specs(fwd) + specs(bwd),
        out_specs=[out_spec(fwd), out_spec(bwd)],
        out_shape=[o_shape, o_shape],
        scratch_shapes=[pltpu.VMEM((2 * ks * DN_GROUP, DN_HEAD, DN_HEAD), F32)],
        compiler_params=_params(("arbitrary", "arbitrary", "arbitrary")),
        name="dn_scan",
    )(qkv, qkv, qkv, gates, g_rows, qkv, qkv, qkv, gates, g_rows)


_GLA_LEVELS = (32, 16, 8, 4, 2, 1)


def _gla_tables(reverse):
    idx = np.arange(CHUNK)
    tau = (CHUNK - 1 - idx) if reverse else idx
    ti, tk = tau[:, None], tau[None, :]
    groups = [tk <= ti, tk > ti]
    masks = []
    for s in _GLA_LEVELS:
        bi, bk = ti // s, tk // s
        if s > 1:
            groups.append((bk == bi) & (tk <= ti) & (tk > bi * s))
        groups.append(((bk == bi) & (tk > ti)) | (tk == (bi + 1) * s))
        masks.append((bi % 2 == 1) & (bk == bi - 1))
    masks.append(ti == tk)
    sel = np.concatenate(groups, axis=0).astype(np.float32)
    return jnp.asarray(sel, BF16), jnp.asarray(np.stack(masks).astype(np.float32))


def _gla_scan_kernel(qf_ref, kf_ref, vf_ref, lf_ref, wf_ref, bf_ref, self_ref, mf_ref,
                     qr_ref, kr_ref, vr_ref, lr_ref, wr_ref, br_ref, selr_ref, mr_ref,
                     of_ref, ob_ref, s_ref):
    @pl.when(pl.program_id(2) == 0)
    def _():
        s_ref[...] = jnp.zeros_like(s_ref)

    nc = SCAN_BLOCK // CHUNK
    n_lev = len(_GLA_LEVELS)
    chunk_rows = [slice(c * CHUNK, (c + 1) * CHUNK) for c in range(nc)]
    dirs = ((qf_ref, kf_ref, vf_ref, lf_ref, wf_ref, bf_ref, self_ref, mf_ref, False),
            (qr_ref, kr_ref, vr_ref, lr_ref, wr_ref, br_ref, selr_ref, mr_ref, True))
    hs = GLA_H_STEP
    kcols = lambda hh: slice(hh * GLA_HEAD_K, (hh + 1) * GLA_HEAD_K)
    vcols = lambda hh: slice(hh * GLA_HEAD_V, (hh + 1) * GLA_HEAD_V)
    ql, kl, qg_l, kt_l, e_l, v_l = [], [], [], [], [], []
    for q_ref, k_ref, v_ref, l_ref, w_ref, b_ref, sel_ref, m_ref, reverse in dirs:
        last = 0 if reverse else CHUNK - 1
        logits = _dot(l_ref[0], w_ref[...]) + b_ref[...]
        gk_all = (jnp.minimum(logits, 0.0) - jnp.log(1.0 + jnp.exp(-jnp.abs(logits)))) * (1.0 / GLA_GATE_NORMALIZER)
        for hh in range(hs):
            for rows in chunk_rows:
                q = q_ref[0, rows, kcols(hh)] * (GLA_HEAD_K ** -0.5)
                k = k_ref[0, rows, kcols(hh)]
                gk = gk_all[rows, kcols(hh)]
                hi = gk.astype(BF16)
                mid = (gk - hi.astype(F32)).astype(BF16)
                both = jnp.dot(sel_ref[...], jnp.concatenate([hi, mid], axis=-1), preferred_element_type=F32)
                sums = both[:, :GLA_HEAD_K] + both[:, GLA_HEAD_K:]
                part = lambda n: sums[n * CHUNK:(n + 1) * CHUNK]
                bcum, tail = part(0), part(1)
                ql += [q * jnp.exp(part(2 + 2 * lv)) for lv in range(n_lev - 1)] + [q, q]
                kl += [k * jnp.exp(part(3 + 2 * lv)) for lv in range(n_lev - 1)] + [k * jnp.exp(part(2 * n_lev)), k]
                qg_l.append(q * jnp.exp(bcum))
                kt_l.append(k * jnp.exp(tail))
                e_l.append(jnp.exp(bcum[last:last + 1, :]))
                v_l.append(v_ref[0, rows, vcols(hh)])
    scores = _bdot_nt(jnp.stack(ql), jnp.stack(kl))
    a_l = []
    for n in range(2 * hs * nc):
        m_ref = dirs[n // (hs * nc)][7]
        a = m_ref[0] * scores[n * (n_lev + 1)]
        for lv in range(1, n_lev + 1):
            a = a + m_ref[lv] * scores[n * (n_lev + 1) + lv]
        a_l.append(a)
    v = jnp.stack(v_l)
    x = _bdot_tn(v, jnp.stack(kt_l))
    s_l = [None] * (2 * hs * nc)
    for ch in range(2 * hs):
        state = s_ref[ch]
        for c in (range(nc - 1, -1, -1) if dirs[ch // hs][8] else range(nc)):
            n = ch * nc + c
            s_l[n] = state
            state = state * e_l[n] + x[n]
        s_ref[ch] = state
    o = _bdot(jnp.stack(a_l), v) + _bdot_nt(jnp.stack(qg_l), jnp.stack(s_l))
    for n in range(2 * hs * nc):
        ch, c = divmod(n, nc)
        (of_ref, ob_ref)[ch // hs][0, chunk_rows[c], vcols(ch % hs)] = o[n].astype(BF16)


def _gla_scan(p, w2cat, b2cat):
    nb, t, _ = p.shape
    n_blocks = t // SCAN_BLOCK
    hs = GLA_H_STEP
    h = GLA_HEADS // hs
    low_block = (2 * GLA_K_DIM + 2 * GLA_V_DIM) // LANES
    fwd = lambda s: s
    bwd = lambda s: jnp.where(s == 0, 0, n_blocks - s)
    assert _GLA_LEVELS[-1] == 1
    n_sel = (1 + 2 * len(_GLA_LEVELS)) * CHUNK
    n_mask = len(_GLA_LEVELS) + 1

    def specs(order, d):
        return [pl.BlockSpec((1, SCAN_BLOCK, hs * GLA_HEAD_K), lambda b, hh, s: (b, order(s), hh)),
                pl.BlockSpec((1, SCAN_BLOCK, hs * GLA_HEAD_K), lambda b, hh, s: (b, order(s), h + hh)),
                pl.BlockSpec((1, SCAN_BLOCK, hs * GLA_HEAD_V), lambda b, hh, s: (b, order(s), h + hh)),
                pl.BlockSpec((1, SCAN_BLOCK, LANES), lambda b, hh, s: (b, order(s), low_block)),
                pl.BlockSpec((LANES, hs * GLA_HEAD_K), lambda b, hh, s: (0, d * h + hh)),
                pl.BlockSpec((1, hs * GLA_HEAD_K), lambda b, hh, s: (0, d * h + hh)),
                pl.BlockSpec((n_sel, CHUNK), lambda b, hh, s: (0, 0)),
                pl.BlockSpec((n_mask, CHUNK, CHUNK), lambda b, hh, s: (0, 0, 0))]

    out_spec = lambda order: pl.BlockSpec((1, SCAN_BLOCK, hs * GLA_HEAD_V), lambda b, hh, s: (b, order(s), hh))
    o_shape = jax.ShapeDtypeStruct((nb, t, GLA_V_DIM), BF16)
    sel_f, mask_f = _gla_tables(False)
    sel_r, mask_r = _gla_tables(True)
    return pl.pallas_call(
        _gla_scan_kernel,
        grid=(nb, h, n_blocks),
        in_specs=specs(fwd, 0) + specs(bwd, 1),
        out_specs=[out_spec(fwd), out_spec(bwd)],
        out_shape=[o_shape, o_shape],
        scratch_shapes=[pltpu.VMEM((2 * hs, GLA_HEAD_V, GLA_HEAD_K), F32)],
        compiler_params=_params(("arbitrary", "arbitrary", "arbitrary")),
        name="gla_scan",
    )(p, p, p, p, w2cat, b2cat, sel_f, mask_f, p, p, p, p, w2cat, b2cat, sel_r, mask_r)


def _attn_prep_kernel(p_ref, qg_ref, kg_ref, cos_ref, sin_ref, q_ref, k_ref, v_ref):
    cos = cos_ref[...]
    sin = sin_ref[...]
    lane = lax.broadcasted_iota(jnp.int32, (1, ATTN_HEAD), 1)
    first = (lane % (ATTN_HEAD // 2)) < (ATTN_HEAD // 4)
    q_scale = ATTN_HEAD ** -0.5 * math.log2(math.e)

    def norm_rope(x, g):
        ms = jnp.mean(x * x, axis=-1, keepdims=True)
        y = x * lax.rsqrt(ms + NORM_EPS) * g
        partner = jnp.where(first, pltpu.roll(y, ATTN_HEAD - ATTN_HEAD // 4, 1), pltpu.roll(y, ATTN_HEAD // 4, 1))
        return y * cos + partner * sin

    qd = ATTN_Q_HEADS * ATTN_HEAD
    kd = ATTN_KV_HEADS * ATTN_HEAD
    q_pieces = [norm_rope(p_ref[0, :, h * ATTN_HEAD:(h + 1) * ATTN_HEAD], qg_ref[...]) * q_scale
                for h in range(ATTN_Q_HEADS)]
    q_ref[0] = jnp.concatenate(q_pieces, axis=-1).astype(BF16)
    k_pieces = [norm_rope(p_ref[0, :, qd + h * ATTN_HEAD:qd + (h + 1) * ATTN_HEAD], kg_ref[...])
                for h in range(ATTN_KV_HEADS)]
    k_ref[0] = jnp.concatenate(k_pieces, axis=-1).astype(BF16)
    v_ref[0] = p_ref[0, :, qd + kd:qd + 2 * kd].astype(BF16)


def _rope_tables(t, ctx_len):
    n_rows = (t - ctx_len) // GRID_W
    axis_dim = ATTN_HEAD // 2
    inv_freq = jnp.power(ROPE_THETA, -jnp.arange(0, axis_dim, 2, dtype=F32) / axis_dim)
    ar = jnp.arange(n_rows, dtype=F32)[:, None] * inv_freq
    ac = jnp.arange(GRID_W, dtype=F32)[:, None] * inv_freq
    on_rows = lambda x: jnp.broadcast_to(x[:, None, :], (n_rows, GRID_W, x.shape[-1]))
    on_cols = lambda x: jnp.broadcast_to(x[None, :, :], (n_rows, GRID_W, x.shape[-1]))
    cr, sr, cc, sc = on_rows(jnp.cos(ar)), on_rows(jnp.sin(ar)), on_cols(jnp.cos(ac)), on_cols(jnp.sin(ac))
    cos = jnp.concatenate([cr, cr, cc, cc], axis=-1).reshape(t - ctx_len, ATTN_HEAD)
    sin = jnp.concatenate([-sr, sr, -sc, sc], axis=-1).reshape(t - ctx_len, ATTN_HEAD)
    cos = jnp.concatenate([jnp.ones((ctx_len, ATTN_HEAD), F32), cos], axis=0)
    sin = jnp.concatenate([jnp.zeros((ctx_len, ATTN_HEAD), F32), sin], axis=0)
    return cos, sin


def _attn_prep(p, q_g, k_g, cos, sin):
    nb, t, n = p.shape
    tm = _row_tile(t, 640)
    qd = ATTN_Q_HEADS * ATTN_HEAD
    kd = ATTN_KV_HEADS * ATTN_HEAD
    row = lambda w: pl.BlockSpec((1, tm, w), lambda b, i: (b, i, 0))
    tab = pl.BlockSpec((tm, ATTN_HEAD), lambda b, i: (i, 0))
    vec = pl.BlockSpec((1, ATTN_HEAD), lambda b, i: (0, 0))
    return pl.pallas_call(
        _attn_prep_kernel,
        grid=(nb, t // tm),
        in_specs=[row(n), vec, vec, tab, tab],
        out_specs=[row(qd), row(kd), row(kd)],
        out_shape=[jax.ShapeDtypeStruct((nb, t, qd), BF16),
                   jax.ShapeDtypeStruct((nb, t, kd), BF16),
                   jax.ShapeDtypeStruct((nb, t, kd), BF16)],
        compiler_params=_params(("arbitrary", "arbitrary")),
        name="attn_prep",
    )(p, q_g.reshape(1, -1), k_g.reshape(1, -1), cos, sin)


def _flash_kernel(q_ref, k_ref, v_ref, o_ref, qs_ref, m_ref, l_ref, acc_ref, sa_ref, sb_ref, *, tq, tk, ctx_len, t_total):
    i = pl.program_id(2)
    g, hd = ATTN_GROUP, ATTN_HEAD
    for h in range(g):
        qs_ref[h * tq:(h + 1) * tq, :] = q_ref[0, :, h * hd:(h + 1) * hd]
    m_ref[...] = jnp.full_like(m_ref, -jnp.inf)
    l_ref[...] = jnp.zeros_like(l_ref)
    acc_ref[...] = jnp.zeros_like(acc_ref)

    def scores(start, width):
        kc = k_ref[0, pl.ds(start, width), :]
        return lax.dot_general(qs_ref[...], kc, (((1,), (1,)), ((), ())), preferred_element_type=F32)

    def update(s, start, width):
        vc = v_ref[0, pl.ds(start, width), :]
        m_old = m_ref[...]
        m_new = jnp.maximum(m_old, jnp.max(s, axis=-1, keepdims=True))
        alpha = jnp.exp2(m_old - m_new)
        p = jnp.exp2(s - jnp.tile(m_new, (1, width // LANES)))
        psum = p[:, 0:LANES]
        for n in range(1, width // LANES):
            psum = psum + p[:, n * LANES:(n + 1) * LANES]
        l_ref[...] = alpha * l_ref[...] + psum
        acc_ref[...] = alpha * acc_ref[...] + jnp.dot(p.astype(BF16), vc, preferred_element_type=F32)
        m_ref[...] = m_new

    @pl.when(i * tq < ctx_len)
    def _():
        update(scores(0, ctx_len), 0, ctx_len)

    n_kv = t_total // tk

    @pl.when(i * tq >= ctx_len)
    def _():
        sa_ref[...] = scores(0, tk)

        def pair(c):
            first = pl.multiple_of(2 * c * tk, tk)
            second = pl.multiple_of(first + tk, tk)
            third = pl.multiple_of(jnp.minimum(2 * c + 2, n_kv - 1) * tk, tk)
            sb_ref[...] = scores(second, tk)
            update(sa_ref[...], first, tk)
            sa_ref[...] = scores(third, tk)
            update(sb_ref[...], second, tk)

        n_pairs = n_kv // 2
        unroll = 4 if n_pairs % 4 == 1 else 2

        def body(c, carry):
            for u in range(unroll):
                pair(unroll * c + u)
            return carry
        lax.fori_loop(0, n_pairs // unroll, body, 0)
        for c in range(n_pairs - n_pairs % unroll, n_pairs):
            pair(c)

    out = acc_ref[...] / jnp.sum(l_ref[...], axis=-1, keepdims=True)
    o_ref[0] = jnp.concatenate([out[h * tq:(h + 1) * tq] for h in range(g)], axis=-1).astype(BF16)


def _flash_attention(q, k, v, ctx_len):
    nb, t, qd = q.shape
    tq = 256
    tk = _row_tile(t, 640, LANES)
    assert ctx_len == tq and t % tq == 0 and ctx_len % LANES == 0 and (t // tk) % 2 == 0
    gw = ATTN_GROUP * ATTN_HEAD
    return pl.pallas_call(
        functools.partial(_flash_kernel, tq=tq, tk=tk, ctx_len=ctx_len, t_total=t),
        grid=(nb, ATTN_KV_HEADS, t // tq),
        in_specs=[pl.BlockSpec((1, tq, gw), lambda b, kv, i: (b, i, kv)),
                  pl.BlockSpec((1, t, ATTN_HEAD), lambda b, kv, i: (b, 0, kv)),
                  pl.BlockSpec((1, t, ATTN_HEAD), lambda b, kv, i: (b, 0, kv))],
        out_specs=pl.BlockSpec((1, tq, gw), lambda b, kv, i: (b, i, kv)),
        out_shape=jax.ShapeDtypeStruct((nb, t, qd), BF16),
        scratch_shapes=[pltpu.VMEM((ATTN_GROUP * tq, ATTN_HEAD), BF16),
                        pltpu.VMEM((ATTN_GROUP * tq, LANES), F32),
                        pltpu.VMEM((ATTN_GROUP * tq, LANES), F32),
                        pltpu.VMEM((ATTN_GROUP * tq, ATTN_HEAD), F32),
                        pltpu.VMEM((ATTN_GROUP * tq, tk), F32),
                        pltpu.VMEM((ATTN_GROUP * tq, tk), F32)],
        compiler_params=_params(("arbitrary", "arbitrary", "arbitrary")),
        name="flash_attention",
    )(q, k, v)


def _pad_cols(w, n):
    return jnp.pad(w, ((0, 0), (0, n - w.shape[1])))


def _deltanet_layer(xc, mod, norm_g, w_in, conv_w, a_log, dt_bias, out_norm_g, w_out, ctx_len):
    nb, t, _ = xc.shape
    gate_col = DN_QKV_DIM + DN_V_DIM
    conv_w_t = jnp.pad(conv_w.T.astype(F32), ((0, 8 - SHORT_CONV), (0, 0)))
    qkv = _dn_qkv(xc, mod, norm_g, w_in[:, :DN_QKV_DIM].astype(BF16), conv_w_t, ctx_len)
    w_gate = _pad_cols(w_in[:, gate_col:][:, _dn_gate_lane_perm()], LANES)
    w_rest = jnp.concatenate([w_in[:, DN_QKV_DIM:gate_col], w_gate], axis=1).astype(BF16)
    z, gate_logits = _in_projection(xc, mod, norm_g, w_rest, MXU_CHUNK, ctx_len, bf16_cols=DN_V_DIM)
    gb = _dn_gates(gate_logits, 0, a_log, dt_bias)
    per_head = 4 * DN_GROUP
    g_rows = gb[..., :DN_K_HEADS * per_head].reshape(nb, t // CHUNK, CHUNK, DN_K_HEADS, per_head)
    g_rows = g_rows.transpose(0, 3, 1, 4, 2)
    o_f, o_b = _dn_scan(qkv, gb, g_rows)
    return _out_projection([o_f, o_b], z, 0, out_norm_g, DN_V_HEADS, DN_HEAD,
                           w_out.astype(BF16), xc, mod, ctx_len, 640)


def _gla_layer(xc, mod, norm_g, w_in, gate_w2, gate_b2, out_norm_g, w_out, ctx_len):
    n_pad = 2 * GLA_K_DIM + 2 * GLA_V_DIM + LANES
    p = _in_projection(xc, mod, norm_g, _pad_cols(w_in, n_pad).astype(BF16), MXU_CHUNK, ctx_len)
    r = GLA_GATE_RANK
    w2cat = jnp.zeros((LANES, 2 * GLA_K_DIM), F32)
    w2cat = w2cat.at[0:r, :GLA_K_DIM].set(gate_w2[0]).at[r:2 * r, GLA_K_DIM:].set(gate_w2[1]).astype(BF16)
    b2cat = gate_b2.reshape(1, 2 * GLA_K_DIM).astype(F32)
    o_f, o_b = _gla_scan(p, w2cat, b2cat)
    z_block = (2 * GLA_K_DIM + GLA_V_DIM) // GLA_V_DIM
    return _out_projection([o_f, o_b], p, z_block, out_norm_g, GLA_HEADS, GLA_HEAD_V,
                           w_out.astype(BF16), xc, mod, ctx_len, 640)


def _attention_layer(xc, mod, norm_g, w_in, q_g, k_g, w_out, rope, ctx_len):
    p = _in_projection(xc, mod, norm_g, w_in.astype(BF16), MXU_CHUNK, ctx_len)
    q, k, v = _attn_prep(p, q_g, k_g, *rope)
    o = _flash_attention(q, k, v, ctx_len)
    return _out_projection([o], None, 0, None, 0, 0, w_out.astype(BF16), xc, mod, ctx_len, 1280)


def kernel(x, c, ctx, c_ctx, ada_w, ada_b, norm_mix_g, norm_ffn_g, ffn_w1, ffn_w2, dn_w_in, dn_conv_w, dn_a_log, dn_dt_bias, dn_norm_g, dn_w_out, gla_w_in, gla_gate_w2, gla_gate_b2, gla_norm_g, gla_w_out, attn_w_in, attn_q_norm_g, attn_k_norm_g, attn_w_out):
    nb, seq, d = x.shape
    ctx_len = ctx.shape[1]
    depth = ada_w.shape[0]
    assert ctx_len == SCAN_BLOCK and seq % SCAN_BLOCK == 0 and nb < MOD_ROWS
    t = ctx_len + seq
    xc = jnp.concatenate([ctx, x], axis=1)
    cvec = jnp.zeros((MOD_ROWS, d), F32).at[:nb].set(c).at[nb].set(c_ctx)
    mods = _ada_vectors(cvec, ada_w, ada_b)
    rope = _rope_tables(t, ctx_len)
    for i in range(depth):
        mix, slot = i % 3, i // 3
        mod = mods[i]
        if mix == 0:
            xc = _deltanet_layer(xc, mod, norm_mix_g[i], dn_w_in[slot], dn_conv_w[slot], dn_a_log[slot],
                                 dn_dt_bias[slot], dn_norm_g[slot], dn_w_out[slot], ctx_len)
        elif mix == 1:
            xc = _gla_layer(xc, mod, norm_mix_g[i], gla_w_in[slot], gla_gate_w2[slot], gla_gate_b2[slot],
                            gla_norm_g[slot], gla_w_out[slot], ctx_len)
        else:
            xc = _attention_layer(xc, mod, norm_mix_g[i], attn_w_in[slot], attn_q_norm_g[slot],
                                  attn_k_norm_g[slot], attn_w_out[slot], rope, ctx_len)
        xc = _ffn(xc, mod, norm_ffn_g[i], ffn_w1[i].astype(BF16), ffn_w2[i].astype(BF16), ctx_len)
    return xc[:, ctx_len:, :]
```

```python
import functools
import math

import numpy as np
import jax
import jax.numpy as jnp
from jax import lax
from jax.experimental import pallas as pl
from jax.experimental.pallas import tpu as pltpu

F32 = jnp.float32
BF16 = jnp.bfloat16

NORM_EPS = 1e-6
GRID_W = 64
ROPE_THETA = 10000.0
SHORT_CONV = 5

DN_K_HEADS = 8
DN_V_HEADS = 16
DN_HEAD = 128
DN_GROUP = DN_V_HEADS // DN_K_HEADS
DN_K_DIM = DN_K_HEADS * DN_HEAD
DN_V_DIM = DN_V_HEADS * DN_HEAD
DN_QKV_DIM = 2 * DN_K_DIM + DN_V_DIM
DN_KH_STEP = 4

GLA_HEADS = 4
GLA_HEAD_K = 128
GLA_HEAD_V = 256
GLA_K_DIM = GLA_HEADS * GLA_HEAD_K
GLA_V_DIM = GLA_HEADS * GLA_HEAD_V
GLA_GATE_RANK = 16
GLA_GATE_NORMALIZER = 16.0
GLA_H_STEP = 4

ATTN_Q_HEADS = 8
ATTN_KV_HEADS = 2
ATTN_HEAD = 128
ATTN_GROUP = ATTN_Q_HEADS // ATTN_KV_HEADS

CHUNK = 64
SCAN_BLOCK = 256
LANES = 128
MXU_CHUNK = 1024
MOD_ROWS = 8
VMEM_LIMIT = 56 * 1024 * 1024


def _params(semantics, vmem=VMEM_LIMIT):
    return pltpu.CompilerParams(dimension_semantics=semantics, vmem_limit_bytes=vmem)


def _sigmoid(x):
    return 1.0 / (1.0 + jnp.exp(-x))


def _softplus(x):
    return jnp.maximum(x, 0.0) + jnp.log(1.0 + jnp.exp(-jnp.abs(x)))


def _split3(x):
    hi = x.astype(BF16)
    r1 = x - hi.astype(F32)
    mid = r1.astype(BF16)
    lo = (r1 - mid.astype(F32)).astype(BF16)
    return hi, mid, lo


def _dot(a, b):
    return jnp.dot(a.astype(BF16), b.astype(BF16), preferred_element_type=F32)


def _dot_sel(p_bf16, x):
    hi, mid, lo = _split3(x)
    d = lambda y: jnp.dot(p_bf16, y, preferred_element_type=F32)
    return d(hi) + d(mid) + d(lo)


def _row_tile(total, target, multiple=8):
    best = None
    for t in range(multiple, min(total, target) + 1, multiple):
        if total % t == 0:
            best = t
    assert best is not None, (total, target, multiple)
    return best


def _mod_norm(x, g, mod_ref, b, row0, ctx_len, nb, shift_idx, scale_idx):
    d = x.shape[-1]
    ms = jnp.mean(x * x, axis=-1, keepdims=True)
    y = x * lax.rsqrt(ms + NORM_EPS) * g
    rows = row0 + lax.broadcasted_iota(jnp.int32, (x.shape[0], 1), 0)
    is_ctx = rows < ctx_len

    def pick(idx):
        vx = mod_ref[pl.ds(b, 1), idx * d:(idx + 1) * d]
        vc = mod_ref[nb:nb + 1, idx * d:(idx + 1) * d]
        return jnp.where(is_ctx, vc, vx)

    return y * (1.0 + pick(scale_idx)) + pick(shift_idx), is_ctx, pick


def _ada_kernel(c_ref, w_ref, b_ref, o_ref):
    c = c_ref[...]
    s = c * _sigmoid(c)
    o_ref[0] = jnp.dot(s, w_ref[0], preferred_element_type=F32,
                       precision=lax.Precision.HIGHEST) + b_ref[0]


def _ada_vectors(cvec, ada_w, ada_b):
    depth, d, n = ada_w.shape
    tn = _row_tile(n, 1536, LANES)
    return pl.pallas_call(
        _ada_kernel,
        grid=(depth, n // tn),
        in_specs=[pl.BlockSpec((MOD_ROWS, d), lambda l, j: (0, 0)),
                  pl.BlockSpec((1, d, tn), lambda l, j: (l, 0, j)),
                  pl.BlockSpec((1, 1, tn), lambda l, j: (l, 0, j))],
        out_specs=pl.BlockSpec((1, MOD_ROWS, tn), lambda l, j: (l, 0, j)),
        out_shape=jax.ShapeDtypeStruct((depth, MOD_ROWS, n), F32),
        compiler_params=_params(("arbitrary", "arbitrary")),
        name="ada_vectors",
    )(cvec, ada_w, ada_b.reshape(depth, 1, n))


def _inproj_kernel(x_ref, mod_ref, g_ref, w_ref, *o_refs, tm, tn, ctx_len, nb):
    b = pl.program_id(0)
    i = pl.program_id(1)
    h, _, _ = _mod_norm(x_ref[0], g_ref[...], mod_ref, b, i * tm, ctx_len, nb, 0, 1)
    h = h.astype(BF16)
    first = 0
    for o_ref in o_refs:
        n = o_ref.shape[2]
        for start in range(0, n, tn):
            stop = min(start + tn, n)
            res = jnp.dot(h, w_ref[:, first + start:first + stop], preferred_element_type=F32)
            o_ref[0, :, start:stop] = res.astype(o_ref.dtype)
        first += n


def _in_projection(xc, mod, g, w_bf16, tn, ctx_len, bf16_cols=0):
    nb, t, d = xc.shape
    n = w_bf16.shape[1]
    tm = _row_tile(t, 640)
    assert bf16_cols % tn == 0
    widths = [(bf16_cols, BF16)] * (bf16_cols > 0) + [(n - bf16_cols, F32)]
    out = pl.pallas_call(
        functools.partial(_inproj_kernel, tm=tm, tn=tn, ctx_len=ctx_len, nb=nb),
        grid=(nb, t // tm),
        in_specs=[pl.BlockSpec((1, tm, d), lambda b, i: (b, i, 0)),
                  pl.BlockSpec(mod.shape, lambda b, i: (0, 0)),
                  pl.BlockSpec((1, d), lambda b, i: (0, 0)),
                  pl.BlockSpec((d, n), lambda b, i: (0, 0), pipeline_mode=pl.Buffered(1))],
        out_specs=[pl.BlockSpec((1, tm, w), lambda b, i: (b, i, 0)) for w, _ in widths],
        out_shape=[jax.ShapeDtypeStruct((nb, t, w), dt) for w, dt in widths],
        compiler_params=_params(("arbitrary", "arbitrary")),
        name="in_projection",
    )(xc, mod, g.reshape(1, d), w_bf16)
    return out if bf16_cols else out[0]


def _outproj_kernel(*refs, n_o, gated, heads, head_dim, tm, ctx_len, nb):
    o_refs = refs[:n_o]
    pos = n_o
    if gated:
        z_ref, ng_ref = refs[pos], refs[pos + 1]
        pos += 2
    w_ref, x_ref, mod_ref, out_ref = refs[pos:pos + 4]
    b = pl.program_id(0)
    i = pl.program_id(1)
    if gated:
        pieces = []
        for h in range(heads):
            sl = slice(h * head_dim, (h + 1) * head_dim)
            o = o_refs[0][0, :, sl].astype(F32)
            for r in o_refs[1:]:
                o = o + r[0, :, sl].astype(F32)
            ms = jnp.mean(o * o, axis=-1, keepdims=True)
            o = o * lax.rsqrt(ms + NORM_EPS) * ng_ref[...]
            z = z_ref[0, :, sl].astype(F32)
            pieces.append((o * (z * _sigmoid(z))).astype(BF16))
        lhs = jnp.concatenate(pieces, axis=-1)
    else:
        lhs = o_refs[0][0]
    y = jnp.dot(lhs, w_ref[...], preferred_element_type=F32)
    d = y.shape[-1]
    rows = i * tm + lax.broadcasted_iota(jnp.int32, (tm, 1), 0)
    gate = jnp.where(rows < ctx_len, mod_ref[nb:nb + 1, 2 * d:3 * d], mod_ref[pl.ds(b, 1), 2 * d:3 * d])
    out_ref[0] = x_ref[0] + gate * y


def _out_projection(o_list, z_src, z_col_block, norm_g, heads, head_dim, w_bf16, xc, mod, ctx_len, tm_target):
    nb, t, d = xc.shape
    dv = w_bf16.shape[0]
    tm = _row_tile(t, tm_target)
    gated = z_src is not None
    row_spec = lambda width, col: pl.BlockSpec((1, tm, width), lambda b, i: (b, i, col))
    in_specs = [row_spec(dv, 0) for _ in o_list]
    args = list(o_list)
    if gated:
        in_specs += [row_spec(dv, z_col_block), pl.BlockSpec((1, head_dim), lambda b, i: (0, 0))]
        args += [z_src, norm_g.reshape(1, head_dim)]
    in_specs += [pl.BlockSpec((dv, d), lambda b, i: (0, 0)), row_spec(d, 0),
                 pl.BlockSpec(mod.shape, lambda b, i: (0, 0))]
    args += [w_bf16, xc, mod]
    return pl.pallas_call(
        functools.partial(_outproj_kernel, n_o=len(o_list), gated=gated, heads=heads, head_dim=head_dim,
                          tm=tm, ctx_len=ctx_len, nb=nb),
        grid=(nb, t // tm),
        in_specs=in_specs,
        out_specs=row_spec(d, 0),
        out_shape=jax.ShapeDtypeStruct((nb, t, d), F32),
        compiler_params=_params(("arbitrary", "arbitrary")),
        name="out_projection",
    )(*args)


def _ffn_kernel(x_ref, mod_ref, g_ref, w1_ref, w2_ref, o_ref, acc_ref, *, tm, fk, ctx_len, nb):
    b = pl.program_id(0)
    i = pl.program_id(1)
    x = x_ref[0]
    h, _, pick = _mod_norm(x, g_ref[...], mod_ref, b, i * tm, ctx_len, nb, 3, 4)
    h = h.astype(BF16)
    d_ff = w1_ref.shape[1]
    for k in range(d_ff // fk):
        u = jnp.dot(h, w1_ref[:, k * fk:(k + 1) * fk], preferred_element_type=F32)
        u = jnp.maximum(u, 0.0)
        u = (u * u).astype(BF16)
        contrib = jnp.dot(u, w2_ref[k * fk:(k + 1) * fk, :], preferred_element_type=F32)
        if k == 0:
            acc_ref[...] = contrib
        else:
            acc_ref[...] += contrib
    o_ref[0] = x + pick(5) * acc_ref[...]


def _ffn(xc, mod, g, w1_bf16, w2_bf16, ctx_len):
    nb, t, d = xc.shape
    d_ff = w1_bf16.shape[1]
    tm = _row_tile(t, 640)
    return pl.pallas_call(
        functools.partial(_ffn_kernel, tm=tm, fk=512, ctx_len=ctx_len, nb=nb),
        grid=(nb, t // tm),
        in_specs=[pl.BlockSpec((1, tm, d), lambda b, i: (b, i, 0)),
                  pl.BlockSpec(mod.shape, lambda b, i: (0, 0)),
                  pl.BlockSpec((1, d), lambda b, i: (0, 0)),
                  pl.BlockSpec((d, d_ff), lambda b, i: (0, 0)),
                  pl.BlockSpec((d_ff, d), lambda b, i: (0, 0))],
        out_specs=pl.BlockSpec((1, tm, d), lambda b, i: (b, i, 0)),
        out_shape=jax.ShapeDtypeStruct((nb, t, d), F32),
        scratch_shapes=[pltpu.VMEM((tm, d), F32)],
        compiler_params=_params(("arbitrary", "arbitrary")),
        name="ffn",
    )(xc, mod, g.reshape(1, d), w1_bf16, w2_bf16)


HALO = 8


def _dn_qkv_kernel(x_ref, xp_ref, xn_ref, mod_ref, g_ref, w_ref, cw_ref, o_ref, *,
                   tm, t_total, ctx_len, nb, fix_tile, fix_row):
    b = pl.program_id(0)
    i = pl.program_id(1)
    row0 = i * tm
    n_ext = tm + 2 * HALO
    pad = SHORT_CONV // 2
    x_ext = jnp.concatenate([xp_ref[0], x_ref[0], xn_ref[0]], axis=0)
    h, _, _ = _mod_norm(x_ext, g_ref[...], mod_ref, b, row0 - HALO, ctx_len, nb, 0, 1)
    t = row0 - HALO + lax.broadcasted_iota(jnp.int32, (n_ext, 1), 0)
    nearest = jnp.clip(t, row0, row0 + tm - 1)
    valid = (jnp.where(t >= 0, 1, 0) * jnp.where(t < t_total, 1, 0)
             * jnp.where(jnp.where(t >= ctx_len, 1, 0) == jnp.where(nearest >= ctx_len, 1, 0), 1, 0))
    h = jnp.where(valid > 0, h, 0.0).astype(BF16)
    if fix_row is not None:
        rr = lax.broadcasted_iota(jnp.int32, (2 * HALO, 1), 0)
        crosses = {}
        for d in range(-pad, pad + 1):
            if d:
                inside = jnp.where(rr + d >= 0, 1, 0) * jnp.where(rr + d < 2 * HALO, 1, 0)
                other = jnp.where(jnp.where(rr < HALO, 1, 0) != jnp.where(rr + d < HALO, 1, 0), 1, 0)
                crosses[d] = jnp.where(i == fix_tile, inside * other, 0) > 0
    n = w_ref.shape[1]
    for start in range(0, n, MXU_CHUNK):
        res = jnp.dot(h, w_ref[:, start:start + MXU_CHUNK], preferred_element_type=F32)
        for hd in range(MXU_CHUNK // DN_HEAD):
            col0 = start + hd * DN_HEAD
            cols = slice(col0, col0 + DN_HEAD)
            r = res[:, hd * DN_HEAD:(hd + 1) * DN_HEAD]
            acc = None
            for tap in range(SHORT_CONV):
                d = tap - pad
                xs = (r if d == 0 else pltpu.roll(r, (-d) % n_ext, 0))[HALO:HALO + tm]
                term = xs * cw_ref[tap:tap + 1, cols]
                acc = term if acc is None else acc + term
            if fix_row is not None:
                slab = r[fix_row:fix_row + 2 * HALO]
                wrong = None
                for d, mask in crosses.items():
                    term = jnp.where(mask, pltpu.roll(slab, (-d) % (2 * HALO), 0), 0.0) * cw_ref[d + pad:d + pad + 1, cols]
                    wrong = term if wrong is None else wrong + term
                acc = jnp.concatenate([acc[:fix_row - HALO], acc[fix_row - HALO:fix_row + HALO] - wrong,
                                       acc[fix_row + HALO:]], axis=0)
            half = 0.5 * acc
            y = half + half * jnp.tanh(half)
            if col0 < 2 * DN_K_DIM:
                ss = jnp.sum(y * y, axis=-1, keepdims=True)
                y = y * (lax.rsqrt(ss + NORM_EPS) * (DN_HEAD ** -0.5 if col0 < DN_K_DIM else 1.0))
            o_ref[0, :, cols] = y


def _dn_qkv(xc, mod, g, w_qkv_bf16, conv_w_t, ctx_len):
    nb, t, d = xc.shape
    n = w_qkv_bf16.shape[1]
    tm = _row_tile(t, 640)
    hb = tm // HALO
    last = t // HALO - 1
    fix_tile, fix_row = (ctx_len // tm, ctx_len % tm) if ctx_len % tm else (None, None)
    assert fix_row is None or (fix_row % HALO == 0 and HALO <= fix_row <= tm - HALO)
    return pl.pallas_call(
        functools.partial(_dn_qkv_kernel, tm=tm, t_total=t, ctx_len=ctx_len, nb=nb,
                          fix_tile=fix_tile, fix_row=fix_row),
        grid=(nb, t // tm),
        in_specs=[pl.BlockSpec((1, tm, d), lambda b, i: (b, i, 0)),
                  pl.BlockSpec((1, HALO, d), lambda b, i: (b, jnp.maximum(i * hb - 1, 0), 0)),
                  pl.BlockSpec((1, HALO, d), lambda b, i: (b, jnp.minimum((i + 1) * hb, last), 0)),
                  pl.BlockSpec(mod.shape, lambda b, i: (0, 0)),
                  pl.BlockSpec((1, d), lambda b, i: (0, 0)),
                  pl.BlockSpec((d, n), lambda b, i: (0, 0), pipeline_mode=pl.Buffered(1)),
                  pl.BlockSpec((8, n), lambda b, i: (0, 0))],
        out_specs=pl.BlockSpec((1, tm, n), lambda b, i: (b, i, 0)),
        out_shape=jax.ShapeDtypeStruct((nb, t, n), F32),
        compiler_params=_params(("arbitrary", "arbitrary")),
        name="dn_qkv",
    )(xc, xc, xc, mod, g.reshape(1, d), w_qkv_bf16, conv_w_t)


def _dn_gate_kernel(ab_ref, par_ref, o_ref, *, tm):
    x = ab_ref[0]
    g = -jnp.exp(par_ref[0:1, :]) * _softplus(x + par_ref[1:2, :])
    beta = _sigmoid(x)
    lane = lax.broadcasted_iota(jnp.int32, (1, LANES), 1)
    used = lane < 4 * DN_V_HEADS
    is_beta = (lane & (2 * DN_GROUP)) != 0
    is_reverse = (lane & DN_GROUP) != 0
    r = lax.broadcasted_iota(jnp.int32, (CHUNK, CHUNK), 0)
    c = lax.broadcasted_iota(jnp.int32, (CHUNK, CHUNK), 1)
    lower = jnp.where(c <= r, 1.0, 0.0).astype(BF16)
    upper = jnp.where(c >= r, 1.0, 0.0).astype(BF16)
    for k in range(tm // CHUNK):
        rows = slice(k * CHUNK, (k + 1) * CHUNK)
        gk = g[rows]
        fwd = _dot_sel(lower, gk)
        bwd = _dot_sel(upper, gk)
        o_ref[0, rows, :] = jnp.where(used, jnp.where(is_beta, beta[rows], jnp.where(is_reverse, bwd, fwd)), 0.0)


def _dn_gate_lane_perm():
    perm = np.zeros(4 * DN_V_HEADS, np.int32)
    for kh in range(DN_K_HEADS):
        for kind in range(2):
            for d in range(2):
                for j in range(DN_GROUP):
                    lane = ((kh * 2 + kind) * 2 + d) * DN_GROUP + j
                    perm[lane] = (kind * 2 + d) * DN_V_HEADS + kh * DN_GROUP + j
    return perm


def _dn_gates(p, ab_col_block, a_log, dt_bias):
    nb, t, _ = p.shape
    tm = _row_tile(t, 1280, CHUNK)
    perm = _dn_gate_lane_perm()
    on_lanes = lambda v: jnp.concatenate([v.reshape(-1).astype(F32), jnp.zeros(2 * DN_V_HEADS, F32)])[perm]
    par = jnp.zeros((8, LANES), F32)
    par = par.at[0, :4 * DN_V_HEADS].set(on_lanes(a_log))
    par = par.at[1, :4 * DN_V_HEADS].set(on_lanes(dt_bias))
    return pl.pallas_call(
        functools.partial(_dn_gate_kernel, tm=tm),
        grid=(nb, t // tm),
        in_specs=[pl.BlockSpec((1, tm, LANES), lambda b, i: (b, i, ab_col_block)),
                  pl.BlockSpec((8, LANES), lambda b, i: (0, 0))],
        out_specs=pl.BlockSpec((1, tm, LANES), lambda b, i: (b, i, 0)),
        out_shape=jax.ShapeDtypeStruct((nb, t, LANES), F32),
        compiler_params=_params(("arbitrary", "arbitrary")),
        name="dn_gates",
    )(p, par)


def _bdot(a, b):
    return jnp.einsum("nij,njk->nik", a.astype(BF16), b.astype(BF16), preferred_element_type=F32)


def _bdot_nt(a, b):
    return jnp.einsum("nid,njd->nij", a.astype(BF16), b.astype(BF16), preferred_element_type=F32)


def _bdot_tn(a, b):
    return jnp.einsum("nci,ncj->nij", a.astype(BF16), b.astype(BF16), preferred_element_type=F32)


N_LEVELS = int(math.log2(CHUNK))


def _unit_triangular_inverse(parts, eye):
    coupling = lambda lv: jnp.concatenate([jnp.where(masks[lv], m, 0.0) for m, masks in parts], axis=0)
    x = eye - coupling(0)
    for lv in range(1, N_LEVELS):
        x = x - _bdot(x, _bdot(coupling(lv), x))
    return x


def _coupling_masks(ri, ci, reverse):
    ti, tj = (CHUNK - 1 - ri, CHUNK - 1 - ci) if reverse else (ri, ci)
    masks = []
    for lv in range(N_LEVELS):
        bi, bj = lax.shift_right_logical(ti, lv), lax.shift_right_logical(tj, lv)
        masks.append(jnp.where((bi & 1) == 1, bi - 1, -1) == bj)
    return masks


def _dn_scan_kernel(qf_ref, kf_ref, vf_ref, gf_ref, gtf_ref,
                    qr_ref, kr_ref, vr_ref, gr_ref, gtr_ref,
                    of_ref, ob_ref, s_ref):
    @pl.when(pl.program_id(2) == 0)
    def _():
        s_ref[...] = jnp.zeros_like(s_ref)

    ri = lax.broadcasted_iota(jnp.int32, (CHUNK, CHUNK), 0)
    ci = lax.broadcasted_iota(jnp.int32, (CHUNK, CHUNK), 1)
    eye = jnp.where(ri == ci, 1.0, 0.0)
    nc = SCAN_BLOCK // CHUNK
    chunk_rows = [slice(c * CHUNK, (c + 1) * CHUNK) for c in range(nc)]
    dirs = ((qf_ref, kf_ref, vf_ref, gf_ref, gtf_ref, False),
            (qr_ref, kr_ref, vr_ref, gr_ref, gtr_ref, True))
    m_parts, rhs_l, a_l, qg_l, kt_l, egl_l = [], [], [], [], [], []
    head_cols = lambda n: slice(n * DN_HEAD, (n + 1) * DN_HEAD)
    for d, (q_ref, k_ref, v_ref, g_ref, gt_ref, reverse) in enumerate(dirs):
        incl = (ci >= ri) if reverse else (ci <= ri)
        strict = (ci > ri) if reverse else (ci < ri)
        last = 0 if reverse else CHUNK - 1
        m_l = []
        per_head = 4 * DN_GROUP
        first_lane = pl.program_id(1) * (DN_KH_STEP * per_head)
        g_all = pltpu.roll(g_ref[0], jnp.where(first_lane == 0, 0, LANES - first_lane), 1)
        for hh in range(DN_KH_STEP):
            q = jnp.stack([q_ref[0, r, head_cols(hh)] for r in chunk_rows])
            k = jnp.stack([k_ref[0, r, head_cols(hh)] for r in chunk_rows])
            qk_kk = _bdot_nt(jnp.concatenate([q, k], axis=1), k)
            qk, kk = qk_kk[:, :CHUNK], qk_kk[:, CHUNK:]
            for j in range(DN_GROUP):
                col = 2 * d + j
                lane = hh * per_head + col
                gc = jnp.stack([g_all[r, lane:lane + 1] for r in chunk_rows])
                bc = jnp.stack([g_all[r, lane + 2 * DN_GROUP:lane + 2 * DN_GROUP + 1] for r in chunk_rows])
                gr = jnp.stack([gt_ref[0, hh, c, col:col + 1, :] for c in range(nc)])
                v = jnp.stack([v_ref[0, r, head_cols(hh * DN_GROUP + j)] for r in chunk_rows])
                decay = jnp.where(incl, jnp.exp(jnp.where(incl, gc - gr, 0.0)), 0.0)
                eg = jnp.exp(gc)
                gl = gc[:, last:last + 1, :]
                m_l.append(jnp.where(strict, bc * kk * decay, 0.0))
                rhs_l.append(jnp.concatenate([k * (bc * eg), v * bc], axis=-1))
                a_l.append(qk * decay)
                qg_l.append(q * eg)
                kt_l.append(k * jnp.exp(gl - gc))
                egl_l.append(jnp.exp(gl))
        m_parts.append((jnp.concatenate(m_l, axis=0), _coupling_masks(ri, ci, reverse)))
    cat = lambda xs: jnp.concatenate(xs, axis=0)
    a, kt = cat(a_l), cat(kt_l)
    wu = _bdot(_unit_triangular_inverse(m_parts, eye), cat(rhs_l))
    kb = _bdot_tn(kt, wu)
    qo = _bdot(a, wu)
    qeff = cat(qg_l) - qo[:, :, :DN_HEAD]
    egl = cat(egl_l)
    state = s_ref[...]
    o_refs = (of_ref, ob_ref)
    vh_step = DN_KH_STEP * DN_GROUP
    for step in range(nc):
        chunk_of = [step if d == 0 else nc - 1 - step for d in range(2) for _ in range(vh_step)]
        idx = [ch * nc + c for ch, c in enumerate(chunk_of)]
        pick = lambda x: jnp.stack([x[n] for n in idx])
        kb_s = pick(kb)
        qs_ks = _bdot(jnp.concatenate([pick(qeff), kb_s[:, :, :DN_HEAD]], axis=1), state)
        o = qs_ks[:, :CHUNK] + pick(qo)[:, :, DN_HEAD:]
        state = pick(egl) * state + kb_s[:, :, DN_HEAD:] - qs_ks[:, CHUNK:]
        for ch, c in enumerate(chunk_of):
            o_refs[ch // vh_step][0, chunk_rows[c], head_cols(ch % vh_step)] = o[ch].astype(BF16)
    s_ref[...] = state


def _dn_scan(qkv, gates, g_rows):
    nb, t, _ = qkv.shape
    n_blocks = t // SCAN_BLOCK
    n_chunks = SCAN_BLOCK // CHUNK
    ks = DN_KH_STEP
    groups = DN_K_HEADS // ks
    fwd = lambda s: s
    bwd = lambda s: jnp.where(s == 0, 0, n_blocks - s)

    def specs(order):
        return [pl.BlockSpec((1, SCAN_BLOCK, ks * DN_HEAD), lambda b, h, s: (b, order(s), h)),
                pl.BlockSpec((1, SCAN_BLOCK, ks * DN_HEAD), lambda b, h, s: (b, order(s), groups + h)),
                pl.BlockSpec((1, SCAN_BLOCK, ks * DN_GROUP * DN_HEAD), lambda b, h, s: (b, order(s), groups + h)),
                pl.BlockSpec((1, SCAN_BLOCK, LANES), lambda b, h, s: (b, order(s), 0)),
                pl.BlockSpec((1, ks, n_chunks, 8, CHUNK), lambda b, h, s: (b, h, order(s), 0, 0))]

    out_spec = lambda order: pl.BlockSpec((1, SCAN_BLOCK, ks * DN_GROUP * DN_HEAD), lambda b, h, s: (b, order(s), h))
    o_shape = jax.ShapeDtypeStruct((nb, t, DN_V_DIM), BF16)
    return pl.pallas_call(
        _dn_scan_kernel,
        grid=(nb, groups, n_blocks),
        in_specs=specs(fwd) + specs(bwd),
        out_specs=[out_spec(fwd), out_spec(bwd)],
        out_shape=[o_shape, o_shape],
        scratch_shapes=[pltpu.VMEM((2 * ks * DN_GROUP, DN_HEAD, DN_HEAD), F32)],
        compiler_params=_params(("arbitrary", "arbitrary", "arbitrary")),
        name="dn_scan",
    )(qkv, qkv, qkv, gates, g_rows, qkv, qkv, qkv, gates, g_rows)


_GLA_LEVELS = (32, 16, 8, 4, 2, 1)


def _gla_tables(reverse):
    idx = np.arange(CHUNK)
    tau = (CHUNK - 1 - idx) if reverse else idx
    ti, tk = tau[:, None], tau[None, :]
    groups = [tk <= ti, tk > ti]
    masks = []
    for s in _GLA_LEVELS:
        bi, bk = ti // s, tk // s
        if s > 1:
            groups.append((bk == bi) & (tk <= ti) & (tk > bi * s))
        groups.append(((bk == bi) & (tk > ti)) | (tk == (bi + 1) * s))
        masks.append((bi % 2 == 1) & (bk == bi - 1))
    masks.append(ti == tk)
    sel = np.concatenate(groups, axis=0).astype(np.float32)
    return jnp.asarray(sel, BF16), jnp.asarray(np.stack(masks).astype(np.float32))


def _gla_scan_kernel(qf_ref, kf_ref, vf_ref, lf_ref, wf_ref, bf_ref, self_ref, mf_ref,
                     qr_ref, kr_ref, vr_ref, lr_ref, wr_ref, br_ref, selr_ref, mr_ref,
                     of_ref, ob_ref, s_ref):
    @pl.when(pl.program_id(2) == 0)
    def _():
        s_ref[...] = jnp.zeros_like(s_ref)

    nc = SCAN_BLOCK // CHUNK
    n_lev = len(_GLA_LEVELS)
    chunk_rows = [slice(c * CHUNK, (c + 1) * CHUNK) for c in range(nc)]
    dirs = ((qf_ref, kf_ref, vf_ref, lf_ref, wf_ref, bf_ref, self_ref, mf_ref, False),
            (qr_ref, kr_ref, vr_ref, lr_ref, wr_ref, br_ref, selr_ref, mr_ref, True))
    hs = GLA_H_STEP
    kcols = lambda hh: slice(hh * GLA_HEAD_K, (hh + 1) * GLA_HEAD_K)
    vcols = lambda hh: slice(hh * GLA_HEAD_V, (hh + 1) * GLA_HEAD_V)
    ql, kl, qg_l, kt_l, e_l, v_l = [], [], [], [], [], []
    for q_ref, k_ref, v_ref, l_ref, w_ref, b_ref, sel_ref, m_ref, reverse in dirs:
        last = 0 if reverse else CHUNK - 1
        logits = _dot(l_ref[0], w_ref[...]) + b_ref[...]
        gk_all = (jnp.minimum(logits, 0.0) - jnp.log(1.0 + jnp.exp(-jnp.abs(logits)))) * (1.0 / GLA_GATE_NORMALIZER)
        for hh in range(hs):
            for rows in chunk_rows:
                q = q_ref[0, rows, kcols(hh)] * (GLA_HEAD_K ** -0.5)
                k = k_ref[0, rows, kcols(hh)]
                gk = gk_all[rows, kcols(hh)]
                hi = gk.astype(BF16)
                mid = (gk - hi.astype(F32)).astype(BF16)
                both = jnp.dot(sel_ref[...], jnp.concatenate([hi, mid], axis=-1), preferred_element_type=F32)
                sums = both[:, :GLA_HEAD_K] + both[:, GLA_HEAD_K:]
                part = lambda n: sums[n * CHUNK:(n + 1) * CHUNK]
                bcum, tail = part(0), part(1)
                ql += [q * jnp.exp(part(2 + 2 * lv)) for lv in range(n_lev - 1)] + [q, q]
                kl += [k * jnp.exp(part(3 + 2 * lv)) for lv in range(n_lev - 1)] + [k * jnp.exp(part(2 * n_lev)), k]
                qg_l.append(q * jnp.exp(bcum))
                kt_l.append(k * jnp.exp(tail))
                e_l.append(jnp.exp(bcum[last:last + 1, :]))
                v_l.append(v_ref[0, rows, vcols(hh)])
    scores = _bdot_nt(jnp.stack(ql), jnp.stack(kl))
    a_l = []
    for n in range(2 * hs * nc):
        m_ref = dirs[n // (hs * nc)][7]
        a = m_ref[0] * scores[n * (n_lev + 1)]
        for lv in range(1, n_lev + 1):
            a = a + m_ref[lv] * scores[n * (n_lev + 1) + lv]
        a_l.append(a)
    v = jnp.stack(v_l)
    x = _bdot_tn(v, jnp.stack(kt_l))
    s_l = [None] * (2 * hs * nc)
    for ch in range(2 * hs):
        state = s_ref[ch]
        for c in (range(nc - 1, -1, -1) if dirs[ch // hs][8] else range(nc)):
            n = ch * nc + c
            s_l[n] = state
            state = state * e_l[n] + x[n]
        s_ref[ch] = state
    o = _bdot(jnp.stack(a_l), v) + _bdot_nt(jnp.stack(qg_l), jnp.stack(s_l))
    for n in range(2 * hs * nc):
        ch, c = divmod(n, nc)
        (of_ref, ob_ref)[ch // hs][0, chunk_rows[c], vcols(ch % hs)] = o[n].astype(BF16)


def _gla_scan(p, w2cat, b2cat):
    nb, t, _ = p.shape
    n_blocks = t // SCAN_BLOCK
    hs = GLA_H_STEP
    h = GLA_HEADS // hs
    low_block = (2 * GLA_K_DIM + 2 * GLA_V_DIM) // LANES
    fwd = lambda s: s
    bwd = lambda s: jnp.where(s == 0, 0, n_blocks - s)
    assert _GLA_LEVELS[-1] == 1
    n_sel = (1 + 2 * len(_GLA_LEVELS)) * CHUNK
    n_mask = len(_GLA_LEVELS) + 1

    def specs(order, d):
        return [pl.BlockSpec((1, SCAN_BLOCK, hs * GLA_HEAD_K), lambda b, hh, s: (b, order(s), hh)),
                pl.BlockSpec((1, SCAN_BLOCK, hs * GLA_HEAD_K), lambda b, hh, s: (b, order(s), h + hh)),
                pl.BlockSpec((1, SCAN_BLOCK, hs * GLA_HEAD_V), lambda b, hh, s: (b, order(s), h + hh)),
                pl.BlockSpec((1, SCAN_BLOCK, LANES), lambda b, hh, s: (b, order(s), low_block)),
                pl.BlockSpec((LANES, hs * GLA_HEAD_K), lambda b, hh, s: (0, d * h + hh)),
                pl.BlockSpec((1, hs * GLA_HEAD_K), lambda b, hh, s: (0, d * h + hh)),
                pl.BlockSpec((n_sel, CHUNK), lambda b, hh, s: (0, 0)),
                pl.BlockSpec((n_mask, CHUNK, CHUNK), lambda b, hh, s: (0, 0, 0))]

    out_spec = lambda order: pl.BlockSpec((1, SCAN_BLOCK, hs * GLA_HEAD_V), lambda b, hh, s: (b, order(s), hh))
    o_shape = jax.ShapeDtypeStruct((nb, t, GLA_V_DIM), BF16)
    sel_f, mask_f = _gla_tables(False)
    sel_r, mask_r = _gla_tables(True)
    return pl.pallas_call(
        _gla_scan_kernel,
        grid=(nb, h, n_blocks),
        in_specs=specs(fwd, 0) + specs(bwd, 1),
        out_specs=[out_spec(fwd), out_spec(bwd)],
        out_shape=[o_shape, o_shape],
        scratch_shapes=[pltpu.VMEM((2 * hs, GLA_HEAD_V, GLA_HEAD_K), F32)],
        compiler_params=_params(("arbitrary", "arbitrary", "arbitrary")),
        name="gla_scan",
    )(p, p, p, p, w2cat, b2cat, sel_f, mask_f, p, p, p, p, w2cat, b2cat, sel_r, mask_r)


def _attn_prep_kernel(p_ref, qg_ref, kg_ref, cos_ref, sin_ref, q_ref, k_ref, v_ref):
    cos = cos_ref[...]
    sin = sin_ref[...]
    lane = lax.broadcasted_iota(jnp.int32, (1, ATTN_HEAD), 1)
    first = (lane % (ATTN_HEAD // 2)) < (ATTN_HEAD // 4)
    q_scale = ATTN_HEAD ** -0.5 * math.log2(math.e)

    def norm_rope(x, g):
        ms = jnp.mean(x * x, axis=-1, keepdims=True)
        y = x * lax.rsqrt(ms + NORM_EPS) * g
        partner = jnp.where(first, pltpu.roll(y, ATTN_HEAD - ATTN_HEAD // 4, 1), pltpu.roll(y, ATTN_HEAD // 4, 1))
        return y * cos + partner * sin

    qd = ATTN_Q_HEADS * ATTN_HEAD
    kd = ATTN_KV_HEADS * ATTN_HEAD
    q_pieces = [norm_rope(p_ref[0, :, h * ATTN_HEAD:(h + 1) * ATTN_HEAD], qg_ref[...]) * q_scale
                for h in range(ATTN_Q_HEADS)]
    q_ref[0] = jnp.concatenate(q_pieces, axis=-1).astype(BF16)
    k_pieces = [norm_rope(p_ref[0, :, qd + h * ATTN_HEAD:qd + (h + 1) * ATTN_HEAD], kg_ref[...])
                for h in range(ATTN_KV_HEADS)]
    k_ref[0] = jnp.concatenate(k_pieces, axis=-1).astype(BF16)
    v_ref[0] = p_ref[0, :, qd + kd:qd + 2 * kd].astype(BF16)


def _rope_tables(t, ctx_len):
    n_rows = (t - ctx_len) // GRID_W
    axis_dim = ATTN_HEAD // 2
    inv_freq = jnp.power(ROPE_THETA, -jnp.arange(0, axis_dim, 2, dtype=F32) / axis_dim)
    ar = jnp.arange(n_rows, dtype=F32)[:, None] * inv_freq
    ac = jnp.arange(GRID_W, dtype=F32)[:, None] * inv_freq
    on_rows = lambda x: jnp.broadcast_to(x[:, None, :], (n_rows, GRID_W, x.shape[-1]))
    on_cols = lambda x: jnp.broadcast_to(x[None, :, :], (n_rows, GRID_W, x.shape[-1]))
    cr, sr, cc, sc = on_rows(jnp.cos(ar)), on_rows(jnp.sin(ar)), on_cols(jnp.cos(ac)), on_cols(jnp.sin(ac))
    cos = jnp.concatenate([cr, cr, cc, cc], axis=-1).reshape(t - ctx_len, ATTN_HEAD)
    sin = jnp.concatenate([-sr, sr, -sc, sc], axis=-1).reshape(t - ctx_len, ATTN_HEAD)
    cos = jnp.concatenate([jnp.ones((ctx_len, ATTN_HEAD), F32), cos], axis=0)
    sin = jnp.concatenate([jnp.zeros((ctx_len, ATTN_HEAD), F32), sin], axis=0)
    return cos, sin


def _attn_prep(p, q_g, k_g, cos, sin):
    nb, t, n = p.shape
    tm = _row_tile(t, 640)
    qd = ATTN_Q_HEADS * ATTN_HEAD
    kd = ATTN_KV_HEADS * ATTN_HEAD
    row = lambda w: pl.BlockSpec((1, tm, w), lambda b, i: (b, i, 0))
    tab = pl.BlockSpec((tm, ATTN_HEAD), lambda b, i: (i, 0))
    vec = pl.BlockSpec((1, ATTN_HEAD), lambda b, i: (0, 0))
    return pl.pallas_call(
        _attn_prep_kernel,
        grid=(nb, t // tm),
        in_specs=[row(n), vec, vec, tab, tab],
        out_specs=[row(qd), row(kd), row(kd)],
        out_shape=[jax.ShapeDtypeStruct((nb, t, qd), BF16),
                   jax.ShapeDtypeStruct((nb, t, kd), BF16),
                   jax.ShapeDtypeStruct((nb, t, kd), BF16)],
        compiler_params=_params(("arbitrary", "arbitrary")),
        name="attn_prep",
    )(p, q_g.reshape(1, -1), k_g.reshape(1, -1), cos, sin)


def _flash_kernel(q_ref, k_ref, v_ref, o_ref, qs_ref, m_ref, l_ref, acc_ref, sa_ref, sb_ref, *, tq, tk, ctx_len, t_total):
    i = pl.program_id(2)
    g, hd = ATTN_GROUP, ATTN_HEAD
    for h in range(g):
        qs_ref[h * tq:(h + 1) * tq, :] = q_ref[0, :, h * hd:(h + 1) * hd]
    m_ref[...] = jnp.full_like(m_ref, -jnp.inf)
    l_ref[...] = jnp.zeros_like(l_ref)
    acc_ref[...] = jnp.zeros_like(acc_ref)

    def scores(start, width):
        kc = k_ref[0, pl.ds(start, width), :]
        return lax.dot_general(qs_ref[...], kc, (((1,), (1,)), ((), ())), preferred_element_type=F32)

    def update(s, start, width):
        vc = v_ref[0, pl.ds(start, width), :]
        m_old = m_ref[...]
        m_new = jnp.maximum(m_old, jnp.max(s, axis=-1, keepdims=True))
        alpha = jnp.exp2(m_old - m_new)
        p = jnp.exp2(s - jnp.tile(m_new, (1, width // LANES)))
        psum = p[:, 0:LANES]
        for n in range(1, width // LANES):
            psum = psum + p[:, n * LANES:(n + 1) * LANES]
        l_ref[...] = alpha * l_ref[...] + psum
        acc_ref[...] = alpha * acc_ref[...] + jnp.dot(p.astype(BF16), vc, preferred_element_type=F32)
        m_ref[...] = m_new

    @pl.when(i * tq < ctx_len)
    def _():
        update(scores(0, ctx_len), 0, ctx_len)

    n_kv = t_total // tk

    @pl.when(i * tq >= ctx_len)
    def _():
        sa_ref[...] = scores(0, tk)

        def pair(c):
            first = pl.multiple_of(2 * c * tk, tk)
            second = pl.multiple_of(first + tk, tk)
            third = pl.multiple_of(jnp.minimum(2 * c + 2, n_kv - 1) * tk, tk)
            sb_ref[...] = scores(second, tk)
            update(sa_ref[...], first, tk)
            sa_ref[...] = scores(third, tk)
            update(sb_ref[...], second, tk)

        n_pairs = n_kv // 2
        unroll = 4 if n_pairs % 4 == 1 else 2

        def body(c, carry):
            for u in range(unroll):
                pair(unroll * c + u)
            return carry
        lax.fori_loop(0, n_pairs // unroll, body, 0)
        for c in range(n_pairs - n_pairs % unroll, n_pairs):
            pair(c)

    out = acc_ref[...] / jnp.sum(l_ref[...], axis=-1, keepdims=True)
    o_ref[0] = jnp.concatenate([out[h * tq:(h + 1) * tq] for h in range(g)], axis=-1).astype(BF16)


def _flash_attention(q, k, v, ctx_len):
    nb, t, qd = q.shape
    tq = 256
    tk = _row_tile(t, 640, LANES)
    assert ctx_len == tq and t % tq == 0 and ctx_len % LANES == 0 and (t // tk) % 2 == 0
    gw = ATTN_GROUP * ATTN_HEAD
    return pl.pallas_call(
        functools.partial(_flash_kernel, tq=tq, tk=tk, ctx_len=ctx_len, t_total=t),
        grid=(nb, ATTN_KV_HEADS, t // tq),
        in_specs=[pl.BlockSpec((1, tq, gw), lambda b, kv, i: (b, i, kv)),
                  pl.BlockSpec((1, t, ATTN_HEAD), lambda b, kv, i: (b, 0, kv)),
                  pl.BlockSpec((1, t, ATTN_HEAD), lambda b, kv, i: (b, 0, kv))],
        out_specs=pl.BlockSpec((1, tq, gw), lambda b, kv, i: (b, i, kv)),
        out_shape=jax.ShapeDtypeStruct((nb, t, qd), BF16),
        scratch_shapes=[pltpu.VMEM((ATTN_GROUP * tq, ATTN_HEAD), BF16),
                        pltpu.VMEM((ATTN_GROUP * tq, LANES), F32),
                        pltpu.VMEM((ATTN_GROUP * tq, LANES), F32),
                        pltpu.VMEM((ATTN_GROUP * tq, ATTN_HEAD), F32),
                        pltpu.VMEM((ATTN_GROUP * tq, tk), F32),
                        pltpu.VMEM((ATTN_GROUP * tq, tk), F32)],
        compiler_params=_params(("arbitrary", "arbitrary", "arbitrary")),
        name="flash_attention",
    )(q, k, v)


def _pad_cols(w, n):
    return jnp.pad(w, ((0, 0), (0, n - w.shape[1])))


def _deltanet_layer(xc, mod, norm_g, w_in, conv_w, a_log, dt_bias, out_norm_g, w_out, ctx_len):
    nb, t, _ = xc.shape
    gate_col = DN_QKV_DIM + DN_V_DIM
    conv_w_t = jnp.pad(conv_w.T.astype(F32), ((0, 8 - SHORT_CONV), (0, 0)))
    qkv = _dn_qkv(xc, mod, norm_g, w_in[:, :DN_QKV_DIM].astype(BF16), conv_w_t, ctx_len)
    w_gate = _pad_cols(w_in[:, gate_col:][:, _dn_gate_lane_perm()], LANES)
    w_rest = jnp.concatenate([w_in[:, DN_QKV_DIM:gate_col], w_gate], axis=1).astype(BF16)
    z, gate_logits = _in_projection(xc, mod, norm_g, w_rest, MXU_CHUNK, ctx_len, bf16_cols=DN_V_DIM)
    gb = _dn_gates(gate_logits, 0, a_log, dt_bias)
    per_head = 4 * DN_GROUP
    g_rows = gb[..., :DN_K_HEADS * per_head].reshape(nb, t // CHUNK, CHUNK, DN_K_HEADS, per_head)
    g_rows = g_rows.transpose(0, 3, 1, 4, 2)
    o_f, o_b = _dn_scan(qkv, gb, g_rows)
    return _out_projection([o_f, o_b], z, 0, out_norm_g, DN_V_HEADS, DN_HEAD,
                           w_out.astype(BF16), xc, mod, ctx_len, 640)


def _gla_layer(xc, mod, norm_g, w_in, gate_w2, gate_b2, out_norm_g, w_out, ctx_len):
    n_pad = 2 * GLA_K_DIM + 2 * GLA_V_DIM + LANES
    p = _in_projection(xc, mod, norm_g, _pad_cols(w_in, n_pad).astype(BF16), MXU_CHUNK, ctx_len)
    r = GLA_GATE_RANK
    w2cat = jnp.zeros((LANES, 2 * GLA_K_DIM), F32)
    w2cat = w2cat.at[0:r, :GLA_K_DIM].set(gate_w2[0]).at[r:2 * r, GLA_K_DIM:].set(gate_w2[1]).astype(BF16)
    b2cat = gate_b2.reshape(1, 2 * GLA_K_DIM).astype(F32)
    o_f, o_b = _gla_scan(p, w2cat, b2cat)
    z_block = (2 * GLA_K_DIM + GLA_V_DIM) // GLA_V_DIM
    return _out_projection([o_f, o_b], p, z_block, out_norm_g, GLA_HEADS, GLA_HEAD_V,
                           w_out.astype(BF16), xc, mod, ctx_len, 640)


def _attention_layer(xc, mod, norm_g, w_in, q_g, k_g, w_out, rope, ctx_len):
    p = _in_projection(xc, mod, norm_g, w_in.astype(BF16), MXU_CHUNK, ctx_len)
    q, k, v = _attn_prep(p, q_g, k_g, *rope)
    o = _flash_attention(q, k, v, ctx_len)
    return _out_projection([o], None, 0, None, 0, 0, w_out.astype(BF16), xc, mod, ctx_len, 1280)


def kernel(x, c, ctx, c_ctx, ada_w, ada_b, norm_mix_g, norm_ffn_g, ffn_w1, ffn_w2, dn_w_in, dn_conv_w, dn_a_log, dn_dt_bias, dn_norm_g, dn_w_out, gla_w_in, gla_gate_w2, gla_gate_b2, gla_norm_g, gla_w_out, attn_w_in, attn_q_norm_g, attn_k_norm_g, attn_w_out):
    nb, seq, d = x.shape
    ctx_len = ctx.shape[1]
    depth = ada_w.shape[0]
    assert ctx_len == SCAN_BLOCK and seq % SCAN_BLOCK == 0 and nb < MOD_ROWS
    t = ctx_len + seq
    xc = jnp.concatenate([ctx, x], axis=1)
    cvec = jnp.zeros((MOD_ROWS, d), F32).at[:nb].set(c).at[nb].set(c_ctx)
    mods = _ada_vectors(cvec, ada_w, ada_b)
    rope = _rope_tables(t, ctx_len)
    for i in range(depth):
        mix, slot = i % 3, i // 3
        mod = mods[i]
        if mix == 0:
            xc = _deltanet_layer(xc, mod, norm_mix_g[i], dn_w_in[slot], dn_conv_w[slot], dn_a_log[slot],
                                 dn_dt_bias[slot], dn_norm_g[slot], dn_w_out[slot], ctx_len)
        elif mix == 1:
            xc = _gla_layer(xc, mod, norm_mix_g[i], gla_w_in[slot], gla_gate_w2[slot], gla_gate_b2[slot],
                            gla_norm_g[slot], gla_w_out[slot], ctx_len)
        else:
            xc = _attention_layer(xc, mod, norm_mix_g[i], attn_w_in[slot], attn_q_norm_g[slot],
                                  attn_k_norm_g[slot], attn_w_out[slot], rope, ctx_len)
        xc = _ffn(xc, mod, norm_ffn_g[i], ffn_w1[i].astype(BF16), ffn_w2[i].astype(BF16), ctx_len)
    return xc[:, ctx_len:, :]
```

```python
import functools
import math

import numpy as np
import jax
import jax.numpy as jnp
from jax import lax
from jax.experimental import pallas as pl
from jax.experimental.pallas import tpu as pltpu

F32 = jnp.float32
BF16 = jnp.bfloat16

NORM_EPS = 1e-6
GRID_W = 64
ROPE_THETA = 10000.0
SHORT_CONV = 5

DN_K_HEADS = 8
DN_V_HEADS = 16
DN_HEAD = 128
DN_GROUP = DN_V_HEADS // DN_K_HEADS
DN_K_DIM = DN_K_HEADS * DN_HEAD
DN_V_DIM = DN_V_HEADS * DN_HEAD
DN_QKV_DIM = 2 * DN_K_DIM + DN_V_DIM
DN_KH_STEP = 8

GLA_HEADS = 4
GLA_HEAD_K = 128
GLA_HEAD_V = 256
GLA_K_DIM = GLA_HEADS * GLA_HEAD_K
GLA_V_DIM = GLA_HEADS * GLA_HEAD_V
GLA_GATE_RANK = 16
GLA_GATE_NORMALIZER = 16.0
GLA_H_STEP = 4

ATTN_Q_HEADS = 8
ATTN_KV_HEADS = 2
ATTN_HEAD = 128
ATTN_GROUP = ATTN_Q_HEADS // ATTN_KV_HEADS

CHUNK = 64
SCAN_BLOCK = 256
LANES = 128
MXU_CHUNK = 1024
MOD_ROWS = 8
VMEM_LIMIT = 56 * 1024 * 1024


def _params(semantics, vmem=VMEM_LIMIT):
    return pltpu.CompilerParams(dimension_semantics=semantics, vmem_limit_bytes=vmem)


def _sigmoid(x):
    return 1.0 / (1.0 + jnp.exp(-x))


def _softplus(x):
    return jnp.maximum(x, 0.0) + jnp.log(1.0 + jnp.exp(-jnp.abs(x)))


def _split3(x):
    hi = x.astype(BF16)
    r1 = x - hi.astype(F32)
    mid = r1.astype(BF16)
    lo = (r1 - mid.astype(F32)).astype(BF16)
    return hi, mid, lo


def _dot(a, b):
    return jnp.dot(a.astype(BF16), b.astype(BF16), preferred_element_type=F32)


def _dot_sel(p_bf16, x):
    hi, mid, lo = _split3(x)
    d = lambda y: jnp.dot(p_bf16, y, preferred_element_type=F32)
    return d(hi) + d(mid) + d(lo)


def _row_tile(total, target, multiple=8):
    best = None
    for t in range(multiple, min(total, target) + 1, multiple):
        if total % t == 0:
            best = t
    assert best is not None, (total, target, multiple)
    return best


def _mod_norm(x, g, mod_ref, b, row0, ctx_len, nb, shift_idx, scale_idx):
    d = x.shape[-1]
    ms = jnp.mean(x * x, axis=-1, keepdims=True)
    y = x * lax.rsqrt(ms + NORM_EPS) * g
    rows = row0 + lax.broadcasted_iota(jnp.int32, (x.shape[0], 1), 0)
    is_ctx = rows < ctx_len

    def pick(idx):
        vx = mod_ref[pl.ds(b, 1), idx * d:(idx + 1) * d]
        vc = mod_ref[nb:nb + 1, idx * d:(idx + 1) * d]
        return jnp.where(is_ctx, vc, vx)

    return y * (1.0 + pick(scale_idx)) + pick(shift_idx), is_ctx, pick


def _ada_kernel(c_ref, w_ref, b_ref, o_ref):
    c = c_ref[...]
    s = c * _sigmoid(c)
    o_ref[0] = jnp.dot(s, w_ref[0], preferred_element_type=F32,
                       precision=lax.Precision.HIGHEST) + b_ref[0]


def _ada_vectors(cvec, ada_w, ada_b):
    depth, d, n = ada_w.shape
    tn = _row_tile(n, 1536, LANES)
    return pl.pallas_call(
        _ada_kernel,
        grid=(depth, n // tn),
        in_specs=[pl.BlockSpec((MOD_ROWS, d), lambda l, j: (0, 0)),
                  pl.BlockSpec((1, d, tn), lambda l, j: (l, 0, j)),
                  pl.BlockSpec((1, 1, tn), lambda l, j: (l, 0, j))],
        out_specs=pl.BlockSpec((1, MOD_ROWS, tn), lambda l, j: (l, 0, j)),
        out_shape=jax.ShapeDtypeStruct((depth, MOD_ROWS, n), F32),
        compiler_params=_params(("arbitrary", "arbitrary")),
        name="ada_vectors",
    )(cvec, ada_w, ada_b.reshape(depth, 1, n))


def _inproj_kernel(x_ref, mod_ref, g_ref, w_ref, *o_refs, tm, tn, ctx_len, nb):
    b = pl.program_id(0)
    i = pl.program_id(1)
    h, _, _ = _mod_norm(x_ref[0], g_ref[...], mod_ref, b, i * tm, ctx_len, nb, 0, 1)
    h = h.astype(BF16)
    first = 0
    for o_ref in o_refs:
        n = o_ref.shape[2]
        for start in range(0, n, tn):
            stop = min(start + tn, n)
            res = jnp.dot(h, w_ref[:, first + start:first + stop], preferred_element_type=F32)
            o_ref[0, :, start:stop] = res.astype(o_ref.dtype)
        first += n


def _in_projection(xc, mod, g, w_bf16, tn, ctx_len, bf16_cols=0):
    nb, t, d = xc.shape
    n = w_bf16.shape[1]
    tm = _row_tile(t, 640)
    assert bf16_cols % tn == 0
    widths = [(bf16_cols, BF16)] * (bf16_cols > 0) + [(n - bf16_cols, F32)]
    out = pl.pallas_call(
        functools.partial(_inproj_kernel, tm=tm, tn=tn, ctx_len=ctx_len, nb=nb),
        grid=(nb, t // tm),
        in_specs=[pl.BlockSpec((1, tm, d), lambda b, i: (b, i, 0)),
                  pl.BlockSpec(mod.shape, lambda b, i: (0, 0)),
                  pl.BlockSpec((1, d), lambda b, i: (0, 0)),
                  pl.BlockSpec((d, n), lambda b, i: (0, 0), pipeline_mode=pl.Buffered(1))],
        out_specs=[pl.BlockSpec((1, tm, w), lambda b, i: (b, i, 0)) for w, _ in widths],
        out_shape=[jax.ShapeDtypeStruct((nb, t, w), dt) for w, dt in widths],
        compiler_params=_params(("arbitrary", "arbitrary")),
        name="in_projection",
    )(xc, mod, g.reshape(1, d), w_bf16)
    return out if bf16_cols else out[0]


def _outproj_kernel(*refs, n_o, gated, heads, head_dim, tm, ctx_len, nb):
    o_refs = refs[:n_o]
    pos = n_o
    if gated:
        z_ref, ng_ref = refs[pos], refs[pos + 1]
        pos += 2
    w_ref, x_ref, mod_ref, out_ref = refs[pos:pos + 4]
    b = pl.program_id(0)
    i = pl.program_id(1)
    if gated:
        pieces = []
        for h in range(heads):
            sl = slice(h * head_dim, (h + 1) * head_dim)
            o = o_refs[0][0, :, sl].astype(F32)
            for r in o_refs[1:]:
                o = o + r[0, :, sl].astype(F32)
            ms = jnp.mean(o * o, axis=-1, keepdims=True)
            o = o * lax.rsqrt(ms + NORM_EPS) * ng_ref[...]
            z = z_ref[0, :, sl].astype(F32)
            pieces.append((o * (z * _sigmoid(z))).astype(BF16))
        lhs = jnp.concatenate(pieces, axis=-1)
    else:
        lhs = o_refs[0][0]
    y = jnp.dot(lhs, w_ref[...], preferred_element_type=F32)
    d = y.shape[-1]
    rows = i * tm + lax.broadcasted_iota(jnp.int32, (tm, 1), 0)
    gate = jnp.where(rows < ctx_len, mod_ref[nb:nb + 1, 2 * d:3 * d], mod_ref[pl.ds(b, 1), 2 * d:3 * d])
    out_ref[0] = x_ref[0] + gate * y


def _out_projection(o_list, z_src, z_col_block, norm_g, heads, head_dim, w_bf16, xc, mod, ctx_len, tm_target):
    nb, t, d = xc.shape
    dv = w_bf16.shape[0]
    tm = _row_tile(t, tm_target)
    gated = z_src is not None
    row_spec = lambda width, col: pl.BlockSpec((1, tm, width), lambda b, i: (b, i, col))
    in_specs = [row_spec(dv, 0) for _ in o_list]
    args = list(o_list)
    if gated:
        in_specs += [row_spec(dv, z_col_block), pl.BlockSpec((1, head_dim), lambda b, i: (0, 0))]
        args += [z_src, norm_g.reshape(1, head_dim)]
    in_specs += [pl.BlockSpec((dv, d), lambda b, i: (0, 0)), row_spec(d, 0),
                 pl.BlockSpec(mod.shape, lambda b, i: (0, 0))]
    args += [w_bf16, xc, mod]
    return pl.pallas_call(
        functools.partial(_outproj_kernel, n_o=len(o_list), gated=gated, heads=heads, head_dim=head_dim,
                          tm=tm, ctx_len=ctx_len, nb=nb),
        grid=(nb, t // tm),
        in_specs=in_specs,
        out_specs=row_spec(d, 0),
        out_shape=jax.ShapeDtypeStruct((nb, t, d), F32),
        compiler_params=_params(("arbitrary", "arbitrary")),
        name="out_projection",
    )(*args)


def _ffn_kernel(x_ref, mod_ref, g_ref, w1_ref, w2_ref, o_ref, acc_ref, *, tm, fk, ctx_len, nb):
    b = pl.program_id(0)
    i = pl.program_id(1)
    x = x_ref[0]
    h, _, pick = _mod_norm(x, g_ref[...], mod_ref, b, i * tm, ctx_len, nb, 3, 4)
    h = h.astype(BF16)
    d_ff = w1_ref.shape[1]
    for k in range(d_ff // fk):
        u = jnp.dot(h, w1_ref[:, k * fk:(k + 1) * fk], preferred_element_type=F32)
        u = jnp.maximum(u, 0.0)
        u = (u * u).astype(BF16)
        contrib = jnp.dot(u, w2_ref[k * fk:(k + 1) * fk, :], preferred_element_type=F32)
        if k == 0:
            acc_ref[...] = contrib
        else:
            acc_ref[...] += contrib
    o_ref[0] = x + pick(5) * acc_ref[...]


def _ffn(xc, mod, g, w1_bf16, w2_bf16, ctx_len):
    nb, t, d = xc.shape
    d_ff = w1_bf16.shape[1]
    tm = _row_tile(t, 640)
    return pl.pallas_call(
        functools.partial(_ffn_kernel, tm=tm, fk=512, ctx_len=ctx_len, nb=nb),
        grid=(nb, t // tm),
        in_specs=[pl.BlockSpec((1, tm, d), lambda b, i: (b, i, 0)),
                  pl.BlockSpec(mod.shape, lambda b, i: (0, 0)),
                  pl.BlockSpec((1, d), lambda b, i: (0, 0)),
                  pl.BlockSpec((d, d_ff), lambda b, i: (0, 0)),
                  pl.BlockSpec((d_ff, d), lambda b, i: (0, 0))],
        out_specs=pl.BlockSpec((1, tm, d), lambda b, i: (b, i, 0)),
        out_shape=jax.ShapeDtypeStruct((nb, t, d), F32),
        scratch_shapes=[pltpu.VMEM((tm, d), F32)],
        compiler_params=_params(("arbitrary", "arbitrary")),
        name="ffn",
    )(xc, mod, g.reshape(1, d), w1_bf16, w2_bf16)


HALO = 8


def _dn_qkv_kernel(x_ref, xp_ref, xn_ref, mod_ref, g_ref, w_ref, cw_ref, o_ref, *,
                   tm, t_total, ctx_len, nb, fix_tile, fix_row):
    b = pl.program_id(0)
    i = pl.program_id(1)
    row0 = i * tm
    n_ext = tm + 2 * HALO
    pad = SHORT_CONV // 2
    x_ext = jnp.concatenate([xp_ref[0], x_ref[0], xn_ref[0]], axis=0)
    h, _, _ = _mod_norm(x_ext, g_ref[...], mod_ref, b, row0 - HALO, ctx_len, nb, 0, 1)
    t = row0 - HALO + lax.broadcasted_iota(jnp.int32, (n_ext, 1), 0)
    nearest = jnp.clip(t, row0, row0 + tm - 1)
    valid = (jnp.where(t >= 0, 1, 0) * jnp.where(t < t_total, 1, 0)
             * jnp.where(jnp.where(t >= ctx_len, 1, 0) == jnp.where(nearest >= ctx_len, 1, 0), 1, 0))
    h = jnp.where(valid > 0, h, 0.0).astype(BF16)
    if fix_row is not None:
        rr = lax.broadcasted_iota(jnp.int32, (2 * HALO, 1), 0)
        crosses = {}
        for d in range(-pad, pad + 1):
            if d:
                inside = jnp.where(rr + d >= 0, 1, 0) * jnp.where(rr + d < 2 * HALO, 1, 0)
                other = jnp.where(jnp.where(rr < HALO, 1, 0) != jnp.where(rr + d < HALO, 1, 0), 1, 0)
                crosses[d] = jnp.where(i == fix_tile, inside * other, 0) > 0
    n = w_ref.shape[1]
    for start in range(0, n, MXU_CHUNK):
        res = jnp.dot(h, w_ref[:, start:start + MXU_CHUNK], preferred_element_type=F32)
        for hd in range(MXU_CHUNK // DN_HEAD):
            col0 = start + hd * DN_HEAD
            cols = slice(col0, col0 + DN_HEAD)
            r = res[:, hd * DN_HEAD:(hd + 1) * DN_HEAD]
            acc = None
            for tap in range(SHORT_CONV):
                d = tap - pad
                xs = (r if d == 0 else pltpu.roll(r, (-d) % n_ext, 0))[HALO:HALO + tm]
                term = xs * cw_ref[tap:tap + 1, cols]
                acc = term if acc is None else acc + term
            if fix_row is not None:
                slab = r[fix_row:fix_row + 2 * HALO]
                wrong = None
                for d, mask in crosses.items():
                    term = jnp.where(mask, pltpu.roll(slab, (-d) % (2 * HALO), 0), 0.0) * cw_ref[d + pad:d + pad + 1, cols]
                    wrong = term if wrong is None else wrong + term
                acc = jnp.concatenate([acc[:fix_row - HALO], acc[fix_row - HALO:fix_row + HALO] - wrong,
                                       acc[fix_row + HALO:]], axis=0)
            half = 0.5 * acc
            y = half + half * jnp.tanh(half)
            if col0 < 2 * DN_K_DIM:
                ss = jnp.sum(y * y, axis=-1, keepdims=True)
                y = y * (lax.rsqrt(ss + NORM_EPS) * (DN_HEAD ** -0.5 if col0 < DN_K_DIM else 1.0))
            o_ref[0, :, cols] = y


def _dn_qkv(xc, mod, g, w_qkv_bf16, conv_w_t, ctx_len):
    nb, t, d = xc.shape
    n = w_qkv_bf16.shape[1]
    tm = _row_tile(t, 640)
    hb = tm // HALO
    last = t // HALO - 1
    fix_tile, fix_row = (ctx_len // tm, ctx_len % tm) if ctx_len % tm else (None, None)
    assert fix_row is None or (fix_row % HALO == 0 and HALO <= fix_row <= tm - HALO)
    return pl.pallas_call(
        functools.partial(_dn_qkv_kernel, tm=tm, t_total=t, ctx_len=ctx_len, nb=nb,
                          fix_tile=fix_tile, fix_row=fix_row),
        grid=(nb, t // tm),
        in_specs=[pl.BlockSpec((1, tm, d), lambda b, i: (b, i, 0)),
                  pl.BlockSpec((1, HALO, d), lambda b, i: (b, jnp.maximum(i * hb - 1, 0), 0)),
                  pl.BlockSpec((1, HALO, d), lambda b, i: (b, jnp.minimum((i + 1) * hb, last), 0)),
                  pl.BlockSpec(mod.shape, lambda b, i: (0, 0)),
                  pl.BlockSpec((1, d), lambda b, i: (0, 0)),
                  pl.BlockSpec((d, n), lambda b, i: (0, 0), pipeline_mode=pl.Buffered(1)),
                  pl.BlockSpec((8, n), lambda b, i: (0, 0))],
        out_specs=pl.BlockSpec((1, tm, n), lambda b, i: (b, i, 0)),
        out_shape=jax.ShapeDtypeStruct((nb, t, n), F32),
        compiler_params=_params(("arbitrary", "arbitrary")),
        name="dn_qkv",
    )(xc, xc, xc, mod, g.reshape(1, d), w_qkv_bf16, conv_w_t)


def _dn_gate_kernel(ab_ref, par_ref, o_ref, *, tm):
    x = ab_ref[0]
    g = -jnp.exp(par_ref[0:1, :]) * _softplus(x + par_ref[1:2, :])
    beta = _sigmoid(x)
    lane = lax.broadcasted_iota(jnp.int32, (1, LANES), 1)
    used = lane < 4 * DN_V_HEADS
    is_beta = (lane & (2 * DN_GROUP)) != 0
    is_reverse = (lane & DN_GROUP) != 0
    r = lax.broadcasted_iota(jnp.int32, (CHUNK, CHUNK), 0)
    c = lax.broadcasted_iota(jnp.int32, (CHUNK, CHUNK), 1)
    lower = jnp.where(c <= r, 1.0, 0.0).astype(BF16)
    upper = jnp.where(c >= r, 1.0, 0.0).astype(BF16)
    for k in range(tm // CHUNK):
        rows = slice(k * CHUNK, (k + 1) * CHUNK)
        gk = g[rows]
        fwd = _dot_sel(lower, gk)
        bwd = _dot_sel(upper, gk)
        o_ref[0, rows, :] = jnp.where(used, jnp.where(is_beta, beta[rows], jnp.where(is_reverse, bwd, fwd)), 0.0)


def _dn_gate_lane_perm():
    perm = np.zeros(4 * DN_V_HEADS, np.int32)
    for kh in range(DN_K_HEADS):
        for kind in range(2):
            for d in range(2):
                for j in range(DN_GROUP):
                    lane = ((kh * 2 + kind) * 2 + d) * DN_GROUP + j
                    perm[lane] = (kind * 2 + d) * DN_V_HEADS + kh * DN_GROUP + j
    return perm


def _dn_gates(p, ab_col_block, a_log, dt_bias):
    nb, t, _ = p.shape
    tm = _row_tile(t, 1280, CHUNK)
    perm = _dn_gate_lane_perm()
    on_lanes = lambda v: jnp.concatenate([v.reshape(-1).astype(F32), jnp.zeros(2 * DN_V_HEADS, F32)])[perm]
    par = jnp.zeros((8, LANES), F32)
    par = par.at[0, :4 * DN_V_HEADS].set(on_lanes(a_log))
    par = par.at[1, :4 * DN_V_HEADS].set(on_lanes(dt_bias))
    return pl.pallas_call(
        functools.partial(_dn_gate_kernel, tm=tm),
        grid=(nb, t // tm),
        in_specs=[pl.BlockSpec((1, tm, LANES), lambda b, i: (b, i, ab_col_block)),
                  pl.BlockSpec((8, LANES), lambda b, i: (0, 0))],
        out_specs=pl.BlockSpec((1, tm, LANES), lambda b, i: (b, i, 0)),
        out_shape=jax.ShapeDtypeStruct((nb, t, LANES), F32),
        compiler_params=_params(("arbitrary", "arbitrary")),
        name="dn_gates",
    )(p, par)


def _bdot(a, b):
    return jnp.einsum("nij,njk->nik", a.astype(BF16), b.astype(BF16), preferred_element_type=F32)


def _bdot_nt(a, b):
    return jnp.einsum("nid,njd->nij", a.astype(BF16), b.astype(BF16), preferred_element_type=F32)


def _bdot_tn(a, b):
    return jnp.einsum("nci,ncj->nij", a.astype(BF16), b.astype(BF16), preferred_element_type=F32)


N_LEVELS = int(math.log2(CHUNK))


def _unit_triangular_inverse(parts, eye):
    coupling = lambda lv: jnp.concatenate([jnp.where(masks[lv], m, 0.0) for m, masks in parts], axis=0)
    x = eye - coupling(0)
    for lv in range(1, N_LEVELS):
        x = x - _bdot(x, _bdot(coupling(lv), x))
    return x


def _coupling_masks(ri, ci, reverse):
    ti, tj = (CHUNK - 1 - ri, CHUNK - 1 - ci) if reverse else (ri, ci)
    masks = []
    for lv in range(N_LEVELS):
        bi, bj = lax.shift_right_logical(ti, lv), lax.shift_right_logical(tj, lv)
        masks.append(jnp.where((bi & 1) == 1, bi - 1, -1) == bj)
    return masks


def _dn_scan_kernel(qf_ref, kf_ref, vf_ref, gf_ref, gtf_ref,
                    qr_ref, kr_ref, vr_ref, gr_ref, gtr_ref,
                    of_ref, ob_ref, s_ref):
    @pl.when(pl.program_id(2) == 0)
    def _():
        s_ref[...] = jnp.zeros_like(s_ref)

    ri = lax.broadcasted_iota(jnp.int32, (CHUNK, CHUNK), 0)
    ci = lax.broadcasted_iota(jnp.int32, (CHUNK, CHUNK), 1)
    eye = jnp.where(ri == ci, 1.0, 0.0)
    nc = SCAN_BLOCK // CHUNK
    chunk_rows = [slice(c * CHUNK, (c + 1) * CHUNK) for c in range(nc)]
    dirs = ((qf_ref, kf_ref, vf_ref, gf_ref, gtf_ref, False),
            (qr_ref, kr_ref, vr_ref, gr_ref, gtr_ref, True))
    m_parts, rhs_l, a_l, qg_l, kt_l, egl_l = [], [], [], [], [], []
    head_cols = lambda n: slice(n * DN_HEAD, (n + 1) * DN_HEAD)
    for d, (q_ref, k_ref, v_ref, g_ref, gt_ref, reverse) in enumerate(dirs):
        incl = (ci >= ri) if reverse else (ci <= ri)
        strict = (ci > ri) if reverse else (ci < ri)
        last = 0 if reverse else CHUNK - 1
        m_l = []
        per_head = 4 * DN_GROUP
        first_lane = pl.program_id(1) * (DN_KH_STEP * per_head)
        g_all = pltpu.roll(g_ref[0], jnp.where(first_lane == 0, 0, LANES - first_lane), 1)
        for hh in range(DN_KH_STEP):
            q = jnp.stack([q_ref[0, r, head_cols(hh)] for r in chunk_rows])
            k = jnp.stack([k_ref[0, r, head_cols(hh)] for r in chunk_rows])
            qk_kk = _bdot_nt(jnp.concatenate([q, k], axis=1), k)
            qk, kk = qk_kk[:, :CHUNK], qk_kk[:, CHUNK:]
            for j in range(DN_GROUP):
                col = 2 * d + j
                lane = hh * per_head + col
                gc = jnp.stack([g_all[r, lane:lane + 1] for r in chunk_rows])
                bc = jnp.stack([g_all[r, lane + 2 * DN_GROUP:lane + 2 * DN_GROUP + 1] for r in chunk_rows])
                gr = jnp.stack([gt_ref[0, hh, c, col:col + 1, :] for c in range(nc)])
                v = jnp.stack([v_ref[0, r, head_cols(hh * DN_GROUP + j)] for r in chunk_rows])
                decay = jnp.where(incl, jnp.exp(jnp.where(incl, gc - gr, 0.0)), 0.0)
                eg = jnp.exp(gc)
                gl = gc[:, last:last + 1, :]
                m_l.append(jnp.where(strict, bc * kk * decay, 0.0))
                rhs_l.append(jnp.concatenate([k * (bc * eg), v * bc], axis=-1))
                a_l.append(qk * decay)
                qg_l.append(q * eg)
                kt_l.append(k * jnp.exp(gl - gc))
                egl_l.append(jnp.exp(gl))
        m_parts.append((jnp.concatenate(m_l, axis=0), _coupling_masks(ri, ci, reverse)))
    cat = lambda xs: jnp.concatenate(xs, axis=0)
    a, kt = cat(a_l), cat(kt_l)
    wu = _bdot(_unit_triangular_inverse(m_parts, eye), cat(rhs_l))
    kb = _bdot_tn(kt, wu)
    qo = _bdot(a, wu)
    qeff = cat(qg_l) - qo[:, :, :DN_HEAD]
    egl = cat(egl_l)
    state = s_ref[...]
    o_refs = (of_ref, ob_ref)
    vh_step = DN_KH_STEP * DN_GROUP
    for step in range(nc):
        chunk_of = [step if d == 0 else nc - 1 - step for d in range(2) for _ in range(vh_step)]
        idx = [ch * nc + c for ch, c in enumerate(chunk_of)]
        pick = lambda x: jnp.stack([x[n] for n in idx])
        kb_s = pick(kb)
        qs_ks = _bdot(jnp.concatenate([pick(qeff), kb_s[:, :, :DN_HEAD]], axis=1), state)
        o = qs_ks[:, :CHUNK] + pick(qo)[:, :, DN_HEAD:]
        state = pick(egl) * state + kb_s[:, :, DN_HEAD:] - qs_ks[:, CHUNK:]
        for ch, c in enumerate(chunk_of):
            o_refs[ch // vh_step][0, chunk_rows[c], head_cols(ch % vh_step)] = o[ch].astype(BF16)
    s_ref[...] = state


def _dn_scan(qkv, gates, g_rows):
    nb, t, _ = qkv.shape
    n_blocks = t // SCAN_BLOCK
    n_chunks = SCAN_BLOCK // CHUNK
    ks = DN_KH_STEP
    groups = DN_K_HEADS // ks
    fwd = lambda s: s
    bwd = lambda s: jnp.where(s == 0, 0, n_blocks - s)

    def specs(order):
        return [pl.BlockSpec((1, SCAN_BLOCK, ks * DN_HEAD), lambda b, h, s: (b, order(s), h)),
                pl.BlockSpec((1, SCAN_BLOCK, ks * DN_HEAD), lambda b, h, s: (b, order(s), groups + h)),
                pl.BlockSpec((1, SCAN_BLOCK, ks * DN_GROUP * DN_HEAD), lambda b, h, s: (b, order(s), groups + h)),
                pl.BlockSpec((1, SCAN_BLOCK, LANES), lambda b, h, s: (b, order(s), 0)),
                pl.BlockSpec((1, ks, n_chunks, 8, CHUNK), lambda b, h, s: (b, h, order(s), 0, 0))]

    out_spec = lambda order: pl.BlockSpec((1, SCAN_BLOCK, ks * DN_GROUP * DN_HEAD), lambda b, h, s: (b, order(s), h))
    o_shape = jax.ShapeDtypeStruct((nb, t, DN_V_DIM), BF16)
    return pl.pallas_call(
        _dn_scan_kernel,
        grid=(nb, groups, n_blocks),
        in_specs=specs(fwd) + specs(bwd),
        out_specs=[out_spec(fwd), out_spec(bwd)],
        out_shape=[o_shape, o_shape],
        scratch_shapes=[pltpu.VMEM((2 * ks * DN_GROUP, DN_HEAD, DN_HEAD), F32)],
        compiler_params=_params(("arbitrary", "arbitrary", "arbitrary")),
        name="dn_scan",
    )(qkv, qkv, qkv, gates, g_rows, qkv, qkv, qkv, gates, g_rows)


_GLA_LEVELS = (32, 16, 8, 4, 2, 1)


def _gla_tables(reverse):
    idx = np.arange(CHUNK)
    tau = (CHUNK - 1 - idx) if reverse else idx
    ti, tk = tau[:, None], tau[None, :]
    groups = [tk <= ti, tk > ti]
    masks = []
    for s in _GLA_LEVELS:
        bi, bk = ti // s, tk // s
        if s > 1:
            groups.append((bk == bi) & (tk <= ti) & (tk > bi * s))
        groups.append(((bk == bi) & (tk > ti)) | (tk == (bi + 1) * s))
        masks.append((bi % 2 == 1) & (bk == bi - 1))
    masks.append(ti == tk)
    sel = np.concatenate(groups, axis=0).astype(np.float32)
    return jnp.asarray(sel, BF16), jnp.asarray(np.stack(masks).astype(np.float32))


def _gla_scan_kernel(qf_ref, kf_ref, vf_ref, lf_ref, wf_ref, bf_ref, self_ref, mf_ref,
                     qr_ref, kr_ref, vr_ref, lr_ref, wr_ref, br_ref, selr_ref, mr_ref,
                     of_ref, ob_ref, s_ref):
    @pl.when(pl.program_id(2) == 0)
    def _():
        s_ref[...] = jnp.zeros_like(s_ref)

    nc = SCAN_BLOCK // CHUNK
    n_lev = len(_GLA_LEVELS)
    chunk_rows = [slice(c * CHUNK, (c + 1) * CHUNK) for c in range(nc)]
    dirs = ((qf_ref, kf_ref, vf_ref, lf_ref, wf_ref, bf_ref, self_ref, mf_ref, False),
            (qr_ref, kr_ref, vr_ref, lr_ref, wr_ref, br_ref, selr_ref, mr_ref, True))
    hs = GLA_H_STEP
    kcols = lambda hh: slice(hh * GLA_HEAD_K, (hh + 1) * GLA_HEAD_K)
    vcols = lambda hh: slice(hh * GLA_HEAD_V, (hh + 1) * GLA_HEAD_V)
    ql, kl, qg_l, kt_l, e_l, v_l = [], [], [], [], [], []
    for q_ref, k_ref, v_ref, l_ref, w_ref, b_ref, sel_ref, m_ref, reverse in dirs:
        last = 0 if reverse else CHUNK - 1
        logits = _dot(l_ref[0], w_ref[...]) + b_ref[...]
        gk_all = (jnp.minimum(logits, 0.0) - jnp.log(1.0 + jnp.exp(-jnp.abs(logits)))) * (1.0 / GLA_GATE_NORMALIZER)
        for hh in range(hs):
            for rows in chunk_rows:
                q = q_ref[0, rows, kcols(hh)] * (GLA_HEAD_K ** -0.5)
                k = k_ref[0, rows, kcols(hh)]
                gk = gk_all[rows, kcols(hh)]
                hi = gk.astype(BF16)
                mid = (gk - hi.astype(F32)).astype(BF16)
                both = jnp.dot(sel_ref[...], jnp.concatenate([hi, mid], axis=-1), preferred_element_type=F32)
                sums = both[:, :GLA_HEAD_K] + both[:, GLA_HEAD_K:]
                part = lambda n: sums[n * CHUNK:(n + 1) * CHUNK]
                bcum, tail = part(0), part(1)
                ql += [q * jnp.exp(part(2 + 2 * lv)) for lv in range(n_lev - 1)] + [q, q]
                kl += [k * jnp.exp(part(3 + 2 * lv)) for lv in range(n_lev - 1)] + [k * jnp.exp(part(2 * n_lev)), k]
                qg_l.append(q * jnp.exp(bcum))
                kt_l.append(k * jnp.exp(tail))
                e_l.append(jnp.exp(bcum[last:last + 1, :]))
                v_l.append(v_ref[0, rows, vcols(hh)])
    scores = _bdot_nt(jnp.stack(ql), jnp.stack(kl))
    a_l = []
    for n in range(2 * hs * nc):
        m_ref = dirs[n // (hs * nc)][7]
        a = m_ref[0] * scores[n * (n_lev + 1)]
        for lv in range(1, n_lev + 1):
            a = a + m_ref[lv] * scores[n * (n_lev + 1) + lv]
        a_l.append(a)
    v = jnp.stack(v_l)
    x = _bdot_tn(v, jnp.stack(kt_l))
    s_l = [None] * (2 * hs * nc)
    for ch in range(2 * hs):
        state = s_ref[ch]
        for c in (range(nc - 1, -1, -1) if dirs[ch // hs][8] else range(nc)):
            n = ch * nc + c
            s_l[n] = state
            state = state * e_l[n] + x[n]
        s_ref[ch] = state
    o = _bdot(jnp.stack(a_l), v) + _bdot_nt(jnp.stack(qg_l), jnp.stack(s_l))
    for n in range(2 * hs * nc):
        ch, c = divmod(n, nc)
        (of_ref, ob_ref)[ch // hs][0, chunk_rows[c], vcols(ch % hs)] = o[n].astype(BF16)


def _gla_scan(p, w2cat, b2cat):
    nb, t, _ = p.shape
    n_blocks = t // SCAN_BLOCK
    hs = GLA_H_STEP
    h = GLA_HEADS // hs
    low_block = (2 * GLA_K_DIM + 2 * GLA_V_DIM) // LANES
    fwd = lambda s: s
    bwd = lambda s: jnp.where(s == 0, 0, n_blocks - s)
    assert _GLA_LEVELS[-1] == 1
    n_sel = (1 + 2 * len(_GLA_LEVELS)) * CHUNK
    n_mask = len(_GLA_LEVELS) + 1

    def specs(order, d):
        return [pl.BlockSpec((1, SCAN_BLOCK, hs * GLA_HEAD_K), lambda b, hh, s: (b, order(s), hh)),
                pl.BlockSpec((1, SCAN_BLOCK, hs * GLA_HEAD_K), lambda b, hh, s: (b, order(s), h + hh)),
                pl.BlockSpec((1, SCAN_BLOCK, hs * GLA_HEAD_V), lambda b, hh, s: (b, order(s), h + hh)),
                pl.BlockSpec((1, SCAN_BLOCK, LANES), lambda b, hh, s: (b, order(s), low_block)),
                pl.BlockSpec((LANES, hs * GLA_HEAD_K), lambda b, hh, s: (0, d * h + hh)),
                pl.BlockSpec((1, hs * GLA_HEAD_K), lambda b, hh, s: (0, d * h + hh)),
                pl.BlockSpec((n_sel, CHUNK), lambda b, hh, s: (0, 0)),
                pl.BlockSpec((n_mask, CHUNK, CHUNK), lambda b, hh, s: (0, 0, 0))]

    out_spec = lambda order: pl.BlockSpec((1, SCAN_BLOCK, hs * GLA_HEAD_V), lambda b, hh, s: (b, order(s), hh))
    o_shape = jax.ShapeDtypeStruct((nb, t, GLA_V_DIM), BF16)
    sel_f, mask_f = _gla_tables(False)
    sel_r, mask_r = _gla_tables(True)
    return pl.pallas_call(
        _gla_scan_kernel,
        grid=(nb, h, n_blocks),
        in_specs=specs(fwd, 0) + specs(bwd, 1),
        out_specs=[out_spec(fwd), out_spec(bwd)],
        out_shape=[o_shape, o_shape],
        scratch_shapes=[pltpu.VMEM((2 * hs, GLA_HEAD_V, GLA_HEAD_K), F32)],
        compiler_params=_params(("arbitrary", "arbitrary", "arbitrary")),
        name="gla_scan",
    )(p, p, p, p, w2cat, b2cat, sel_f, mask_f, p, p, p, p, w2cat, b2cat, sel_r, mask_r)


def _attn_prep_kernel(p_ref, qg_ref, kg_ref, cos_ref, sin_ref, q_ref, k_ref, v_ref):
    cos = cos_ref[...]
    sin = sin_ref[...]
    lane = lax.broadcasted_iota(jnp.int32, (1, ATTN_HEAD), 1)
    first = (lane % (ATTN_HEAD // 2)) < (ATTN_HEAD // 4)
    q_scale = ATTN_HEAD ** -0.5 * math.log2(math.e)

    def norm_rope(x, g):
        ms = jnp.mean(x * x, axis=-1, keepdims=True)
        y = x * lax.rsqrt(ms + NORM_EPS) * g
        partner = jnp.where(first, pltpu.roll(y, ATTN_HEAD - ATTN_HEAD // 4, 1), pltpu.roll(y, ATTN_HEAD // 4, 1))
        return y * cos + partner * sin

    qd = ATTN_Q_HEADS * ATTN_HEAD
    kd = ATTN_KV_HEADS * ATTN_HEAD
    q_pieces = [norm_rope(p_ref[0, :, h * ATTN_HEAD:(h + 1) * ATTN_HEAD], qg_ref[...]) * q_scale
                for h in range(ATTN_Q_HEADS)]
    q_ref[0] = jnp.concatenate(q_pieces, axis=-1).astype(BF16)
    k_pieces = [norm_rope(p_ref[0, :, qd + h * ATTN_HEAD:qd + (h + 1) * ATTN_HEAD], kg_ref[...])
                for h in range(ATTN_KV_HEADS)]
    k_ref[0] = jnp.concatenate(k_pieces, axis=-1).astype(BF16)
    v_ref[0] = p_ref[0, :, qd + kd:qd + 2 * kd].astype(BF16)


def _rope_tables(t, ctx_len):
    n_rows = (t - ctx_len) // GRID_W
    axis_dim = ATTN_HEAD // 2
    inv_freq = jnp.power(ROPE_THETA, -jnp.arange(0, axis_dim, 2, dtype=F32) / axis_dim)
    ar = jnp.arange(n_rows, dtype=F32)[:, None] * inv_freq
    ac = jnp.arange(GRID_W, dtype=F32)[:, None] * inv_freq
    on_rows = lambda x: jnp.broadcast_to(x[:, None, :], (n_rows, GRID_W, x.shape[-1]))
    on_cols = lambda x: jnp.broadcast_to(x[None, :, :], (n_rows, GRID_W, x.shape[-1]))
    cr, sr, cc, sc = on_rows(jnp.cos(ar)), on_rows(jnp.sin(ar)), on_cols(jnp.cos(ac)), on_cols(jnp.sin(ac))
    cos = jnp.concatenate([cr, cr, cc, cc], axis=-1).reshape(t - ctx_len, ATTN_HEAD)
    sin = jnp.concatenate([-sr, sr, -sc, sc], axis=-1).reshape(t - ctx_len, ATTN_HEAD)
    cos = jnp.concatenate([jnp.ones((ctx_len, ATTN_HEAD), F32), cos], axis=0)
    sin = jnp.concatenate([jnp.zeros((ctx_len, ATTN_HEAD), F32), sin], axis=0)
    return cos, sin


def _attn_prep(p, q_g, k_g, cos, sin):
    nb, t, n = p.shape
    tm = _row_tile(t, 640)
    qd = ATTN_Q_HEADS * ATTN_HEAD
    kd = ATTN_KV_HEADS * ATTN_HEAD
    row = lambda w: pl.BlockSpec((1, tm, w), lambda b, i: (b, i, 0))
    tab = pl.BlockSpec((tm, ATTN_HEAD), lambda b, i: (i, 0))
    vec = pl.BlockSpec((1, ATTN_HEAD), lambda b, i: (0, 0))
    return pl.pallas_call(
        _attn_prep_kernel,
        grid=(nb, t // tm),
        in_specs=[row(n), vec, vec, tab, tab],
        out_specs=[row(qd), row(kd), row(kd)],
        out_shape=[jax.ShapeDtypeStruct((nb, t, qd), BF16),
                   jax.ShapeDtypeStruct((nb, t, kd), BF16),
                   jax.ShapeDtypeStruct((nb, t, kd), BF16)],
        compiler_params=_params(("arbitrary", "arbitrary")),
        name="attn_prep",
    )(p, q_g.reshape(1, -1), k_g.reshape(1, -1), cos, sin)


def _flash_kernel(q_ref, k_ref, v_ref, o_ref, qs_ref, m_ref, l_ref, acc_ref, sa_ref, sb_ref, *, tq, tk, ctx_len, t_total):
    i = pl.program_id(2)
    g, hd = ATTN_GROUP, ATTN_HEAD
    for h in range(g):
        qs_ref[h * tq:(h + 1) * tq, :] = q_ref[0, :, h * hd:(h + 1) * hd]
    m_ref[...] = jnp.full_like(m_ref, -jnp.inf)
    l_ref[...] = jnp.zeros_like(l_ref)
    acc_ref[...] = jnp.zeros_like(acc_ref)

    def scores(start, width):
        kc = k_ref[0, pl.ds(start, width), :]
        return lax.dot_general(qs_ref[...], kc, (((1,), (1,)), ((), ())), preferred_element_type=F32)

    def update(s, start, width):
        vc = v_ref[0, pl.ds(start, width), :]
        m_old = m_ref[...]
        m_new = jnp.maximum(m_old, jnp.max(s, axis=-1, keepdims=True))
        alpha = jnp.exp2(m_old - m_new)
        p = jnp.exp2(s - jnp.tile(m_new, (1, width // LANES)))
        psum = p[:, 0:LANES]
        for n in range(1, width // LANES):
            psum = psum + p[:, n * LANES:(n + 1) * LANES]
        l_ref[...] = alpha * l_ref[...] + psum
        acc_ref[...] = alpha * acc_ref[...] + jnp.dot(p.astype(BF16), vc, preferred_element_type=F32)
        m_ref[...] = m_new

    @pl.when(i * tq < ctx_len)
    def _():
        update(scores(0, ctx_len), 0, ctx_len)

    n_kv = t_total // tk

    @pl.when(i * tq >= ctx_len)
    def _():
        sa_ref[...] = scores(0, tk)

        def pair(c):
            first = pl.multiple_of(2 * c * tk, tk)
            second = pl.multiple_of(first + tk, tk)
            third = pl.multiple_of(jnp.minimum(2 * c + 2, n_kv - 1) * tk, tk)
            sb_ref[...] = scores(second, tk)
            update(sa_ref[...], first, tk)
            sa_ref[...] = scores(third, tk)
            update(sb_ref[...], second, tk)

        n_pairs = n_kv // 2
        unroll = 6 if n_pairs % 6 == 1 else (4 if n_pairs % 4 == 1 else 2)

        def body(c, carry):
            for u in range(unroll):
                pair(unroll * c + u)
            return carry
        lax.fori_loop(0, n_pairs // unroll, body, 0)
        for c in range(n_pairs - n_pairs % unroll, n_pairs):
            pair(c)

    out = acc_ref[...] / jnp.sum(l_ref[...], axis=-1, keepdims=True)
    o_ref[0] = jnp.concatenate([out[h * tq:(h + 1) * tq] for h in range(g)], axis=-1).astype(BF16)


def _flash_attention(q, k, v, ctx_len):
    nb, t, qd = q.shape
    tq = 256
    tk = _row_tile(t, 640, LANES)
    assert ctx_len == tq and t % tq == 0 and ctx_len % LANES == 0 and (t // tk) % 2 == 0
    gw = ATTN_GROUP * ATTN_HEAD
    return pl.pallas_call(
        functools.partial(_flash_kernel, tq=tq, tk=tk, ctx_len=ctx_len, t_total=t),
        grid=(nb, ATTN_KV_HEADS, t // tq),
        in_specs=[pl.BlockSpec((1, tq, gw), lambda b, kv, i: (b, i, kv)),
                  pl.BlockSpec((1, t, ATTN_HEAD), lambda b, kv, i: (b, 0, kv)),
                  pl.BlockSpec((1, t, ATTN_HEAD), lambda b, kv, i: (b, 0, kv))],
        out_specs=pl.BlockSpec((1, tq, gw), lambda b, kv, i: (b, i, kv)),
        out_shape=jax.ShapeDtypeStruct((nb, t, qd), BF16),
        scratch_shapes=[pltpu.VMEM((ATTN_GROUP * tq, ATTN_HEAD), BF16),
                        pltpu.VMEM((ATTN_GROUP * tq, LANES), F32),
                        pltpu.VMEM((ATTN_GROUP * tq, LANES), F32),
                        pltpu.VMEM((ATTN_GROUP * tq, ATTN_HEAD), F32),
                        pltpu.VMEM((ATTN_GROUP * tq, tk), F32),
                        pltpu.VMEM((ATTN_GROUP * tq, tk), F32)],
        compiler_params=_params(("arbitrary", "arbitrary", "arbitrary")),
        name="flash_attention",
    )(q, k, v)


def _pad_cols(w, n):
    return jnp.pad(w, ((0, 0), (0, n - w.shape[1])))


def _deltanet_layer(xc, mod, norm_g, w_in, conv_w, a_log, dt_bias, out_norm_g, w_out, ctx_len):
    nb, t, _ = xc.shape
    gate_col = DN_QKV_DIM + DN_V_DIM
    conv_w_t = jnp.pad(conv_w.T.astype(F32), ((0, 8 - SHORT_CONV), (0, 0)))
    qkv = _dn_qkv(xc, mod, norm_g, w_in[:, :DN_QKV_DIM].astype(BF16), conv_w_t, ctx_len)
    w_gate = _pad_cols(w_in[:, gate_col:][:, _dn_gate_lane_perm()], LANES)
    w_rest = jnp.concatenate([w_in[:, DN_QKV_DIM:gate_col], w_gate], axis=1).astype(BF16)
    z, gate_logits = _in_projection(xc, mod, norm_g, w_rest, MXU_CHUNK, ctx_len, bf16_cols=DN_V_DIM)
    gb = _dn_gates(gate_logits, 0, a_log, dt_bias)
    per_head = 4 * DN_GROUP
    g_rows = gb[..., :DN_K_HEADS * per_head].reshape(nb, t // CHUNK, CHUNK, DN_K_HEADS, per_head)
    g_rows = g_rows.transpose(0, 3, 1, 4, 2)
    o_f, o_b = _dn_scan(qkv, gb, g_rows)
    return _out_projection([o_f, o_b], z, 0, out_norm_g, DN_V_HEADS, DN_HEAD,
                           w_out.astype(BF16), xc, mod, ctx_len, 640)


def _gla_layer(xc, mod, norm_g, w_in, gate_w2, gate_b2, out_norm_g, w_out, ctx_len):
    n_pad = 2 * GLA_K_DIM + 2 * GLA_V_DIM + LANES
    p = _in_projection(xc, mod, norm_g, _pad_cols(w_in, n_pad).astype(BF16), MXU_CHUNK, ctx_len)
    r = GLA_GATE_RANK
    w2cat = jnp.zeros((LANES, 2 * GLA_K_DIM), F32)
    w2cat = w2cat.at[0:r, :GLA_K_DIM].set(gate_w2[0]).at[r:2 * r, GLA_K_DIM:].set(gate_w2[1]).astype(BF16)
    b2cat = gate_b2.reshape(1, 2 * GLA_K_DIM).astype(F32)
    o_f, o_b = _gla_scan(p, w2cat, b2cat)
    z_block = (2 * GLA_K_DIM + GLA_V_DIM) // GLA_V_DIM
    return _out_projection([o_f, o_b], p, z_block, out_norm_g, GLA_HEADS, GLA_HEAD_V,
                           w_out.astype(BF16), xc, mod, ctx_len, 640)


def _attention_layer(xc, mod, norm_g, w_in, q_g, k_g, w_out, rope, ctx_len):
    p = _in_projection(xc, mod, norm_g, w_in.astype(BF16), MXU_CHUNK, ctx_len)
    q, k, v = _attn_prep(p, q_g, k_g, *rope)
    o = _flash_attention(q, k, v, ctx_len)
    return _out_projection([o], None, 0, None, 0, 0, w_out.astype(BF16), xc, mod, ctx_len, 1280)


def kernel(x, c, ctx, c_ctx, ada_w, ada_b, norm_mix_g, norm_ffn_g, ffn_w1, ffn_w2, dn_w_in, dn_conv_w, dn_a_log, dn_dt_bias, dn_norm_g, dn_w_out, gla_w_in, gla_gate_w2, gla_gate_b2, gla_norm_g, gla_w_out, attn_w_in, attn_q_norm_g, attn_k_norm_g, attn_w_out):
    nb, seq, d = x.shape
    ctx_len = ctx.shape[1]
    depth = ada_w.shape[0]
    assert ctx_len == SCAN_BLOCK and seq % SCAN_BLOCK == 0 and nb < MOD_ROWS
    t = ctx_len + seq
    xc = jnp.concatenate([ctx, x], axis=1)
    cvec = jnp.zeros((MOD_ROWS, d), F32).at[:nb].set(c).at[nb].set(c_ctx)
    mods = _ada_vectors(cvec, ada_w, ada_b)
    rope = _rope_tables(t, ctx_len)
    for i in range(depth):
        mix, slot = i % 3, i // 3
        mod = mods[i]
        if mix == 0:
            xc = _deltanet_layer(xc, mod, norm_mix_g[i], dn_w_in[slot], dn_conv_w[slot], dn_a_log[slot],
                                 dn_dt_bias[slot], dn_norm_g[slot], dn_w_out[slot], ctx_len)
        elif mix == 1:
            xc = _gla_layer(xc, mod, norm_mix_g[i], gla_w_in[slot], gla_gate_w2[slot], gla_gate_b2[slot],
                            gla_norm_g[slot], gla_w_out[slot], ctx_len)
        else:
            xc = _attention_layer(xc, mod, norm_mix_g[i], attn_w_in[slot], attn_q_norm_g[slot],
                                  attn_k_norm_g[slot], attn_w_out[slot], rope, ctx_len)
        xc = _ffn(xc, mod, norm_ffn_g[i], ffn_w1[i].astype(BF16), ffn_w2[i].astype(BF16), ctx_len)
    return xc[:, ctx_len:, :]
```

```python
import functools
import math

import numpy as np
import jax
import jax.numpy as jnp
from jax import lax
from jax.experimental import pallas as pl
from jax.experimental.pallas import tpu as pltpu

F32 = jnp.float32
BF16 = jnp.bfloat16

NORM_EPS = 1e-6
GRID_W = 64
ROPE_THETA = 10000.0
SHORT_CONV = 5

DN_K_HEADS = 8
DN_V_HEADS = 16
DN_HEAD = 128
DN_GROUP = DN_V_HEADS // DN_K_HEADS
DN_K_DIM = DN_K_HEADS * DN_HEAD
DN_V_DIM = DN_V_HEADS * DN_HEAD
DN_QKV_DIM = 2 * DN_K_DIM + DN_V_DIM
DN_KH_STEP = 8

GLA_HEADS = 4
GLA_HEAD_K = 128
GLA_HEAD_V = 256
GLA_K_DIM = GLA_HEADS * GLA_HEAD_K
GLA_V_DIM = GLA_HEADS * GLA_HEAD_V
GLA_GATE_RANK = 16
GLA_GATE_NORMALIZER = 16.0
GLA_H_STEP = 4

ATTN_Q_HEADS = 8
ATTN_KV_HEADS = 2
ATTN_HEAD = 128
ATTN_GROUP = ATTN_Q_HEADS // ATTN_KV_HEADS

CHUNK = 64
SCAN_BLOCK = 256
LANES = 128
MXU_CHUNK = 1024
MOD_ROWS = 8
VMEM_LIMIT = 56 * 1024 * 1024


def _params(semantics, vmem=VMEM_LIMIT):
    return pltpu.CompilerParams(dimension_semantics=semantics, vmem_limit_bytes=vmem)


def _sigmoid(x):
    return 1.0 / (1.0 + jnp.exp(-x))


def _softplus(x):
    return jnp.maximum(x, 0.0) + jnp.log(1.0 + jnp.exp(-jnp.abs(x)))


def _split3(x):
    hi = x.astype(BF16)
    r1 = x - hi.astype(F32)
    mid = r1.astype(BF16)
    lo = (r1 - mid.astype(F32)).astype(BF16)
    return hi, mid, lo


def _dot(a, b):
    return jnp.dot(a.astype(BF16), b.astype(BF16), preferred_element_type=F32)


def _dot_sel(p_bf16, x):
    hi, mid, lo = _split3(x)
    d = lambda y: jnp.dot(p_bf16, y, preferred_element_type=F32)
    return d(hi) + d(mid) + d(lo)


def _row_tile(total, target, multiple=8):
    best = None
    for t in range(multiple, min(total, target) + 1, multiple):
        if total % t == 0:
            best = t
    assert best is not None, (total, target, multiple)
    return best


def _mod_norm(x, g, mod_ref, b, row0, ctx_len, nb, shift_idx, scale_idx):
    d = x.shape[-1]
    ms = jnp.mean(x * x, axis=-1, keepdims=True)
    y = x * lax.rsqrt(ms + NORM_EPS) * g
    rows = row0 + lax.broadcasted_iota(jnp.int32, (x.shape[0], 1), 0)
    is_ctx = rows < ctx_len

    def pick(idx):
        vx = mod_ref[pl.ds(b, 1), idx * d:(idx + 1) * d]
        vc = mod_ref[nb:nb + 1, idx * d:(idx + 1) * d]
        return jnp.where(is_ctx, vc, vx)

    return y * (1.0 + pick(scale_idx)) + pick(shift_idx), is_ctx, pick


def _ada_kernel(c_ref, w_ref, b_ref, o_ref):
    c = c_ref[...]
    s = c * _sigmoid(c)
    o_ref[0] = jnp.dot(s, w_ref[0], preferred_element_type=F32,
                       precision=lax.Precision.HIGHEST) + b_ref[0]


def _ada_vectors(cvec, ada_w, ada_b):
    depth, d, n = ada_w.shape
    tn = _row_tile(n, 1536, LANES)
    return pl.pallas_call(
        _ada_kernel,
        grid=(depth, n // tn),
        in_specs=[pl.BlockSpec((MOD_ROWS, d), lambda l, j: (0, 0)),
                  pl.BlockSpec((1, d, tn), lambda l, j: (l, 0, j)),
                  pl.BlockSpec((1, 1, tn), lambda l, j: (l, 0, j))],
        out_specs=pl.BlockSpec((1, MOD_ROWS, tn), lambda l, j: (l, 0, j)),
        out_shape=jax.ShapeDtypeStruct((depth, MOD_ROWS, n), F32),
        compiler_params=_params(("arbitrary", "arbitrary")),
        name="ada_vectors",
    )(cvec, ada_w, ada_b.reshape(depth, 1, n))


def _inproj_kernel(x_ref, mod_ref, g_ref, w_ref, *o_refs, tm, tn, ctx_len, nb):
    b = pl.program_id(0)
    i = pl.program_id(1)
    h, _, _ = _mod_norm(x_ref[0], g_ref[...], mod_ref, b, i * tm, ctx_len, nb, 0, 1)
    h = h.astype(BF16)
    first = 0
    for o_ref in o_refs:
        n = o_ref.shape[2]
        for start in range(0, n, tn):
            stop = min(start + tn, n)
            res = jnp.dot(h, w_ref[:, first + start:first + stop], preferred_element_type=F32)
            o_ref[0, :, start:stop] = res.astype(o_ref.dtype)
        first += n


def _in_projection(xc, mod, g, w_bf16, tn, ctx_len, bf16_cols=0):
    nb, t, d = xc.shape
    n = w_bf16.shape[1]
    tm = _row_tile(t, 640)
    assert bf16_cols % tn == 0
    widths = [(bf16_cols, BF16)] * (bf16_cols > 0) + [(n - bf16_cols, F32)]
    out = pl.pallas_call(
        functools.partial(_inproj_kernel, tm=tm, tn=tn, ctx_len=ctx_len, nb=nb),
        grid=(nb, t // tm),
        in_specs=[pl.BlockSpec((1, tm, d), lambda b, i: (b, i, 0)),
                  pl.BlockSpec(mod.shape, lambda b, i: (0, 0)),
                  pl.BlockSpec((1, d), lambda b, i: (0, 0)),
                  pl.BlockSpec((d, n), lambda b, i: (0, 0), pipeline_mode=pl.Buffered(1))],
        out_specs=[pl.BlockSpec((1, tm, w), lambda b, i: (b, i, 0)) for w, _ in widths],
        out_shape=[jax.ShapeDtypeStruct((nb, t, w), dt) for w, dt in widths],
        compiler_params=_params(("arbitrary", "arbitrary")),
        name="in_projection",
    )(xc, mod, g.reshape(1, d), w_bf16)
    return out if bf16_cols else out[0]


def _outproj_kernel(*refs, n_o, gated, heads, head_dim, tm, ctx_len, nb):
    o_refs = refs[:n_o]
    pos = n_o
    if gated:
        z_ref, ng_ref = refs[pos], refs[pos + 1]
        pos += 2
    w_ref, x_ref, mod_ref, out_ref = refs[pos:pos + 4]
    b = pl.program_id(0)
    i = pl.program_id(1)
    if gated:
        pieces = []
        for h in range(heads):
            sl = slice(h * head_dim, (h + 1) * head_dim)
            o = o_refs[0][0, :, sl].astype(F32)
            for r in o_refs[1:]:
                o = o + r[0, :, sl].astype(F32)
            ms = jnp.mean(o * o, axis=-1, keepdims=True)
            o = o * lax.rsqrt(ms + NORM_EPS) * ng_ref[...]
            z = z_ref[0, :, sl].astype(F32)
            pieces.append((o * (z * _sigmoid(z))).astype(BF16))
        lhs = jnp.concatenate(pieces, axis=-1)
    else:
        lhs = o_refs[0][0]
    y = jnp.dot(lhs, w_ref[...], preferred_element_type=F32)
    d = y.shape[-1]
    rows = i * tm + lax.broadcasted_iota(jnp.int32, (tm, 1), 0)
    gate = jnp.where(rows < ctx_len, mod_ref[nb:nb + 1, 2 * d:3 * d], mod_ref[pl.ds(b, 1), 2 * d:3 * d])
    out_ref[0] = x_ref[0] + gate * y


def _out_projection(o_list, z_src, z_col_block, norm_g, heads, head_dim, w_bf16, xc, mod, ctx_len, tm_target):
    nb, t, d = xc.shape
    dv = w_bf16.shape[0]
    tm = _row_tile(t, tm_target)
    gated = z_src is not None
    row_spec = lambda width, col: pl.BlockSpec((1, tm, width), lambda b, i: (b, i, col))
    in_specs = [row_spec(dv, 0) for _ in o_list]
    args = list(o_list)
    if gated:
        in_specs += [row_spec(dv, z_col_block), pl.BlockSpec((1, head_dim), lambda b, i: (0, 0))]
        args += [z_src, norm_g.reshape(1, head_dim)]
    in_specs += [pl.BlockSpec((dv, d), lambda b, i: (0, 0)), row_spec(d, 0),
                 pl.BlockSpec(mod.shape, lambda b, i: (0, 0))]
    args += [w_bf16, xc, mod]
    return pl.pallas_call(
        functools.partial(_outproj_kernel, n_o=len(o_list), gated=gated, heads=heads, head_dim=head_dim,
                          tm=tm, ctx_len=ctx_len, nb=nb),
        grid=(nb, t // tm),
        in_specs=in_specs,
        out_specs=row_spec(d, 0),
        out_shape=jax.ShapeDtypeStruct((nb, t, d), F32),
        compiler_params=_params(("arbitrary", "arbitrary")),
        name="out_projection",
    )(*args)


def _ffn_kernel(x_ref, mod_ref, g_ref, w1_ref, w2_ref, o_ref, acc_ref, *, tm, fk, ctx_len, nb):
    b = pl.program_id(0)
    i = pl.program_id(1)
    x = x_ref[0]
    h, _, pick = _mod_norm(x, g_ref[...], mod_ref, b, i * tm, ctx_len, nb, 3, 4)
    h = h.astype(BF16)
    d_ff = w1_ref.shape[1]
    for k in range(d_ff // fk):
        u = jnp.dot(h, w1_ref[:, k * fk:(k + 1) * fk], preferred_element_type=F32)
        u = jnp.maximum(u, 0.0)
        u = (u * u).astype(BF16)
        contrib = jnp.dot(u, w2_ref[k * fk:(k + 1) * fk, :], preferred_element_type=F32)
        if k == 0:
            acc_ref[...] = contrib
        else:
            acc_ref[...] += contrib
    o_ref[0] = x + pick(5) * acc_ref[...]


def _ffn(xc, mod, g, w1_bf16, w2_bf16, ctx_len):
    nb, t, d = xc.shape
    d_ff = w1_bf16.shape[1]
    tm = _row_tile(t, 640)
    return pl.pallas_call(
        functools.partial(_ffn_kernel, tm=tm, fk=512, ctx_len=ctx_len, nb=nb),
        grid=(nb, t // tm),
        in_specs=[pl.BlockSpec((1, tm, d), lambda b, i: (b, i, 0)),
                  pl.BlockSpec(mod.shape, lambda b, i: (0, 0)),
                  pl.BlockSpec((1, d), lambda b, i: (0, 0)),
                  pl.BlockSpec((d, d_ff), lambda b, i: (0, 0)),
                  pl.BlockSpec((d_ff, d), lambda b, i: (0, 0))],
        out_specs=pl.BlockSpec((1, tm, d), lambda b, i: (b, i, 0)),
        out_shape=jax.ShapeDtypeStruct((nb, t, d), F32),
        scratch_shapes=[pltpu.VMEM((tm, d), F32)],
        compiler_params=_params(("arbitrary", "arbitrary")),
        name="ffn",
    )(xc, mod, g.reshape(1, d), w1_bf16, w2_bf16)


HALO = 8
CONV_ROW_PARTS = 4


def _dn_qkv_kernel(x_ref, xp_ref, xn_ref, mod_ref, g_ref, w_ref, cw_ref, o_ref, *,
                   tm, t_total, ctx_len, nb, fix_tile, fix_row):
    b = pl.program_id(0)
    i = pl.program_id(1)
    row0 = i * tm
    n_ext = tm + 2 * HALO
    pad = SHORT_CONV // 2
    x_ext = jnp.concatenate([xp_ref[0], x_ref[0], xn_ref[0]], axis=0)
    h, _, _ = _mod_norm(x_ext, g_ref[...], mod_ref, b, row0 - HALO, ctx_len, nb, 0, 1)
    t = row0 - HALO + lax.broadcasted_iota(jnp.int32, (n_ext, 1), 0)
    nearest = jnp.clip(t, row0, row0 + tm - 1)
    valid = (jnp.where(t >= 0, 1, 0) * jnp.where(t < t_total, 1, 0)
             * jnp.where(jnp.where(t >= ctx_len, 1, 0) == jnp.where(nearest >= ctx_len, 1, 0), 1, 0))
    h = jnp.where(valid > 0, h, 0.0).astype(BF16)
    if fix_row is not None:
        rr = lax.broadcasted_iota(jnp.int32, (2 * HALO, 1), 0)
        crosses = {}
        for d in range(-pad, pad + 1):
            if d:
                inside = jnp.where(rr + d >= 0, 1, 0) * jnp.where(rr + d < 2 * HALO, 1, 0)
                other = jnp.where(jnp.where(rr < HALO, 1, 0) != jnp.where(rr + d < HALO, 1, 0), 1, 0)
                crosses[d] = jnp.where(i == fix_tile, inside * other, 0) > 0
    n = w_ref.shape[1]
    for start in range(0, n, MXU_CHUNK):
        res = jnp.dot(h, w_ref[:, start:start + MXU_CHUNK], preferred_element_type=F32)
        for hd in range(MXU_CHUNK // DN_HEAD):
            col0 = start + hd * DN_HEAD
            cols = slice(col0, col0 + DN_HEAD)
            r = res[:, hd * DN_HEAD:(hd + 1) * DN_HEAD]
            shifted = [r if tap == pad else pltpu.roll(r, (pad - tap) % n_ext, 0) for tap in range(SHORT_CONV)]
            part = tm // CONV_ROW_PARTS
            for lo in range(0, tm, part):
                acc = None
                for tap in range(SHORT_CONV):
                    term = shifted[tap][HALO + lo:HALO + lo + part] * cw_ref[tap:tap + 1, cols]
                    acc = term if acc is None else acc + term
                if fix_row is not None and lo <= fix_row - HALO and fix_row + HALO <= lo + part:
                    slab = r[fix_row:fix_row + 2 * HALO]
                    wrong = None
                    for d, mask in crosses.items():
                        term = jnp.where(mask, pltpu.roll(slab, (-d) % (2 * HALO), 0), 0.0) * cw_ref[d + pad:d + pad + 1, cols]
                        wrong = term if wrong is None else wrong + term
                    at = fix_row - HALO - lo
                    acc = jnp.concatenate([acc[:at], acc[at:at + 2 * HALO] - wrong, acc[at + 2 * HALO:]], axis=0)
                half = 0.5 * acc
                y = half + half * jnp.tanh(half)
                if col0 < 2 * DN_K_DIM:
                    ss = jnp.sum(y * y, axis=-1, keepdims=True)
                    y = y * (lax.rsqrt(ss + NORM_EPS) * (DN_HEAD ** -0.5 if col0 < DN_K_DIM else 1.0))
                o_ref[0, lo:lo + part, cols] = y


def _dn_qkv(xc, mod, g, w_qkv_bf16, conv_w_t, ctx_len):
    nb, t, d = xc.shape
    n = w_qkv_bf16.shape[1]
    tm = _row_tile(t, 640)
    hb = tm // HALO
    last = t // HALO - 1
    fix_tile, fix_row = (ctx_len // tm, ctx_len % tm) if ctx_len % tm else (None, None)
    assert fix_row is None or (fix_row % HALO == 0 and HALO <= fix_row <= tm - HALO)
    part = tm // CONV_ROW_PARTS
    assert tm % CONV_ROW_PARTS == 0 and part % HALO == 0
    assert fix_row is None or (fix_row - HALO) // part == (fix_row + HALO - 1) // part
    return pl.pallas_call(
        functools.partial(_dn_qkv_kernel, tm=tm, t_total=t, ctx_len=ctx_len, nb=nb,
                          fix_tile=fix_tile, fix_row=fix_row),
        grid=(nb, t // tm),
        in_specs=[pl.BlockSpec((1, tm, d), lambda b, i: (b, i, 0)),
                  pl.BlockSpec((1, HALO, d), lambda b, i: (b, jnp.maximum(i * hb - 1, 0), 0)),
                  pl.BlockSpec((1, HALO, d), lambda b, i: (b, jnp.minimum((i + 1) * hb, last), 0)),
                  pl.BlockSpec(mod.shape, lambda b, i: (0, 0)),
                  pl.BlockSpec((1, d), lambda b, i: (0, 0)),
                  pl.BlockSpec((d, n), lambda b, i: (0, 0), pipeline_mode=pl.Buffered(1)),
                  pl.BlockSpec((8, n), lambda b, i: (0, 0))],
        out_specs=pl.BlockSpec((1, tm, n), lambda b, i: (b, i, 0)),
        out_shape=jax.ShapeDtypeStruct((nb, t, n), F32),
        compiler_params=_params(("arbitrary", "arbitrary")),
        name="dn_qkv",
    )(xc, xc, xc, mod, g.reshape(1, d), w_qkv_bf16, conv_w_t)


def _dn_gate_kernel(ab_ref, par_ref, o_ref, *, tm):
    x = ab_ref[0]
    g = -jnp.exp(par_ref[0:1, :]) * _softplus(x + par_ref[1:2, :])
    beta = _sigmoid(x)
    lane = lax.broadcasted_iota(jnp.int32, (1, LANES), 1)
    used = lane < 4 * DN_V_HEADS
    is_beta = (lane & (2 * DN_GROUP)) != 0
    is_reverse = (lane & DN_GROUP) != 0
    r = lax.broadcasted_iota(jnp.int32, (CHUNK, CHUNK), 0)
    c = lax.broadcasted_iota(jnp.int32, (CHUNK, CHUNK), 1)
    lower = jnp.where(c <= r, 1.0, 0.0).astype(BF16)
    upper = jnp.where(c >= r, 1.0, 0.0).astype(BF16)
    for k in range(tm // CHUNK):
        rows = slice(k * CHUNK, (k + 1) * CHUNK)
        gk = g[rows]
        fwd = _dot_sel(lower, gk)
        bwd = _dot_sel(upper, gk)
        o_ref[0, rows, :] = jnp.where(used, jnp.where(is_beta, beta[rows], jnp.where(is_reverse, bwd, fwd)), 0.0)


def _dn_gate_lane_perm():
    perm = np.zeros(4 * DN_V_HEADS, np.int32)
    for kh in range(DN_K_HEADS):
        for kind in range(2):
            for d in range(2):
                for j in range(DN_GROUP):
                    lane = ((kh * 2 + kind) * 2 + d) * DN_GROUP + j
                    perm[lane] = (kind * 2 + d) * DN_V_HEADS + kh * DN_GROUP + j
    return perm


def _dn_gates(p, ab_col_block, a_log, dt_bias):
    nb, t, _ = p.shape
    tm = _row_tile(t, 1280, CHUNK)
    perm = _dn_gate_lane_perm()
    on_lanes = lambda v: jnp.concatenate([v.reshape(-1).astype(F32), jnp.zeros(2 * DN_V_HEADS, F32)])[perm]
    par = jnp.zeros((8, LANES), F32)
    par = par.at[0, :4 * DN_V_HEADS].set(on_lanes(a_log))
    par = par.at[1, :4 * DN_V_HEADS].set(on_lanes(dt_bias))
    return pl.pallas_call(
        functools.partial(_dn_gate_kernel, tm=tm),
        grid=(nb, t // tm),
        in_specs=[pl.BlockSpec((1, tm, LANES), lambda b, i: (b, i, ab_col_block)),
                  pl.BlockSpec((8, LANES), lambda b, i: (0, 0))],
        out_specs=pl.BlockSpec((1, tm, LANES), lambda b, i: (b, i, 0)),
        out_shape=jax.ShapeDtypeStruct((nb, t, LANES), F32),
        compiler_params=_params(("arbitrary", "arbitrary")),
        name="dn_gates",
    )(p, par)


def _bdot(a, b):
    return jnp.einsum("nij,njk->nik", a.astype(BF16), b.astype(BF16), preferred_element_type=F32)


def _bdot_nt(a, b):
    return jnp.einsum("nid,njd->nij", a.astype(BF16), b.astype(BF16), preferred_element_type=F32)


def _bdot_tn(a, b):
    return jnp.einsum("nci,ncj->nij", a.astype(BF16), b.astype(BF16), preferred_element_type=F32)


N_LEVELS = int(math.log2(CHUNK))


def _unit_triangular_inverse(parts, eye):
    coupling = lambda lv: jnp.concatenate([jnp.where(masks[lv], m, 0.0) for m, masks in parts], axis=0)
    x = eye - coupling(0)
    for lv in range(1, N_LEVELS):
        x = x - _bdot(x, _bdot(coupling(lv), x))
    return x


def _coupling_masks(ri, ci, reverse):
    ti, tj = (CHUNK - 1 - ri, CHUNK - 1 - ci) if reverse else (ri, ci)
    masks = []
    for lv in range(N_LEVELS):
        bi, bj = lax.shift_right_logical(ti, lv), lax.shift_right_logical(tj, lv)
        masks.append(jnp.where((bi & 1) == 1, bi - 1, -1) == bj)
    return masks


def _dn_scan_kernel(qf_ref, kf_ref, vf_ref, gf_ref, gtf_ref,
                    qr_ref, kr_ref, vr_ref, gr_ref, gtr_ref,
                    of_ref, ob_ref, s_ref):
    @pl.when(pl.program_id(2) == 0)
    def _():
        s_ref[...] = jnp.zeros_like(s_ref)

    ri = lax.broadcasted_iota(jnp.int32, (CHUNK, CHUNK), 0)
    ci = lax.broadcasted_iota(jnp.int32, (CHUNK, CHUNK), 1)
    eye = jnp.where(ri == ci, 1.0, 0.0)
    nc = SCAN_BLOCK // CHUNK
    chunk_rows = [slice(c * CHUNK, (c + 1) * CHUNK) for c in range(nc)]
    dirs = ((qf_ref, kf_ref, vf_ref, gf_ref, gtf_ref, False),
            (qr_ref, kr_ref, vr_ref, gr_ref, gtr_ref, True))
    m_parts, rhs_l, a_l, qg_l, kt_l, egl_l = [], [], [], [], [], []
    head_cols = lambda n: slice(n * DN_HEAD, (n + 1) * DN_HEAD)
    for d, (q_ref, k_ref, v_ref, g_ref, gt_ref, reverse) in enumerate(dirs):
        incl = (ci >= ri) if reverse else (ci <= ri)
        strict = (ci > ri) if reverse else (ci < ri)
        last = 0 if reverse else CHUNK - 1
        m_l = []
        per_head = 4 * DN_GROUP
        first_lane = pl.program_id(1) * (DN_KH_STEP * per_head)
        g_all = pltpu.roll(g_ref[0], jnp.where(first_lane == 0, 0, LANES - first_lane), 1)
        for hh in range(DN_KH_STEP):
            q = jnp.stack([q_ref[0, r, head_cols(hh)] for r in chunk_rows])
            k = jnp.stack([k_ref[0, r, head_cols(hh)] for r in chunk_rows])
            qk_kk = _bdot_nt(jnp.concatenate([q, k], axis=1), k)
            qk, kk = qk_kk[:, :CHUNK], qk_kk[:, CHUNK:]
            for j in range(DN_GROUP):
                col = 2 * d + j
                lane = hh * per_head + col
                gc = jnp.stack([g_all[r, lane:lane + 1] for r in chunk_rows])
                bc = jnp.stack([g_all[r, lane + 2 * DN_GROUP:lane + 2 * DN_GROUP + 1] for r in chunk_rows])
                gr = jnp.stack([gt_ref[0, hh, c, col:col + 1, :] for c in range(nc)])
                v = jnp.stack([v_ref[0, r, head_cols(hh * DN_GROUP + j)] for r in chunk_rows])
                decay = jnp.where(incl, jnp.exp(jnp.where(incl, gc - gr, 0.0)), 0.0)
                eg = jnp.exp(gc)
                gl = gc[:, last:last + 1, :]
                m_l.append(jnp.where(strict, bc * kk * decay, 0.0))
                rhs_l.append(jnp.concatenate([k * (bc * eg), v * bc], axis=-1))
                a_l.append(qk * decay)
                qg_l.append(q * eg)
                kt_l.append(k * jnp.exp(gl - gc))
                egl_l.append(jnp.exp(gl))
        m_parts.append((jnp.concatenate(m_l, axis=0), _coupling_masks(ri, ci, reverse)))
    cat = lambda xs: jnp.concatenate(xs, axis=0)
    a, kt = cat(a_l), cat(kt_l)
    wu = _bdot(_unit_triangular_inverse(m_parts, eye), cat(rhs_l))
    kb = _bdot_tn(kt, wu)
    qo = _bdot(a, wu)
    qeff = cat(qg_l) - qo[:, :, :DN_HEAD]
    egl = cat(egl_l)
    state = s_ref[...]
    o_refs = (of_ref, ob_ref)
    vh_step = DN_KH_STEP * DN_GROUP
    for step in range(nc):
        chunk_of = [step if d == 0 else nc - 1 - step for d in range(2) for _ in range(vh_step)]
        idx = [ch * nc + c for ch, c in enumerate(chunk_of)]
        pick = lambda x: jnp.stack([x[n] for n in idx])
        kb_s = pick(kb)
        qs_ks = _bdot(jnp.concatenate([pick(qeff), kb_s[:, :, :DN_HEAD]], axis=1), state)
        o = qs_ks[:, :CHUNK] + pick(qo)[:, :, DN_HEAD:]
        state = pick(egl) * state + kb_s[:, :, DN_HEAD:] - qs_ks[:, CHUNK:]
        for ch, c in enumerate(chunk_of):
            o_refs[ch // vh_step][0, chunk_rows[c], head_cols(ch % vh_step)] = o[ch].astype(BF16)
    s_ref[...] = state


def _dn_scan(qkv, gates, g_rows):
    nb, t, _ = qkv.shape
    n_blocks = t // SCAN_BLOCK
    n_chunks = SCAN_BLOCK // CHUNK
    ks = DN_KH_STEP
    groups = DN_K_HEADS // ks
    fwd = lambda s: s
    bwd = lambda s: jnp.where(s == 0, 0, n_blocks - s)

    def specs(order):
        return [pl.BlockSpec((1, SCAN_BLOCK, ks * DN_HEAD), lambda b, h, s: (b, order(s), h)),
                pl.BlockSpec((1, SCAN_BLOCK, ks * DN_HEAD), lambda b, h, s: (b, order(s), groups + h)),
                pl.BlockSpec((1, SCAN_BLOCK, ks * DN_GROUP * DN_HEAD), lambda b, h, s: (b, order(s), groups + h)),
                pl.BlockSpec((1, SCAN_BLOCK, LANES), lambda b, h, s: (b, order(s), 0)),
                pl.BlockSpec((1, ks, n_chunks, 8, CHUNK), lambda b, h, s: (b, h, order(s), 0, 0))]

    out_spec = lambda order: pl.BlockSpec((1, SCAN_BLOCK, ks * DN_GROUP * DN_HEAD), lambda b, h, s: (b, order(s), h))
    o_shape = jax.ShapeDtypeStruct((nb, t, DN_V_DIM), BF16)
    return pl.pallas_call(
        _dn_scan_kernel,
        grid=(nb, groups, n_blocks),
        in_specs=specs(fwd) + specs(bwd),
        out_specs=[out_spec(fwd), out_spec(bwd)],
        out_shape=[o_shape, o_shape],
        scratch_shapes=[pltpu.VMEM((2 * ks * DN_GROUP, DN_HEAD, DN_HEAD), F32)],
        compiler_params=_params(("arbitrary", "arbitrary", "arbitrary")),
        name="dn_scan",
    )(qkv, qkv, qkv, gates, g_rows, qkv, qkv, qkv, gates, g_rows)


_GLA_LEVELS = (32, 16, 8, 4, 2, 1)


def _gla_tables(reverse):
    idx = np.arange(CHUNK)
    tau = (CHUNK - 1 - idx) if reverse else idx
    ti, tk = tau[:, None], tau[None, :]
    groups = [tk <= ti, tk > ti]
    masks = []
    for s in _GLA_LEVELS:
        bi, bk = ti // s, tk // s
        if s > 1:
            groups.append((bk == bi) & (tk <= ti) & (tk > bi * s))
        groups.append(((bk == bi) & (tk > ti)) | (tk == (bi + 1) * s))
        masks.append((bi % 2 == 1) & (bk == bi - 1))
    masks.append(ti == tk)
    sel = np.concatenate(groups, axis=0).astype(np.float32)
    return jnp.asarray(sel, BF16), jnp.asarray(np.stack(masks).astype(np.float32))


def _gla_scan_kernel(qf_ref, kf_ref, vf_ref, lf_ref, wf_ref, bf_ref, self_ref, mf_ref,
                     qr_ref, kr_ref, vr_ref, lr_ref, wr_ref, br_ref, selr_ref, mr_ref,
                     of_ref, ob_ref, s_ref):
    @pl.when(pl.program_id(2) == 0)
    def _():
        s_ref[...] = jnp.zeros_like(s_ref)

    nc = SCAN_BLOCK // CHUNK
    n_lev = len(_GLA_LEVELS)
    chunk_rows = [slice(c * CHUNK, (c + 1) * CHUNK) for c in range(nc)]
    dirs = ((qf_ref, kf_ref, vf_ref, lf_ref, wf_ref, bf_ref, self_ref, mf_ref, False),
            (qr_ref, kr_ref, vr_ref, lr_ref, wr_ref, br_ref, selr_ref, mr_ref, True))
    hs = GLA_H_STEP
    kcols = lambda hh: slice(hh * GLA_HEAD_K, (hh + 1) * GLA_HEAD_K)
    vcols = lambda hh: slice(hh * GLA_HEAD_V, (hh + 1) * GLA_HEAD_V)
    ql, kl, qg_l, kt_l, e_l, v_l = [], [], [], [], [], []
    for q_ref, k_ref, v_ref, l_ref, w_ref, b_ref, sel_ref, m_ref, reverse in dirs:
        last = 0 if reverse else CHUNK - 1
        logits = _dot(l_ref[0], w_ref[...]) + b_ref[...]
        gk_all = (jnp.minimum(logits, 0.0) - jnp.log(1.0 + jnp.exp(-jnp.abs(logits)))) * (1.0 / GLA_GATE_NORMALIZER)
        for hh in range(hs):
            for rows in chunk_rows:
                q = q_ref[0, rows, kcols(hh)] * (GLA_HEAD_K ** -0.5)
                k = k_ref[0, rows, kcols(hh)]
                gk = gk_all[rows, kcols(hh)]
                hi = gk.astype(BF16)
                mid = (gk - hi.astype(F32)).astype(BF16)
                both = jnp.dot(sel_ref[...], jnp.concatenate([hi, mid], axis=-1), preferred_element_type=F32)
                sums = both[:, :GLA_HEAD_K] + both[:, GLA_HEAD_K:]
                part = lambda n: sums[n * CHUNK:(n + 1) * CHUNK]
                bcum, tail = part(0), part(1)
                ql += [q * jnp.exp(part(2 + 2 * lv)) for lv in range(n_lev - 1)] + [q, q]
                kl += [k * jnp.exp(part(3 + 2 * lv)) for lv in range(n_lev - 1)] + [k * jnp.exp(part(2 * n_lev)), k]
                qg_l.append(q * jnp.exp(bcum))
                kt_l.append(k * jnp.exp(tail))
                e_l.append(jnp.exp(bcum[last:last + 1, :]))
                v_l.append(v_ref[0, rows, vcols(hh)])
    scores = _bdot_nt(jnp.stack(ql), jnp.stack(kl))
    a_l = []
    for n in range(2 * hs * nc):
        m_ref = dirs[n // (hs * nc)][7]
        a = m_ref[0] * scores[n * (n_lev + 1)]
        for lv in range(1, n_lev + 1):
            a = a + m_ref[lv] * scores[n * (n_lev + 1) + lv]
        a_l.append(a)
    v = jnp.stack(v_l)
    x = _bdot_tn(v, jnp.stack(kt_l))
    s_l = [None] * (2 * hs * nc)
    for ch in range(2 * hs):
        state = s_ref[ch]
        for c in (range(nc - 1, -1, -1) if dirs[ch // hs][8] else range(nc)):
            n = ch * nc + c
            s_l[n] = state
            state = state * e_l[n] + x[n]
        s_ref[ch] = state
    o = _bdot(jnp.stack(a_l), v) + _bdot_nt(jnp.stack(qg_l), jnp.stack(s_l))
    for n in range(2 * hs * nc):
        ch, c = divmod(n, nc)
        (of_ref, ob_ref)[ch // hs][0, chunk_rows[c], vcols(ch % hs)] = o[n].astype(BF16)


def _gla_scan(p, w2cat, b2cat):
    nb, t, _ = p.shape
    n_blocks = t // SCAN_BLOCK
    hs = GLA_H_STEP
    h = GLA_HEADS // hs
    low_block = (2 * GLA_K_DIM + 2 * GLA_V_DIM) // LANES
    fwd = lambda s: s
    bwd = lambda s: jnp.where(s == 0, 0, n_blocks - s)
    assert _GLA_LEVELS[-1] == 1
    n_sel = (1 + 2 * len(_GLA_LEVELS)) * CHUNK
    n_mask = len(_GLA_LEVELS) + 1

    def specs(order, d):
        return [pl.BlockSpec((1, SCAN_BLOCK, hs * GLA_HEAD_K), lambda b, hh, s: (b, order(s), hh)),
                pl.BlockSpec((1, SCAN_BLOCK, hs * GLA_HEAD_K), lambda b, hh, s: (b, order(s), h + hh)),
                pl.BlockSpec((1, SCAN_BLOCK, hs * GLA_HEAD_V), lambda b, hh, s: (b, order(s), h + hh)),
                pl.BlockSpec((1, SCAN_BLOCK, LANES), lambda b, hh, s: (b, order(s), low_block)),
                pl.BlockSpec((LANES, hs * GLA_HEAD_K), lambda b, hh, s: (0, d * h + hh)),
                pl.BlockSpec((1, hs * GLA_HEAD_K), lambda b, hh, s: (0, d * h + hh)),
                pl.BlockSpec((n_sel, CHUNK), lambda b, hh, s: (0, 0)),
                pl.BlockSpec((n_mask, CHUNK, CHUNK), lambda b, hh, s: (0, 0, 0))]

    out_spec = lambda order: pl.BlockSpec((1, SCAN_BLOCK, hs * GLA_HEAD_V), lambda b, hh, s: (b, order(s), hh))
    o_shape = jax.ShapeDtypeStruct((nb, t, GLA_V_DIM), BF16)
    sel_f, mask_f = _gla_tables(False)
    sel_r, mask_r = _gla_tables(True)
    return pl.pallas_call(
        _gla_scan_kernel,
        grid=(nb, h, n_blocks),
        in_specs=specs(fwd, 0) + specs(bwd, 1),
        out_specs=[out_spec(fwd), out_spec(bwd)],
        out_shape=[o_shape, o_shape],
        scratch_shapes=[pltpu.VMEM((2 * hs, GLA_HEAD_V, GLA_HEAD_K), F32)],
        compiler_params=_params(("arbitrary", "arbitrary", "arbitrary")),
        name="gla_scan",
    )(p, p, p, p, w2cat, b2cat, sel_f, mask_f, p, p, p, p, w2cat, b2cat, sel_r, mask_r)


def _attn_prep_kernel(p_ref, qg_ref, kg_ref, cos_ref, sin_ref, q_ref, k_ref, v_ref):
    cos = cos_ref[...]
    sin = sin_ref[...]
    lane = lax.broadcasted_iota(jnp.int32, (1, ATTN_HEAD), 1)
    first = (lane % (ATTN_HEAD // 2)) < (ATTN_HEAD // 4)
    q_scale = ATTN_HEAD ** -0.5 * math.log2(math.e)

    def norm_rope(x, g):
        ms = jnp.mean(x * x, axis=-1, keepdims=True)
        y = x * lax.rsqrt(ms + NORM_EPS) * g
        partner = jnp.where(first, pltpu.roll(y, ATTN_HEAD - ATTN_HEAD // 4, 1), pltpu.roll(y, ATTN_HEAD // 4, 1))
        return y * cos + partner * sin

    qd = ATTN_Q_HEADS * ATTN_HEAD
    kd = ATTN_KV_HEADS * ATTN_HEAD
    q_pieces = [norm_rope(p_ref[0, :, h * ATTN_HEAD:(h + 1) * ATTN_HEAD], qg_ref[...]) * q_scale
                for h in range(ATTN_Q_HEADS)]
    q_ref[0] = jnp.concatenate(q_pieces, axis=-1).astype(BF16)
    k_pieces = [norm_rope(p_ref[0, :, qd + h * ATTN_HEAD:qd + (h + 1) * ATTN_HEAD], kg_ref[...])
                for h in range(ATTN_KV_HEADS)]
    k_ref[0] = jnp.concatenate(k_pieces, axis=-1).astype(BF16)
    v_ref[0] = p_ref[0, :, qd + kd:qd + 2 * kd].astype(BF16)


def _rope_tables(t, ctx_len):
    n_rows = (t - ctx_len) // GRID_W
    axis_dim = ATTN_HEAD // 2
    inv_freq = jnp.power(ROPE_THETA, -jnp.arange(0, axis_dim, 2, dtype=F32) / axis_dim)
    ar = jnp.arange(n_rows, dtype=F32)[:, None] * inv_freq
    ac = jnp.arange(GRID_W, dtype=F32)[:, None] * inv_freq
    on_rows = lambda x: jnp.broadcast_to(x[:, None, :], (n_rows, GRID_W, x.shape[-1]))
    on_cols = lambda x: jnp.broadcast_to(x[None, :, :], (n_rows, GRID_W, x.shape[-1]))
    cr, sr, cc, sc = on_rows(jnp.cos(ar)), on_rows(jnp.sin(ar)), on_cols(jnp.cos(ac)), on_cols(jnp.sin(ac))
    cos = jnp.concatenate([cr, cr, cc, cc], axis=-1).reshape(t - ctx_len, ATTN_HEAD)
    sin = jnp.concatenate([-sr, sr, -sc, sc], axis=-1).reshape(t - ctx_len, ATTN_HEAD)
    cos = jnp.concatenate([jnp.ones((ctx_len, ATTN_HEAD), F32), cos], axis=0)
    sin = jnp.concatenate([jnp.zeros((ctx_len, ATTN_HEAD), F32), sin], axis=0)
    return cos, sin


def _attn_prep(p, q_g, k_g, cos, sin):
    nb, t, n = p.shape
    tm = _row_tile(t, 640)
    qd = ATTN_Q_HEADS * ATTN_HEAD
    kd = ATTN_KV_HEADS * ATTN_HEAD
    row = lambda w: pl.BlockSpec((1, tm, w), lambda b, i: (b, i, 0))
    tab = pl.BlockSpec((tm, ATTN_HEAD), lambda b, i: (i, 0))
    vec = pl.BlockSpec((1, ATTN_HEAD), lambda b, i: (0, 0))
    return pl.pallas_call(
        _attn_prep_kernel,
        grid=(nb, t // tm),
        in_specs=[row(n), vec, vec, tab, tab],
        out_specs=[row(qd), row(kd), row(kd)],
        out_shape=[jax.ShapeDtypeStruct((nb, t, qd), BF16),
                   jax.ShapeDtypeStruct((nb, t, kd), BF16),
                   jax.ShapeDtypeStruct((nb, t, kd), BF16)],
        compiler_params=_params(("arbitrary", "arbitrary")),
        name="attn_prep",
    )(p, q_g.reshape(1, -1), k_g.reshape(1, -1), cos, sin)


def _flash_kernel(q_ref, k_ref, v_ref, o_ref, qs_ref, m_ref, l_ref, acc_ref, sa_ref, sb_ref, *, tq, tk, ctx_len, t_total):
    i = pl.program_id(2)
    g, hd = ATTN_GROUP, ATTN_HEAD
    for h in range(g):
        qs_ref[h * tq:(h + 1) * tq, :] = q_ref[0, :, h * hd:(h + 1) * hd]
    m_ref[...] = jnp.full_like(m_ref, -jnp.inf)
    l_ref[...] = jnp.zeros_like(l_ref)
    acc_ref[...] = jnp.zeros_like(acc_ref)

    def scores(start, width):
        kc = k_ref[0, pl.ds(start, width), :]
        return lax.dot_general(qs_ref[...], kc, (((1,), (1,)), ((), ())), preferred_element_type=F32)

    def update(s, start, width):
        vc = v_ref[0, pl.ds(start, width), :]
        m_old = m_ref[...]
        m_new = jnp.maximum(m_old, jnp.max(s, axis=-1, keepdims=True))
        alpha = jnp.exp2(m_old - m_new)
        p = jnp.exp2(s - jnp.tile(m_new, (1, width // LANES)))
        psum = p[:, 0:LANES]
        for n in range(1, width // LANES):
            psum = psum + p[:, n * LANES:(n + 1) * LANES]
        l_ref[...] = alpha * l_ref[...] + psum
        acc_ref[...] = alpha * acc_ref[...] + jnp.dot(p.astype(BF16), vc, preferred_element_type=F32)
        m_ref[...] = m_new

    @pl.when(i * tq < ctx_len)
    def _():
        update(scores(0, ctx_len), 0, ctx_len)

    n_kv = t_total // tk

    @pl.when(i * tq >= ctx_len)
    def _():
        sa_ref[...] = scores(0, tk)

        def pair(c):
            first = pl.multiple_of(2 * c * tk, tk)
            second = pl.multiple_of(first + tk, tk)
            third = pl.multiple_of(jnp.minimum(2 * c + 2, n_kv - 1) * tk, tk)
            sb_ref[...] = scores(second, tk)
            update(sa_ref[...], first, tk)
            sa_ref[...] = scores(third, tk)
            update(sb_ref[...], second, tk)

        n_pairs = n_kv // 2
        unroll = 6 if n_pairs % 6 == 1 else (4 if n_pairs % 4 == 1 else 2)

        def body(c, carry):
            for u in range(unroll):
                pair(unroll * c + u)
            return carry
        lax.fori_loop(0, n_pairs // unroll, body, 0)
        for c in range(n_pairs - n_pairs % unroll, n_pairs):
            pair(c)

    out = acc_ref[...] / jnp.sum(l_ref[...], axis=-1, keepdims=True)
    o_ref[0] = jnp.concatenate([out[h * tq:(h + 1) * tq] for h in range(g)], axis=-1).astype(BF16)


def _flash_attention(q, k, v, ctx_len):
    nb, t, qd = q.shape
    tq = 256
    tk = _row_tile(t, 640, LANES)
    assert ctx_len == tq and t % tq == 0 and ctx_len % LANES == 0 and (t // tk) % 2 == 0
    gw = ATTN_GROUP * ATTN_HEAD
    return pl.pallas_call(
        functools.partial(_flash_kernel, tq=tq, tk=tk, ctx_len=ctx_len, t_total=t),
        grid=(nb, ATTN_KV_HEADS, t // tq),
        in_specs=[pl.BlockSpec((1, tq, gw), lambda b, kv, i: (b, i, kv)),
                  pl.BlockSpec((1, t, ATTN_HEAD), lambda b, kv, i: (b, 0, kv)),
                  pl.BlockSpec((1, t, ATTN_HEAD), lambda b, kv, i: (b, 0, kv))],
        out_specs=pl.BlockSpec((1, tq, gw), lambda b, kv, i: (b, i, kv)),
        out_shape=jax.ShapeDtypeStruct((nb, t, qd), BF16),
        scratch_shapes=[pltpu.VMEM((ATTN_GROUP * tq, ATTN_HEAD), BF16),
                        pltpu.VMEM((ATTN_GROUP * tq, LANES), F32),
                        pltpu.VMEM((ATTN_GROUP * tq, LANES), F32),
                        pltpu.VMEM((ATTN_GROUP * tq, ATTN_HEAD), F32),
                        pltpu.VMEM((ATTN_GROUP * tq, tk), F32),
                        pltpu.VMEM((ATTN_GROUP * tq, tk), F32)],
        compiler_params=_params(("arbitrary", "arbitrary", "arbitrary")),
        name="flash_attention",
    )(q, k, v)


def _pad_cols(w, n):
    return jnp.pad(w, ((0, 0), (0, n - w.shape[1])))


def _deltanet_layer(xc, mod, norm_g, w_in, conv_w, a_log, dt_bias, out_norm_g, w_out, ctx_len):
    nb, t, _ = xc.shape
    gate_col = DN_QKV_DIM + DN_V_DIM
    conv_w_t = jnp.pad(conv_w.T.astype(F32), ((0, 8 - SHORT_CONV), (0, 0)))
    qkv = _dn_qkv(xc, mod, norm_g, w_in[:, :DN_QKV_DIM].astype(BF16), conv_w_t, ctx_len)
    w_gate = _pad_cols(w_in[:, gate_col:][:, _dn_gate_lane_perm()], LANES)
    w_rest = jnp.concatenate([w_in[:, DN_QKV_DIM:gate_col], w_gate], axis=1).astype(BF16)
    z, gate_logits = _in_projection(xc, mod, norm_g, w_rest, MXU_CHUNK, ctx_len, bf16_cols=DN_V_DIM)
    gb = _dn_gates(gate_logits, 0, a_log, dt_bias)
    per_head = 4 * DN_GROUP
    g_rows = gb[..., :DN_K_HEADS * per_head].reshape(nb, t // CHUNK, CHUNK, DN_K_HEADS, per_head)
    g_rows = g_rows.transpose(0, 3, 1, 4, 2)
    o_f, o_b = _dn_scan(qkv, gb, g_rows)
    return _out_projection([o_f, o_b], z, 0, out_norm_g, DN_V_HEADS, DN_HEAD,
                           w_out.astype(BF16), xc, mod, ctx_len, 640)


def _gla_layer(xc, mod, norm_g, w_in, gate_w2, gate_b2, out_norm_g, w_out, ctx_len):
    n_pad = 2 * GLA_K_DIM + 2 * GLA_V_DIM + LANES
    p = _in_projection(xc, mod, norm_g, _pad_cols(w_in, n_pad).astype(BF16), MXU_CHUNK, ctx_len)
    r = GLA_GATE_RANK
    w2cat = jnp.zeros((LANES, 2 * GLA_K_DIM), F32)
    w2cat = w2cat.at[0:r, :GLA_K_DIM].set(gate_w2[0]).at[r:2 * r, GLA_K_DIM:].set(gate_w2[1]).astype(BF16)
    b2cat = gate_b2.reshape(1, 2 * GLA_K_DIM).astype(F32)
    o_f, o_b = _gla_scan(p, w2cat, b2cat)
    z_block = (2 * GLA_K_DIM + GLA_V_DIM) // GLA_V_DIM
    return _out_projection([o_f, o_b], p, z_block, out_norm_g, GLA_HEADS, GLA_HEAD_V,
                           w_out.astype(BF16), xc, mod, ctx_len, 640)


def _attention_layer(xc, mod, norm_g, w_in, q_g, k_g, w_out, rope, ctx_len):
    p = _in_projection(xc, mod, norm_g, w_in.astype(BF16), MXU_CHUNK, ctx_len)
    q, k, v = _attn_prep(p, q_g, k_g, *rope)
    o = _flash_attention(q, k, v, ctx_len)
    return _out_projection([o], None, 0, None, 0, 0, w_out.astype(BF16), xc, mod, ctx_len, 1280)


def kernel(x, c, ctx, c_ctx, ada_w, ada_b, norm_mix_g, norm_ffn_g, ffn_w1, ffn_w2, dn_w_in, dn_conv_w, dn_a_log, dn_dt_bias, dn_norm_g, dn_w_out, gla_w_in, gla_gate_w2, gla_gate_b2, gla_norm_g, gla_w_out, attn_w_in, attn_q_norm_g, attn_k_norm_g, attn_w_out):
    nb, seq, d = x.shape
    ctx_len = ctx.shape[1]
    depth = ada_w.shape[0]
    assert ctx_len == SCAN_BLOCK and seq % SCAN_BLOCK == 0 and nb < MOD_ROWS
    t = ctx_len + seq
    xc = jnp.concatenate([ctx, x], axis=1)
    cvec = jnp.zeros((MOD_ROWS, d), F32).at[:nb].set(c).at[nb].set(c_ctx)
    mods = _ada_vectors(cvec, ada_w, ada_b)
    rope = _rope_tables(t, ctx_len)
    for i in range(depth):
        mix, slot = i % 3, i // 3
        mod = mods[i]
        if mix == 0:
            xc = _deltanet_layer(xc, mod, norm_mix_g[i], dn_w_in[slot], dn_conv_w[slot], dn_a_log[slot],
                                 dn_dt_bias[slot], dn_norm_g[slot], dn_w_out[slot], ctx_len)
        elif mix == 1:
            xc = _gla_layer(xc, mod, norm_mix_g[i], gla_w_in[slot], gla_gate_w2[slot], gla_gate_b2[slot],
                            gla_norm_g[slot], gla_w_out[slot], ctx_len)
        else:
            xc = _attention_layer(xc, mod, norm_mix_g[i], attn_w_in[slot], attn_q_norm_g[slot],
                                  attn_k_norm_g[slot], attn_w_out[slot], rope, ctx_len)
        xc = _ffn(xc, mod, norm_ffn_g[i], ffn_w1[i].astype(BF16), ffn_w2[i].astype(BF16), ctx_len)
    return xc[:, ctx_len:, :]
```
